```python
import math
import jax, jax.numpy as jnp
from jax import lax
import numpy as np

D_MODEL = 1024
BATCH = 16
SEQ = 2048
DEPTH = 2
DEC_BATCH = 128
DEC_SEQ = 4
PAST_LEN = 16384
PAGE_SIZE = 128

D_PLE = 256
RMS_EPS = 1e-6
D_MIX = D_MODEL
W_A = D_MIX // 4
W_B = D_MIX // 4
W_C = D_MIX // 4
W_D = D_MIX // 4

LRU_BLOCKS = 4
LRU_BLOCK = W_A // LRU_BLOCKS
LRU_CONV = 4
LRU_C = 8.0

SWA_HEADS = 4
SWA_KV_HEADS = 2
SWA_HD = W_B // SWA_HEADS
SWA_GROUP = SWA_HEADS // SWA_KV_HEADS
WINDOW = 128
REL_BUCKETS = 32
REL_MAX_DIST = 128

GLA_HEADS = 4
GLA_DK = W_C // (2 * GLA_HEADS)
GLA_DV = W_C // GLA_HEADS
GLA_RANK = 16
GLA_TAU = 16.0
GLA_CHUNK = 32

RWKV_HEADS = 4
RWKV_N = W_D // RWKV_HEADS
RWKV_W_LORA = 32
RWKV_A_LORA = 32
RWKV_SHIFT = 3 * W_D + RWKV_W_LORA + RWKV_A_LORA
RWKV_LN_EPS = 64e-5

IN_SIZES = (W_A, SWA_HEADS * SWA_HD, SWA_KV_HEADS * SWA_HD, SWA_KV_HEADS * SWA_HD,
            GLA_HEADS * GLA_DK, GLA_HEADS * GLA_DK, GLA_HEADS * GLA_DV, GLA_RANK,
            RWKV_SHIFT, D_MIX)
N_IN = W_A + W_B + 2 * SWA_KV_HEADS * SWA_HD + 2 * GLA_HEADS * GLA_DK + W_C + GLA_RANK + RWKV_SHIFT + D_MIX

kernel_name = 'hymba_rglru_swa_gla_rwkv7_decoder_step'


def _rms(x, g):
    xf = x.astype(jnp.float32)
    y = xf * lax.rsqrt(jnp.mean(xf * xf, axis=-1, keepdims=True) + RMS_EPS)
    return (y * g.astype(jnp.float32)).astype(x.dtype)


def _split(x, sizes):
    cuts = []
    acc = 0
    for s in sizes[:-1]:
        acc += s
        cuts.append(acc)
    return jnp.split(x, cuts, axis=-1)


def _lin_comb(left, right):
    a_l, b_l = left
    a_r, b_r = right
    return a_l * a_r, a_r * b_l + b_r


def _rglru(xa, conv_buf, h0, lp, reset_first):
    f32 = jnp.float32
    bsz, t_len, _ = xa.shape
    xp = jnp.concatenate([conv_buf.astype(xa.dtype), xa], axis=1)
    xc = lax.conv_general_dilated(xp, lp['lru_conv_w'][:, None, :].astype(xa.dtype), window_strides=(1,),
                                  padding='VALID', dimension_numbers=('NWC', 'WIO', 'NWC'),
                                  feature_group_count=W_A) + lp['lru_conv_b']
    xb = xc.reshape(bsz, t_len, LRU_BLOCKS, LRU_BLOCK)
    gate_r = jnp.einsum('btnc,ncd->btnd', xb, lp['lru_wa']).reshape(bsz, t_len, W_A) + lp['lru_ba']
    gate_i = jnp.einsum('btnc,ncd->btnd', xb, lp['lru_wx']).reshape(bsz, t_len, W_A) + lp['lru_bx']
    log_a = -LRU_C * jax.nn.sigmoid(gate_r.astype(f32)) * jax.nn.softplus(-lp['lru_lambda'].astype(f32))
    mult = jnp.sqrt(-jnp.expm1(2.0 * log_a))
    if reset_first:
        mult = mult.at[:, 0].set(1.0)
    b = mult * jax.nn.sigmoid(gate_i.astype(f32)) * xc.astype(f32)
    a_cum, b_cum = lax.associative_scan(_lin_comb, (jnp.exp(log_a), b), axis=1)
    h = a_cum * h0.astype(f32)[:, None, :] + b_cum
    return h.astype(xa.dtype), xp[:, -(LRU_CONV - 1):], h[:, -1]


def _rel_bucket(dist):
    max_exact = REL_BUCKETS // 2
    n = jnp.maximum(dist, 0)
    log_ratio = jnp.log(jnp.maximum(n, 1).astype(jnp.float32) / max_exact) / math.log(REL_MAX_DIST / max_exact)
    large = jnp.minimum(max_exact + (log_ratio * (REL_BUCKETS - max_exact)).astype(jnp.int32), REL_BUCKETS - 1)
    return jnp.where(n < max_exact, n, large)


def _rel_bias(dist, table):
    tq, tk = dist.shape
    b = table[_rel_bucket(dist)].astype(jnp.float32)
    return jnp.transpose(b, (2, 0, 1)).reshape(SWA_KV_HEADS, SWA_GROUP, tq, tk)


def _attend(q, k, v, bias, sinks):
    lead = q.shape[:-2]
    qg = q.reshape(lead + (SWA_KV_HEADS, SWA_GROUP, SWA_HD))
    s = jnp.einsum('...qhgd,...khd->...hgqk', qg, k, preferred_element_type=jnp.float32) * (SWA_HD ** -0.5) + bias
    sink = sinks.astype(jnp.float32).reshape(SWA_KV_HEADS, SWA_GROUP, 1, 1)
    m = jnp.maximum(jnp.max(s, axis=-1, keepdims=True), sink)
    e = jnp.exp(s - m)
    p = e / (jnp.sum(e, axis=-1, keepdims=True) + jnp.exp(sink - m))
    o = jnp.einsum('...hgqk,...khd->...qhgd', p.astype(v.dtype), v)
    return o.reshape(lead + (SWA_HEADS * SWA_HD,))


def _swa_prompt(q, k, v, table, sinks):
    bsz, t_len = q.shape[:2]
    nb = t_len // WINDOW
    qb = q.reshape(bsz, nb, WINDOW, SWA_HEADS, SWA_HD)
    kb = k.reshape(bsz, nb, WINDOW, SWA_KV_HEADS, SWA_HD)
    vb = v.reshape(bsz, nb, WINDOW, SWA_KV_HEADS, SWA_HD)
    pad = ((0, 0), (1, 0), (0, 0), (0, 0), (0, 0))
    kk = jnp.concatenate([jnp.pad(kb, pad)[:, :-1], kb], axis=2)
    vv = jnp.concatenate([jnp.pad(vb, pad)[:, :-1], vb], axis=2)
    qi = jnp.arange(WINDOW)[:, None]
    kj = jnp.arange(2 * WINDOW)[None, :]
    dist = WINDOW + qi - kj
    band = (dist >= 0) & (dist <= WINDOW)
    blk = jnp.arange(nb)[:, None, None]
    valid = band[None] & ((blk > 0) | (kj >= WINDOW)[None])
    bias = jnp.where(valid[:, None, None], _rel_bias(dist, table)[None], -jnp.inf)
    o = _attend(qb, kk, vv, bias, sinks)
    return o.reshape(bsz, t_len, W_B)


def _swa_sample(q, k, v, kbuf, vbuf, table, sinks):
    wb = kbuf.shape[1]
    t_len = q.shape[1]
    kk = jnp.concatenate([kbuf.astype(k.dtype), k], axis=1)
    vv = jnp.concatenate([vbuf.astype(v.dtype), v], axis=1)
    dist = wb + jnp.arange(t_len)[:, None] - jnp.arange(wb + t_len)[None, :]
    valid = (dist >= 0) & (dist <= WINDOW)
    bias = jnp.where(valid, _rel_bias(dist, table), -jnp.inf)
    return _attend(q, kk, vv, bias, sinks)


def _gla(q, k, v, log_alpha, s0):
    f32 = jnp.float32
    bsz, t_len = q.shape[:2]
    c = min(GLA_CHUNK, t_len)
    t_pad = -(-t_len // c) * c
    q, k, v, log_alpha = (t.astype(f32) for t in (q, k, v, log_alpha))
    if t_pad > t_len:
        padw = ((0, 0), (0, t_pad - t_len), (0, 0), (0, 0))
        q, k, v, log_alpha = (jnp.pad(t, padw) for t in (q, k, v, log_alpha))
    nc = t_pad // c
    def rs(t):
        return t.reshape(bsz, nc, c, GLA_HEADS, t.shape[-1])
    qf = rs(q) * (GLA_DK ** -0.5)
    kf, vf, la = rs(k), rs(v), rs(log_alpha)
    b = jnp.cumsum(la, axis=2)
    b_end = b[:, :, -1:]
    q_dec = qf * jnp.exp(b)
    k_inv = kf * jnp.exp(-b)
    k_end = kf * jnp.exp(b_end - b)
    causal = jnp.tril(jnp.ones((c, c), dtype=bool))
    att = jnp.where(causal, jnp.einsum('bnqhd,bnkhd->bnhqk', q_dec, k_inv), 0.0)
    o_intra = jnp.einsum('bnhqk,bnkhv->bnqhv', att, vf)
    kv = jnp.einsum('bnkhd,bnkhv->bnhdv', k_end, vf)
    dec = jnp.exp(b_end[:, :, 0])
    def step(s, inp):
        d, u = inp
        return d[..., None] * s + u, s
    s_fin, s_starts = lax.scan(step, s0.astype(f32), (jnp.moveaxis(dec, 1, 0), jnp.moveaxis(kv, 1, 0)))
    o_inter = jnp.einsum('bnqhd,nbhdv->bnqhv', q_dec, s_starts)
    o = (o_intra + o_inter).reshape(bsz, t_pad, GLA_HEADS, GLA_DV)[:, :t_len]
    return o, s_fin


def _rwkv(u, shift_prev, s0, lp):
    f32 = jnp.float32
    bsz, t_len, _ = u.shape
    u_prev = jnp.concatenate([shift_prev[:, None].astype(u.dtype), u[:, :-1]], axis=1)
    us = (u + (u_prev - u) * lp['rwkv_mu']).astype(f32)
    r, k, v, wd, ad = _split(us, (W_D, W_D, W_D, RWKV_W_LORA, RWKV_A_LORA))
    w = -jax.nn.softplus(-(lp['rwkv_w0'] + jnp.tanh(wd) @ lp['rwkv_w2'])) - 0.5
    decay = jnp.exp(-jnp.exp(w))
    a = jax.nn.sigmoid(lp['rwkv_a0'] + ad @ lp['rwkv_a2'])
    def hs(t):
        return t.reshape(bsz, t_len, RWKV_HEADS, RWKV_N)
    kk = hs(k * lp['rwkv_k_k'])
    kk = kk / jnp.maximum(jnp.sqrt(jnp.sum(kk * kk, axis=-1, keepdims=True)), 1e-12)
    k = k * (1.0 + (a - 1.0) * lp['rwkv_k_a'])
    r, decay, k, v, a = hs(r), hs(decay), hs(k), hs(v), hs(a)
    aa = -kk
    bb = kk * a
    def step(s, inp):
        r_t, w_t, k_t, v_t, a_t, b_t = inp
        sa = jnp.einsum('bhij,bhj->bhi', s, a_t)
        s = s * w_t[:, :, None, :] + sa[..., None] * b_t[:, :, None, :] + v_t[..., None] * k_t[:, :, None, :]
        return s, jnp.einsum('bhij,bhj->bhi', s, r_t)
    xs = tuple(jnp.moveaxis(t, 1, 0) for t in (r, decay, k, v, aa, bb))
    s_fin, y = lax.scan(step, s0.astype(f32), xs)
    y = jnp.moveaxis(y, 0, 1)
    mean = jnp.mean(y, axis=-1, keepdims=True)
    var = jnp.mean(jnp.square(y - mean), axis=-1, keepdims=True)
    y = ((y - mean) * lax.rsqrt(var + RWKV_LN_EPS)).reshape(bsz, t_len, W_D) * lp['rwkv_ln_w'] + lp['rwkv_ln_b']
    y = y + (jnp.sum(r * k * lp['rwkv_r_k'], axis=-1, keepdims=True) * v).reshape(bsz, t_len, W_D)
    return y.astype(u.dtype), s_fin, u[:, -1]


def _layer(x, pe, lp, table, state):
    prompt = state is None
    f32 = jnp.float32
    bsz, t_len, _ = x.shape
    h = _rms(x, lp['norm_pre'])
    z = h @ lp['w_in']
    xa, bq, bk, bv, cq, ck, cv, ca, ud, gate = _split(z, IN_SIZES)
    if prompt:
        h0 = jnp.zeros((bsz, W_A), f32)
        conv0 = jnp.zeros((bsz, LRU_CONV - 1, W_A), x.dtype)
        sg0 = jnp.zeros((bsz, GLA_HEADS, GLA_DK, GLA_DV), f32)
        sr0 = jnp.zeros((bsz, RWKV_HEADS, RWKV_N, RWKV_N), f32)
        sh0 = jnp.zeros((bsz, RWKV_SHIFT), x.dtype)
    else:
        h0, conv0, kbuf, vbuf, sg0, sr0, sh0 = state
    y_a, conv_new, h_new = _rglru(xa, conv0, h0, lp, prompt)
    y_a = _rms(y_a, lp['lru_norm'])
    q = bq.reshape(bsz, t_len, SWA_HEADS, SWA_HD)
    k = bk.reshape(bsz, t_len, SWA_KV_HEADS, SWA_HD)
    v = bv.reshape(bsz, t_len, SWA_KV_HEADS, SWA_HD)
    if prompt:
        y_b = _swa_prompt(q, k, v, table, lp['swa_sinks'])
        k_new, v_new = k[:, -WINDOW:], v[:, -WINDOW:]
    else:
        y_b = _swa_sample(q, k, v, kbuf, vbuf, table, lp['swa_sinks'])
        k_new, v_new = k, v
    y_b = _rms(y_b, lp['swa_norm'])
    log_alpha = jax.nn.log_sigmoid((ca @ lp['gla_w_up'] + lp['gla_b']).astype(f32)) / GLA_TAU
    o_c, sg_new = _gla(cq.reshape(bsz, t_len, GLA_HEADS, GLA_DK), ck.reshape(bsz, t_len, GLA_HEADS, GLA_DK),
                       cv.reshape(bsz, t_len, GLA_HEADS, GLA_DV),
                       log_alpha.reshape(bsz, t_len, GLA_HEADS, GLA_DK), sg0)
    y_c = _rms(o_c, lp['gla_norm']).reshape(bsz, t_len, W_C).astype(x.dtype)
    y_d, sr_new, sh_new = _rwkv(ud, sh0, sr0, lp)
    y = jnp.concatenate([y_a, y_b, y_c, y_d], axis=-1) * jax.nn.silu(gate)
    x = x + _rms(y @ lp['w_out'], lp['norm_post'])
    x = x + (pe @ lp['ple_proj']) * jax.nn.sigmoid(x @ lp['ple_gate'])
    return x, (h_new, conv_new, k_new, v_new, sg_new, sr_new, sh_new)


def setup_inputs(seed: int = 0) -> dict:
    key = jax.random.key(seed)
    ks = iter(jax.random.split(key, 64))
    f32 = jnp.float32
    def nrm(shape, scale=1.0):
        return jax.random.normal(next(ks), shape, f32) * scale
    def gain(shape):
        return 1.0 + 0.05 * jax.random.normal(next(ks), shape, f32)
    wb = min(WINDOW, PAST_LEN)
    inp = {}
    inp['x_prompt'] = nrm((BATCH, SEQ, D_MODEL))
    inp['x_sample'] = nrm((DEC_BATCH, DEC_SEQ, D_MODEL))
    inp['state_lru_h'] = nrm((DEPTH, DEC_BATCH, W_A), 0.5)
    inp['state_lru_conv'] = nrm((DEPTH, DEC_BATCH, LRU_CONV - 1, W_A))
    inp['cache_swa_k'] = nrm((DEPTH, DEC_BATCH, wb, SWA_KV_HEADS, SWA_HD))
    inp['cache_swa_v'] = nrm((DEPTH, DEC_BATCH, wb, SWA_KV_HEADS, SWA_HD))
    inp['state_gla'] = nrm((DEPTH, DEC_BATCH, GLA_HEADS, GLA_DK, GLA_DV), 0.5)
    inp['state_rwkv'] = nrm((DEPTH, DEC_BATCH, RWKV_HEADS, RWKV_N, RWKV_N), 0.3)
    inp['state_rwkv_shift'] = nrm((DEPTH, DEC_BATCH, RWKV_SHIFT))
    inp['p_prompt'] = nrm((DEPTH, BATCH, SEQ, D_PLE))
    inp['p_sample'] = nrm((DEPTH, DEC_BATCH, DEC_SEQ, D_PLE))
    inp['rel_bias'] = nrm((REL_BUCKETS, SWA_HEADS), 0.5)
    inp['norm_pre'] = gain((DEPTH, D_MODEL))
    inp['norm_post'] = gain((DEPTH, D_MODEL))
    inp['w_in'] = nrm((DEPTH, D_MODEL, N_IN), D_MODEL ** -0.5)
    inp['lru_conv_w'] = nrm((DEPTH, LRU_CONV, W_A), LRU_CONV ** -0.5)
    inp['lru_conv_b'] = nrm((DEPTH, W_A), 0.02)
    inp['lru_wa'] = nrm((DEPTH, LRU_BLOCKS, LRU_BLOCK, LRU_BLOCK), LRU_BLOCK ** -0.5)
    inp['lru_ba'] = nrm((DEPTH, W_A), 0.02)
    inp['lru_wx'] = nrm((DEPTH, LRU_BLOCKS, LRU_BLOCK, LRU_BLOCK), LRU_BLOCK ** -0.5)
    inp['lru_bx'] = nrm((DEPTH, W_A), 0.02)
    a8 = jax.random.uniform(next(ks), (DEPTH, W_A), f32, minval=0.9, maxval=0.999)
    a = a8 ** (1.0 / LRU_C)
    inp['lru_lambda'] = jnp.log(a) - jnp.log1p(-a)
    inp['lru_norm'] = gain((DEPTH, W_A))
    inp['swa_sinks'] = nrm((DEPTH, SWA_HEADS), 0.5)
    inp['swa_norm'] = gain((DEPTH, W_B))
    inp['gla_w_up'] = nrm((DEPTH, GLA_RANK, GLA_HEADS * GLA_DK), GLA_RANK ** -0.5)
    inp['gla_b'] = nrm((DEPTH, GLA_HEADS * GLA_DK), 0.1)
    inp['gla_norm'] = gain((DEPTH, GLA_DV))
    inp['rwkv_mu'] = jax.random.uniform(next(ks), (DEPTH, RWKV_SHIFT), f32)
    inp['rwkv_w0'] = jax.random.uniform(next(ks), (DEPTH, W_D), f32, minval=-6.0, maxval=-1.0)
    inp['rwkv_w2'] = nrm((DEPTH, RWKV_W_LORA, W_D), 0.1)
    inp['rwkv_a0'] = nrm((DEPTH, W_D), 0.1)
    inp['rwkv_a2'] = nrm((DEPTH, RWKV_A_LORA, W_D), 0.5 * RWKV_A_LORA ** -0.5)
    inp['rwkv_k_k'] = 0.85 + 0.05 * jax.random.normal(next(ks), (DEPTH, W_D), f32)
    inp['rwkv_k_a'] = gain((DEPTH, W_D))
    inp['rwkv_r_k'] = nrm((DEPTH, RWKV_HEADS, RWKV_N), 0.1)
    inp['rwkv_ln_w'] = gain((DEPTH, W_D))
    inp['rwkv_ln_b'] = nrm((DEPTH, W_D), 0.02)
    inp['w_out'] = nrm((DEPTH, D_MIX, D_MODEL), D_MIX ** -0.5)
    inp['ple_proj'] = nrm((DEPTH, D_PLE, D_MODEL), 0.5 * D_PLE ** -0.5)
    inp['ple_gate'] = nrm((DEPTH, D_MODEL, D_MODEL), D_MODEL ** -0.5)
    return inp


def reference(x_prompt, x_sample, state_lru_h, state_lru_conv, cache_swa_k, cache_swa_v, state_gla,
              state_rwkv, state_rwkv_shift, p_prompt, p_sample, rel_bias, norm_pre, norm_post, w_in,
              lru_conv_w, lru_conv_b, lru_wa, lru_ba, lru_wx, lru_bx, lru_lambda, lru_norm, swa_sinks,
              swa_norm, gla_w_up, gla_b, gla_norm, rwkv_mu, rwkv_w0, rwkv_w2, rwkv_a0, rwkv_a2, rwkv_k_k,
              rwkv_k_a, rwkv_r_k, rwkv_ln_w, rwkv_ln_b, w_out, ple_proj, ple_gate):
    xp = x_prompt
    xs = x_sample
    sts_p = []
    sts_s = []
    for i in range(DEPTH):
        lp = dict(norm_pre=norm_pre[i], norm_post=norm_post[i], w_in=w_in[i],
                  lru_conv_w=lru_conv_w[i], lru_conv_b=lru_conv_b[i], lru_wa=lru_wa[i], lru_ba=lru_ba[i],
                  lru_wx=lru_wx[i], lru_bx=lru_bx[i], lru_lambda=lru_lambda[i], lru_norm=lru_norm[i],
                  swa_sinks=swa_sinks[i], swa_norm=swa_norm[i],
                  gla_w_up=gla_w_up[i], gla_b=gla_b[i], gla_norm=gla_norm[i],
                  rwkv_mu=rwkv_mu[i], rwkv_w0=rwkv_w0[i], rwkv_w2=rwkv_w2[i], rwkv_a0=rwkv_a0[i],
                  rwkv_a2=rwkv_a2[i], rwkv_k_k=rwkv_k_k[i], rwkv_k_a=rwkv_k_a[i], rwkv_r_k=rwkv_r_k[i],
                  rwkv_ln_w=rwkv_ln_w[i], rwkv_ln_b=rwkv_ln_b[i],
                  w_out=w_out[i], ple_proj=ple_proj[i], ple_gate=ple_gate[i])
        xp, st_p = _layer(xp, p_prompt[i], lp, rel_bias, None)
        xs, st_s = _layer(xs, p_sample[i], lp, rel_bias,
                          (state_lru_h[i], state_lru_conv[i], cache_swa_k[i], cache_swa_v[i],
                           state_gla[i], state_rwkv[i], state_rwkv_shift[i]))
        sts_p.append(st_p)
        sts_s.append(st_s)
    hp, cp, kp, vp, gp, rp, shp = [jnp.stack([s[j] for s in sts_p]) for j in range(7)]
    hs, cs, ks, vs, gs, rs, shs = [jnp.stack([s[j] for s in sts_s]) for j in range(7)]
    return (xp, xs, hp, hs, cp, cs, kp, ks, vp, vs, gp, gs, rp, rs, shp, shs)
```

```python
import functools
import math

import jax
import jax.numpy as jnp
from jax import lax
from jax.experimental import pallas as pl
from jax.experimental.pallas import tpu as pltpu

f32 = jnp.float32
bf16 = jnp.bfloat16

D_MODEL = 1024
D_PLE = 256
RMS_EPS = 1e-6
W_MIX = 256
LRU_BLOCKS = 4
LRU_CONV = 4
LRU_C = 8.0
SWA_HEADS = 4
SWA_KV_HEADS = 2
SWA_HD = 64
WINDOW = 128
REL_BUCKETS = 32
REL_MAX_DIST = 128
GLA_HEADS = 4
GLA_DK = 32
GLA_DV = 64
GLA_RANK = 16
GLA_TAU = 16.0
RWKV_HEADS = 4
RWKV_N = 64
RWKV_LORA = 32
RWKV_SHIFT = 3 * W_MIX + 2 * RWKV_LORA
RWKV_LN_EPS = 64e-5

ZA_W = 256
ZB_W = 512
ZC_W = 512
ZD_W = 896
ZG_W = 1024
ZD_TAIL = 768
N_IN_PACKED = ZA_W + ZB_W + ZC_W + ZD_W + ZG_W

LANES = 128
SUBLANES = 8
VMEM_LIMIT = 48 * 1024 * 1024

ROW_TILE = 512
SEQ_TILE = 256
CHUNK = 64

NN = (((1,), (0,)), ((), ()))
NT = (((1,), (1,)), ((), ()))
TN = (((0,), (0,)), ((), ()))


def _dot(a, b, dn):
    return lax.dot_general(a, b, dn, preferred_element_type=f32)


def _split2(x):
    hi = x.astype(bf16)
    lo = (x - hi.astype(f32)).astype(bf16)
    return hi, lo


def _mm(a, b, dn=NN, passes=1):
    if passes == 1:
        return _dot(a.astype(bf16), b.astype(bf16), dn)
    ah, al = _split2(a)
    bh, bl = _split2(b)
    return _dot(ah, bh, dn) + (_dot(ah, bl, dn) + _dot(al, bh, dn))


def _mm_exact_rhs(a, b_bf16):
    a1 = a.astype(bf16)
    r1 = a - a1.astype(f32)
    a2 = r1.astype(bf16)
    a3 = (r1 - a2.astype(f32)).astype(bf16)
    return _dot(a1, b_bf16, NN) + (_dot(a2, b_bf16, NN) + _dot(a3, b_bf16, NN))


def _head_ones(n, width):
    r = lax.broadcasted_iota(jnp.int32, (n, n), 0) // width
    c = lax.broadcasted_iota(jnp.int32, (n, n), 1) // width
    return jnp.where(r == c, 1.0, 0.0).astype(bf16)


def _rms(x, g):
    return x * lax.rsqrt(jnp.mean(x * x, axis=-1, keepdims=True) + RMS_EPS) * g


def _seg_cumsum(x, seg):
    rows = lax.broadcasted_iota(jnp.int32, x.shape, 0) % seg
    d = 1
    while d < seg:
        x = x + jnp.where(rows >= d, pltpu.roll(x, d, 0), 0.0)
        d *= 2
    return x


def _params(*sem):
    return pltpu.CompilerParams(dimension_semantics=sem, vmem_limit_bytes=VMEM_LIMIT)


def _inproj_body(x_ref, g_ref, w_ref, za_ref, zb_ref, zc_ref, zd_ref, zg_ref):
    h = _rms(x_ref[...], g_ref[...]).astype(bf16)
    off = 0
    for o_ref in (za_ref, zb_ref, zc_ref, zd_ref, zg_ref):
        n = o_ref.shape[-1]
        o_ref[...] = _dot(h, w_ref[:, off:off + n], NN)
        off += n


def _inproj(x, g, w):
    rows = x.shape[0]
    tm = min(ROW_TILE, rows)
    widths = (ZA_W, ZB_W, ZC_W, ZD_W, ZG_W)
    return pl.pallas_call(
        _inproj_body,
        grid=(rows // tm,),
        in_specs=[
            pl.BlockSpec((tm, D_MODEL), lambda i: (i, 0)),
            pl.BlockSpec((1, D_MODEL), lambda i: (0, 0)),
            pl.BlockSpec((D_MODEL, N_IN_PACKED), lambda i: (0, 0)),
        ],
        out_specs=[pl.BlockSpec((tm, n), lambda i: (i, 0)) for n in widths],
        out_shape=[jax.ShapeDtypeStruct((rows, n), f32) for n in widths],
        compiler_params=_params("parallel"),
        name="in_proj",
    )(x, g, w)


def _outproj_body(ya_ref, yb_ref, yc_ref, yd_ref, zg_ref, x_ref, pe_ref, wo_ref, gp_ref, wp_ref, wg_ref, o_ref):
    g = zg_ref[...]
    sg = g * jax.nn.sigmoid(g)
    acc = None
    for i, y_ref in enumerate((ya_ref, yb_ref, yc_ref, yd_ref)):
        lo = i * W_MIX
        y = (y_ref[...] * sg[:, lo:lo + W_MIX]).astype(bf16)
        part = _dot(y, wo_ref[lo:lo + W_MIX, :], NN)
        acc = part if acc is None else acc + part
    x1 = x_ref[...] + _rms(acc, gp_ref[...])
    emb = _dot(pe_ref[...].astype(bf16), wp_ref[...], NN)
    gate = jax.nn.sigmoid(_dot(x1.astype(bf16), wg_ref[...], NN))
    o_ref[...] = x1 + emb * gate


def _outproj(ya, yb, yc, yd, zg, x, pe, wo, gp, wp, wg):
    rows = x.shape[0]
    tm = min(ROW_TILE, rows)
    row = lambda n: pl.BlockSpec((tm, n), lambda i: (i, 0))
    full = lambda a: pl.BlockSpec(a.shape, lambda i: (0, 0))
    return pl.pallas_call(
        _outproj_body,
        grid=(rows // tm,),
        in_specs=[row(W_MIX), row(W_MIX), row(W_MIX), row(W_MIX), row(ZG_W), row(D_MODEL), row(D_PLE),
                  full(wo), full(gp), full(wp), full(wg)],
        out_specs=row(D_MODEL),
        out_shape=jax.ShapeDtypeStruct((rows, D_MODEL), f32),
        compiler_params=_params("parallel"),
        name="out_proj",
    )(ya, yb, yc, yd, zg, x, pe, wo, gp, wp, wg)


def _lru_body(x_ref, c0_ref, h0_ref, cw_ref, cb_ref, wa_ref, ba_ref, wx_ref, bx_ref, lam_ref, gn_ref,
              y_ref, hl_ref, ext_ref, hc_ref, *, tc, t_valid, reset_first):
    j = pl.program_id(1)

    @pl.when(j == 0)
    def _():
        ext_ref[0:SUBLANES, :] = c0_ref[0]
        hc_ref[...] = h0_ref[0]

    @pl.when(j > 0)
    def _():
        ext_ref[0:SUBLANES, :] = ext_ref[tc:tc + SUBLANES, :]

    x = x_ref[0]
    ext_ref[SUBLANES:SUBLANES + tc, :] = x
    xc = cw_ref[3:4, :] * x + cb_ref[...]
    for k in range(LRU_CONV - 1):
        s = SUBLANES - (LRU_CONV - 1) + k
        xc = xc + cw_ref[k:k + 1, :] * ext_ref[s:s + tc, :]
    gate_r = _mm(xc, wa_ref[...]) + ba_ref[...]
    gate_i = _mm(xc, wx_ref[...]) + bx_ref[...]
    log_a = -LRU_C * jax.nn.sigmoid(gate_r) * jax.nn.softplus(-lam_ref[...])
    a = jnp.exp(log_a)
    mult = jnp.sqrt(1.0 - a * a)
    rows = lax.broadcasted_iota(jnp.int32, (tc, W_MIX), 0)
    if reset_first:
        mult = jnp.where(rows + j * tc == 0, 1.0, mult)
    b = mult * jax.nn.sigmoid(gate_i) * xc
    d = 1
    while d < tc:
        m = rows >= d
        a_s = pltpu.roll(a, d, 0)
        b_s = pltpu.roll(b, d, 0)
        b = jnp.where(m, a * b_s + b, b)
        a = jnp.where(m, a * a_s, a)
        d *= 2
    h = a * hc_ref[...] + b
    hc_ref[...] = h[tc - 1:tc, :]
    y_ref[0] = _rms(h, gn_ref[...])
    r_last = (t_valid - 1) % tc

    @pl.when(j == (t_valid - 1) // tc)
    def _():
        hl_ref[0] = h[r_last:r_last + 1, :]


def _lru(za, conv0, h0, p, *, tc, t_valid, reset_first):
    b, t, _ = za.shape
    vec = pl.BlockSpec((1, W_MIX), lambda i, j: (0, 0))
    mat = pl.BlockSpec((W_MIX, W_MIX), lambda i, j: (0, 0))
    return pl.pallas_call(
        functools.partial(_lru_body, tc=tc, t_valid=t_valid, reset_first=reset_first),
        grid=(b, t // tc),
        in_specs=[
            pl.BlockSpec((1, tc, W_MIX), lambda i, j: (i, j, 0)),
            pl.BlockSpec((1, SUBLANES, W_MIX), lambda i, j: (i, 0, 0)),
            pl.BlockSpec((1, 1, W_MIX), lambda i, j: (i, 0, 0)),
            pl.BlockSpec((LRU_CONV, W_MIX), lambda i, j: (0, 0)),
            vec, mat, vec, mat, vec, vec, vec,
        ],
        out_specs=[
            pl.BlockSpec((1, tc, W_MIX), lambda i, j: (i, j, 0)),
            pl.BlockSpec((1, 1, W_MIX), lambda i, j: (i, 0, 0)),
        ],
        out_shape=[jax.ShapeDtypeStruct((b, t, W_MIX), f32), jax.ShapeDtypeStruct((b, 1, W_MIX), f32)],
        scratch_shapes=[pltpu.VMEM((tc + SUBLANES, W_MIX), f32), pltpu.VMEM((1, W_MIX), f32)],
        compiler_params=_params("parallel", "arbitrary"),
        name="rglru",
    )(za, conv0, h0, p["conv_w"], p["conv_b"], p["wa"], p["ba"], p["wx"], p["bx"], p["lam"], p["norm"])


def _swa_body(q_ref, ko_ref, vo_ref, kp_ref, vp_ref, bias_ref, sink_ref, gn_ref, y_ref, *, tq, mask_first):
    n = pl.program_id(1)
    q = q_ref[0] * (SWA_HD ** -0.5)
    outs = []
    ssq = None
    for h in range(SWA_HEADS):
        lo = (h // (SWA_HEADS // SWA_KV_HEADS)) * SWA_HD
        qh = q[:, h * SWA_HD:(h + 1) * SWA_HD]
        s_prev = _mm(qh, kp_ref[0, :, lo:lo + SWA_HD], NT) + bias_ref[h, :, 0:WINDOW]
        s_own = _mm(qh, ko_ref[0, :, lo:lo + SWA_HD], NT) + bias_ref[h, :, WINDOW:WINDOW + tq]
        if mask_first:
            s_prev = jnp.where(n > 0, s_prev, -jnp.inf)
        sink = sink_ref[:, h:h + 1]
        m = jnp.maximum(jnp.maximum(jnp.max(s_prev, axis=-1, keepdims=True),
                                    jnp.max(s_own, axis=-1, keepdims=True)), sink)
        e_prev = jnp.exp(s_prev - m)
        e_own = jnp.exp(s_own - m)
        den = (jnp.sum(e_prev, axis=-1, keepdims=True) + jnp.sum(e_own, axis=-1, keepdims=True)
               + jnp.exp(sink - m))
        o = (_mm(e_prev, vp_ref[0, :, lo:lo + SWA_HD]) + _mm(e_own, vo_ref[0, :, lo:lo + SWA_HD])) / den
        outs.append(o)
        sq = jnp.sum(o * o, axis=-1, keepdims=True)
        ssq = sq if ssq is None else ssq + sq
    inv = lax.rsqrt(ssq / W_MIX + RMS_EPS)
    for h, o in enumerate(outs):
        y_ref[0, :, h * SWA_HD:(h + 1) * SWA_HD] = o * inv * gn_ref[:, h * SWA_HD:(h + 1) * SWA_HD]


def _swa(zb, k_prev, v_prev, bias, sinks, gn, *, tq, own_prev):
    b, t, _ = zb.shape
    kw = SWA_KV_HEADS * SWA_HD
    if own_prev:
        kp_spec = pl.BlockSpec((1, WINDOW, kw), lambda i, n: (i, jnp.maximum(n - 1, 0), 2))
        vp_spec = pl.BlockSpec((1, WINDOW, kw), lambda i, n: (i, jnp.maximum(n - 1, 0), 3))
    else:
        kp_spec = pl.BlockSpec((1, WINDOW, kw), lambda i, n: (i, 0, 0))
        vp_spec = pl.BlockSpec((1, WINDOW, kw), lambda i, n: (i, 0, 0))
    return pl.pallas_call(
        functools.partial(_swa_body, tq=tq, mask_first=own_prev),
        grid=(b, t // tq),
        in_specs=[
            pl.BlockSpec((1, tq, W_MIX), lambda i, n: (i, n, 0)),
            pl.BlockSpec((1, tq, kw), lambda i, n: (i, n, 2)),
            pl.BlockSpec((1, tq, kw), lambda i, n: (i, n, 3)),
            kp_spec, vp_spec,
            pl.BlockSpec(bias.shape, lambda i, n: (0, 0, 0)),
            pl.BlockSpec((1, SWA_HEADS), lambda i, n: (0, 0)),
            pl.BlockSpec((1, W_MIX), lambda i, n: (0, 0)),
        ],
        out_specs=pl.BlockSpec((1, tq, W_MIX), lambda i, n: (i, n, 0)),
        out_shape=jax.ShapeDtypeStruct((b, t, W_MIX), f32),
        compiler_params=_params("parallel", "arbitrary"),
        name="swa",
    )(zb, zb, zb, k_prev, v_prev, bias, sinks, gn)


def _gla_body(q_ref, k_ref, v_ref, t_ref, s0_ref, wup_ref, gb_ref, gn_ref, y_ref, sn_ref, st_ref,
              *, tc, chunk, t_valid):
    j = pl.program_id(1)

    @pl.when(j == 0)
    def _():
        st_ref[...] = s0_ref[0]

    rows = lax.broadcasted_iota(jnp.int32, (tc, 1), 0)
    live = rows + j * tc < t_valid
    la = jax.nn.log_sigmoid(_mm(t_ref[0], wup_ref[...], passes=3) + gb_ref[...]) / GLA_TAU
    la = jnp.where(live, la, 0.0)
    k = jnp.where(live, k_ref[0], 0.0)
    v = jnp.where(live, v_ref[0], 0.0)
    bc = _seg_cumsum(la, chunk)
    q_dec = q_ref[0] * (GLA_DK ** -0.5) * jnp.exp(bc)
    k_inv = k * jnp.exp(-bc)
    r2 = lax.broadcasted_iota(jnp.int32, (tc, tc), 0)
    c2 = lax.broadcasted_iota(jnp.int32, (tc, tc), 1)
    causal = (r2 // chunk == c2 // chunk) & (r2 >= c2)
    nch = tc // chunk
    b_end = [bc[(c + 1) * chunk - 1:(c + 1) * chunk, :] for c in range(nch)]
    outs = []
    for h in range(GLA_HEADS):
        ks = slice(h * GLA_DK, (h + 1) * GLA_DK)
        vh = v[:, h * GLA_DV:(h + 1) * GLA_DV]
        att = jnp.where(causal, _mm(q_dec[:, ks], k_inv[:, ks], NT), 0.0)
        o_intra = _mm(att, vh)
        o_inter = []
        st = st_ref[h]
        for c in range(nch):
            rs = slice(c * chunk, (c + 1) * chunk)
            o_inter.append(_mm(q_dec[rs, ks], st, NT))
            k_end = k[rs, ks] * jnp.exp(b_end[c][:, ks] - bc[rs, ks])
            st = st * jnp.exp(b_end[c][:, ks]) + _mm(vh[rs], k_end, TN)
        st_ref[h] = st
        outs.append(o_intra + (o_inter[0] if nch == 1 else jnp.concatenate(o_inter, axis=0)))
    o = jnp.concatenate(outs, axis=-1)
    ms = _mm_exact_rhs(o * o, _head_ones(W_MIX, GLA_DV)) / GLA_DV
    y_ref[0] = o * lax.rsqrt(ms + RMS_EPS) * gn_ref[...]

    @pl.when(j == pl.num_programs(1) - 1)
    def _():
        sn_ref[0] = st_ref[...]


def _gla(zc, zd, s0t, p, *, tc, chunk, t_valid):
    b, t, _ = zc.shape
    qk = GLA_HEADS * GLA_DK
    return pl.pallas_call(
        functools.partial(_gla_body, tc=tc, chunk=chunk, t_valid=t_valid),
        grid=(b, t // tc),
        in_specs=[
            pl.BlockSpec((1, tc, qk), lambda i, j: (i, j, 0)),
            pl.BlockSpec((1, tc, qk), lambda i, j: (i, j, 1)),
            pl.BlockSpec((1, tc, W_MIX), lambda i, j: (i, j, 1)),
            pl.BlockSpec((1, tc, LANES), lambda i, j: (i, j, ZD_TAIL // LANES)),
            pl.BlockSpec((1, GLA_HEADS, GLA_DV, GLA_DK), lambda i, j: (i, 0, 0, 0)),
            pl.BlockSpec((LANES, qk), lambda i, j: (0, 0)),
            pl.BlockSpec((1, qk), lambda i, j: (0, 0)),
            pl.BlockSpec((1, W_MIX), lambda i, j: (0, 0)),
        ],
        out_specs=[
            pl.BlockSpec((1, tc, W_MIX), lambda i, j: (i, j, 0)),
            pl.BlockSpec((1, GLA_HEADS, GLA_DV, GLA_DK), lambda i, j: (i, 0, 0, 0)),
        ],
        out_shape=[jax.ShapeDtypeStruct((b, t, W_MIX), f32),
                   jax.ShapeDtypeStruct((b, GLA_HEADS, GLA_DV, GLA_DK), f32)],
        scratch_shapes=[pltpu.VMEM((GLA_HEADS, GLA_DV, GLA_DK), f32)],
        compiler_params=_params("parallel", "arbitrary"),
        name="gla",
    )(zc, zc, zc, zd, s0t, p["w_up"], p["b"], p["norm"])


def _rwkv_body(u_ref, sh0_ref, s0_ref, mu_ref, w0_ref, w2_ref, a0_ref, a2_ref, kk_ref, ka_ref, rk_ref,
               lw_ref, lb_ref, y_ref, sn_ref, st_ref, up_ref, *, tc, chunk, t_valid, passes):
    j = pl.program_id(1)

    @pl.when(j == 0)
    def _():
        st_ref[...] = s0_ref[0]
        up_ref[...] = sh0_ref[0]

    u = u_ref[0]
    rows1 = lax.broadcasted_iota(jnp.int32, (tc, 1), 0)
    u_prev = jnp.where(rows1 == 0, up_ref[...], pltpu.roll(u, 1, 0))
    up_ref[...] = u[tc - 1:tc, :]
    us = u + (u_prev - u) * mu_ref[...]
    r = us[:, 0:W_MIX]
    k = us[:, W_MIX:2 * W_MIX]
    v = us[:, 2 * W_MIX:3 * W_MIX]
    tail = us[:, ZD_TAIL:ZD_TAIL + LANES]
    w = -jax.nn.softplus(-(w0_ref[...] + _mm(jnp.tanh(tail), w2_ref[...], passes=3))) - 0.5
    lw = -jnp.exp(w)
    a = jax.nn.sigmoid(a0_ref[...] + _mm(tail, a2_ref[...], passes=3))
    ones = _head_ones(W_MIX, RWKV_N)
    kk = k * kk_ref[...]
    kk = kk / jnp.maximum(jnp.sqrt(_mm_exact_rhs(kk * kk, ones)), 1e-12)
    k = k * (1.0 + (a - 1.0) * ka_ref[...])
    live = rows1 + j * tc < t_valid
    lw = jnp.where(live, lw, 0.0)
    aa = jnp.where(live, -kk, 0.0)
    bb = jnp.where(live, kk * a, 0.0)
    kl = jnp.where(live, k, 0.0)
    vl = jnp.where(live, v, 0.0)

    cum = _seg_cumsum(lw, chunk)
    g = jnp.exp(cum)
    gi = jnp.exp(-cum)
    a_t = aa * jnp.exp(cum - lw)
    b_t = bb * gi
    k_t = kl * gi
    r_t = r * g
    nch = tc // chunk
    cum_end = [cum[(c + 1) * chunk - 1:(c + 1) * chunk, :] for c in range(nch)]
    r2 = lax.broadcasted_iota(jnp.int32, (tc, tc), 0)
    c2 = lax.broadcasted_iota(jnp.int32, (tc, tc), 1)
    same = r2 // chunk == c2 // chunk
    strict = same & (r2 > c2)
    incl = same & (r2 >= c2)
    eye = jnp.where(r2 == c2, 1.0, 0.0)

    p_m, p_i, p_s = passes
    ys = []
    for h in range(RWKV_HEADS):
        hs = slice(h * RWKV_N, (h + 1) * RWKV_N)
        ah, bh, kh, rh, vh = a_t[:, hs], b_t[:, hs], k_t[:, hs], r_t[:, hs], vl[:, hs]
        l_ab = jnp.where(strict, _mm(ah, bh, NT, p_m), 0.0)
        l_ak = jnp.where(strict, _mm(ah, kh, NT, p_m), 0.0)
        m_rb = jnp.where(incl, _mm(rh, bh, NT, p_m), 0.0)
        m_rk = jnp.where(incl, _mm(rh, kh, NT, p_m), 0.0)
        inv = eye + l_ab
        pw = l_ab
        d = 2
        while d < chunk:
            pw = _mm(pw, pw, NN, p_i)
            inv = inv + _mm(inv, pw, NN, p_i)
            d *= 2
        p1 = _mm(inv, ah, NN, p_i)
        p2 = _mm(inv, _mm(l_ak, vh, NN, p_i), NN, p_i)
        y_v = _mm(m_rk, vh, NN, p_m)
        st = st_ref[h]
        y_parts = []
        for c in range(nch):
            rs = slice(c * chunk, (c + 1) * chunk)
            both = _mm(jnp.concatenate([p1[rs], rh[rs]], axis=0), st, NT, p_s)
            u_c = both[0:chunk] + p2[rs]
            y_parts.append(both[chunk:2 * chunk] + _mm(m_rb[rs, rs], u_c, NN, p_s) + y_v[rs])
            dec = jnp.exp(cum_end[c][:, hs] - cum[rs, hs])
            lhs = jnp.concatenate([u_c, vh[rs]], axis=0)
            rhs = jnp.concatenate([bb[rs, hs] * dec, kl[rs, hs] * dec], axis=0)
            st = st * jnp.exp(cum_end[c][:, hs]) + _mm(lhs, rhs, TN, p_s)
        st_ref[h] = st
        ys.append(y_parts[0] if nch == 1 else jnp.concatenate(y_parts, axis=0))
    y = jnp.concatenate(ys, axis=-1)
    mean = _mm_exact_rhs(y, ones) / RWKV_N
    yc = y - mean
    var = _mm_exact_rhs(yc * yc, ones) / RWKV_N
    out = yc * lax.rsqrt(var + RWKV_LN_EPS) * lw_ref[...] + lb_ref[...]
    bonus = _mm_exact_rhs(r * k * rk_ref[...], ones)
    y_ref[0] = out + bonus * v

    @pl.when(j == pl.num_programs(1) - 1)
    def _():
        sn_ref[0] = st_ref[...]


def _rwkv(zd, shift0, s0, p, *, tc, chunk, t_valid, passes):
    b, t, _ = zd.shape
    vec = lambda n: pl.BlockSpec((1, n), lambda i, j: (0, 0))
    lora = pl.BlockSpec((LANES, W_MIX), lambda i, j: (0, 0))
    state = pl.BlockSpec((1, RWKV_HEADS, RWKV_N, RWKV_N), lambda i, j: (i, 0, 0, 0))
    return pl.pallas_call(
        functools.partial(_rwkv_body, tc=tc, chunk=chunk, t_valid=t_valid, passes=passes),
        grid=(b, t // tc),
        in_specs=[
            pl.BlockSpec((1, tc, ZD_W), lambda i, j: (i, j, 0)),
            pl.BlockSpec((1, 1, ZD_W), lambda i, j: (i, 0, 0)),
            state,
            vec(ZD_W), vec(W_MIX), lora, vec(W_MIX), lora, vec(W_MIX), vec(W_MIX), vec(W_MIX),
            vec(W_MIX), vec(W_MIX),
        ],
        out_specs=[pl.BlockSpec((1, tc, W_MIX), lambda i, j: (i, j, 0)), state],
        out_shape=[jax.ShapeDtypeStruct((b, t, W_MIX), f32),
                   jax.ShapeDtypeStruct((b, RWKV_HEADS, RWKV_N, RWKV_N), f32)],
        scratch_shapes=[pltpu.VMEM((RWKV_HEADS, RWKV_N, RWKV_N), f32), pltpu.VMEM((1, ZD_W), f32)],
        compiler_params=_params("parallel", "arbitrary"),
        name="rwkv7",
    )(zd, shift0, s0, p["mu"], p["w0"], p["w2"], p["a0"], p["a2"], p["k_k"], p["k_a"], p["r_k"],
      p["ln_w"], p["ln_b"])


def _block_diag(w):
    n, c, _ = w.shape
    out = jnp.zeros((n * c, n * c), w.dtype)
    for i in range(n):
        out = out.at[i * c:(i + 1) * c, i * c:(i + 1) * c].set(w[i])
    return out


def _rel_bucket(dist):
    max_exact = REL_BUCKETS // 2
    n = jnp.maximum(dist, 0)
    log_ratio = jnp.log(jnp.maximum(n, 1).astype(f32) / max_exact) / math.log(REL_MAX_DIST / max_exact)
    large = jnp.minimum(max_exact + (log_ratio * (REL_BUCKETS - max_exact)).astype(jnp.int32), REL_BUCKETS - 1)
    return jnp.where(n < max_exact, n, large)


def _swa_bias(table, tq):
    dist = WINDOW + jnp.arange(tq)[:, None] - jnp.arange(WINDOW + tq)[None, :]
    valid = (dist >= 0) & (dist <= WINDOW)
    b = jnp.transpose(table[_rel_bucket(dist)].astype(f32), (2, 0, 1))
    return jnp.where(valid[None], b, -jnp.inf)


def _pack_layer(i, w_in, lru_conv_w, lru_conv_b, lru_wa, lru_ba, lru_wx, lru_bx, lru_lambda, lru_norm,
                swa_sinks, swa_norm, gla_w_up, gla_b, gla_norm, rwkv_mu, rwkv_w0, rwkv_w2, rwkv_a0, rwkv_a2,
                rwkv_k_k, rwkv_k_a, rwkv_r_k, rwkv_ln_w, rwkv_ln_b, w_out, ple_proj, ple_gate,
                norm_pre, norm_post):
    w = w_in[i]
    c_ca = 5 * W_MIX
    c_ud = c_ca + GLA_RANK
    c_gate = c_ud + RWKV_SHIFT
    pad = jnp.zeros((D_MODEL, ZD_W - RWKV_SHIFT - GLA_RANK), w.dtype)
    w_packed = jnp.concatenate([w[:, :c_ca], w[:, c_ud:c_gate], w[:, c_ca:c_ud], pad, w[:, c_gate:]], axis=1)
    row = lambda a: a.reshape(1, -1)
    lora_lo = RWKV_SHIFT - 2 * RWKV_LORA - ZD_TAIL
    ca_lo = RWKV_SHIFT - ZD_TAIL
    zeros_tail = jnp.zeros((LANES, W_MIX), f32)
    return dict(
        w_in=w_packed.astype(bf16),
        norm_pre=row(norm_pre[i]), norm_post=row(norm_post[i]),
        w_out=w_out[i].astype(bf16), ple_proj=ple_proj[i].astype(bf16), ple_gate=ple_gate[i].astype(bf16),
        lru=dict(conv_w=lru_conv_w[i], conv_b=row(lru_conv_b[i]), wa=_block_diag(lru_wa[i]), ba=row(lru_ba[i]),
                 wx=_block_diag(lru_wx[i]), bx=row(lru_bx[i]), lam=row(lru_lambda[i]), norm=row(lru_norm[i])),
        swa_sinks=row(swa_sinks[i]), swa_norm=row(swa_norm[i]),
        gla=dict(w_up=jnp.zeros((LANES, GLA_HEADS * GLA_DK), f32).at[ca_lo:ca_lo + GLA_RANK].set(gla_w_up[i]),
                 b=row(gla_b[i]), norm=row(jnp.tile(gla_norm[i], GLA_HEADS))),
        rwkv=dict(mu=row(jnp.pad(rwkv_mu[i], (0, ZD_W - RWKV_SHIFT))), w0=row(rwkv_w0[i]),
                  w2=zeros_tail.at[lora_lo:lora_lo + RWKV_LORA].set(rwkv_w2[i]), a0=row(rwkv_a0[i]),
                  a2=zeros_tail.at[lora_lo + RWKV_LORA:lora_lo + 2 * RWKV_LORA].set(rwkv_a2[i]),
                  k_k=row(rwkv_k_k[i]), k_a=row(rwkv_k_a[i]), r_k=row(rwkv_r_k[i]),
                  ln_w=row(rwkv_ln_w[i]), ln_b=row(rwkv_ln_b[i])),
    )


RWKV_PASSES = (3, 3, 3)


def _layer(x, pe, lp, bias, state, *, tc, chunk, tq, t_valid):
    b, t, _ = x.shape
    prompt = state is None
    rows = b * t
    za, zb, zc, zd, zg = _inproj(x.reshape(rows, D_MODEL), lp["norm_pre"], lp["w_in"])
    za = za.reshape(b, t, ZA_W)
    zb = zb.reshape(b, t, ZB_W)
    zc = zc.reshape(b, t, ZC_W)
    zd = zd.reshape(b, t, ZD_W)
    if prompt:
        h0 = jnp.zeros((b, 1, W_MIX), f32)
        conv0 = jnp.zeros((b, SUBLANES, W_MIX), f32)
        kbuf = vbuf = zb
        sg0 = jnp.zeros((b, GLA_HEADS, GLA_DV, GLA_DK), f32)
        sr0 = jnp.zeros((b, RWKV_HEADS, RWKV_N, RWKV_N), f32)
        sh0 = jnp.zeros((b, 1, ZD_W), f32)
    else:
        h0, conv0, kbuf, vbuf, sg0, sr0, sh0 = state
        h0 = h0.reshape(b, 1, W_MIX)
        conv0 = jnp.pad(conv0, ((0, 0), (SUBLANES - (LRU_CONV - 1), 0), (0, 0)))
        kbuf = kbuf.reshape(b, WINDOW, SWA_KV_HEADS * SWA_HD)
        vbuf = vbuf.reshape(b, WINDOW, SWA_KV_HEADS * SWA_HD)
        sg0 = jnp.swapaxes(sg0, 2, 3)
        sh0 = jnp.pad(sh0, ((0, 0), (0, ZD_W - RWKV_SHIFT))).reshape(b, 1, ZD_W)
    ya, h_new = _lru(za, conv0, h0, lp["lru"], tc=tc, t_valid=t_valid, reset_first=prompt)
    yb = _swa(zb, kbuf, vbuf, bias, lp["swa_sinks"], lp["swa_norm"], tq=tq, own_prev=prompt)
    yc, sg_new = _gla(zc, zd, sg0, lp["gla"], tc=tc, chunk=chunk, t_valid=t_valid)
    yd, sr_new = _rwkv(zd, sh0, sr0, lp["rwkv"], tc=tc, chunk=chunk, t_valid=t_valid, passes=RWKV_PASSES)
    flat = lambda a: a.reshape(rows, a.shape[-1])
    x_new = _outproj(flat(ya), flat(yb), flat(yc), flat(yd), zg, flat(x), flat(pe),
                     lp["w_out"], lp["norm_post"], lp["ple_proj"], lp["ple_gate"]).reshape(b, t, D_MODEL)
    kv_rows = slice(t_valid - WINDOW, t_valid) if prompt else slice(0, t_valid)
    kw = SWA_KV_HEADS * SWA_HD
    k_new = zb[:, kv_rows, W_MIX:W_MIX + kw].reshape(b, -1, SWA_KV_HEADS, SWA_HD)
    v_new = zb[:, kv_rows, W_MIX + kw:W_MIX + 2 * kw].reshape(b, -1, SWA_KV_HEADS, SWA_HD)
    new_state = (h_new.reshape(b, W_MIX), za[:, t_valid - (LRU_CONV - 1):t_valid], k_new, v_new,
                 jnp.swapaxes(sg_new, 2, 3), sr_new, zd[:, t_valid - 1, :RWKV_SHIFT])
    return x_new, new_state


def kernel(x_prompt, x_sample, state_lru_h, state_lru_conv, cache_swa_k, cache_swa_v, state_gla, state_rwkv, state_rwkv_shift, p_prompt, p_sample, rel_bias, norm_pre, norm_post, w_in, lru_conv_w, lru_conv_b, lru_wa, lru_ba, lru_wx, lru_bx, lru_lambda, lru_norm, swa_sinks, swa_norm, gla_w_up, gla_b, gla_norm, rwkv_mu, rwkv_w0, rwkv_w2, rwkv_a0, rwkv_a2, rwkv_k_k, rwkv_k_a, rwkv_r_k, rwkv_ln_w, rwkv_ln_b, w_out, ple_proj, ple_gate):
    depth = w_in.shape[0]
    t_p = x_prompt.shape[1]
    t_s = x_sample.shape[1]
    t_s_pad = -(-t_s // SUBLANES) * SUBLANES
    pad_t = lambda a, axis: jnp.pad(a, [(0, t_s_pad - t_s) if d == axis else (0, 0) for d in range(a.ndim)])
    xs = pad_t(x_sample, 1)
    ps = pad_t(p_sample, 2)
    xp = x_prompt
    bias_p = _swa_bias(rel_bias, WINDOW)
    bias_s = _swa_bias(rel_bias, t_s_pad)
    sts_p, sts_s = [], []
    for i in range(depth):
        lp = _pack_layer(i, w_in, lru_conv_w, lru_conv_b, lru_wa, lru_ba, lru_wx, lru_bx, lru_lambda, lru_norm,
                         swa_sinks, swa_norm, gla_w_up, gla_b, gla_norm, rwkv_mu, rwkv_w0, rwkv_w2, rwkv_a0,
                         rwkv_a2, rwkv_k_k, rwkv_k_a, rwkv_r_k, rwkv_ln_w, rwkv_ln_b, w_out, ple_proj, ple_gate,
                         norm_pre, norm_post)
        xp, st_p = _layer(xp, p_prompt[i], lp, bias_p, None,
                          tc=SEQ_TILE, chunk=CHUNK, tq=WINDOW, t_valid=t_p)
        xs, st_s = _layer(xs, ps[i], lp, bias_s,
                          (state_lru_h[i], state_lru_conv[i], cache_swa_k[i], cache_swa_v[i],
                           state_gla[i], state_rwkv[i], state_rwkv_shift[i]),
                          tc=t_s_pad, chunk=t_s_pad, tq=t_s_pad, t_valid=t_s)
        sts_p.append(st_p)
        sts_s.append(st_s)
    hp, cp, kp, vp, gp, rp, shp = [jnp.stack([s[j] for s in sts_p]) for j in range(7)]
    hs, cs, ks, vs, gs, rs, shs = [jnp.stack([s[j] for s in sts_s]) for j in range(7)]
    return (xp, xs[:, :t_s], hp, hs, cp, cs, kp, ks, vp, vs, gp, gs, rp, rs, shp, shs)
```

```python
import functools
import math

import jax
import jax.numpy as jnp
from jax import lax
from jax.experimental import pallas as pl
from jax.experimental.pallas import tpu as pltpu

f32 = jnp.float32
bf16 = jnp.bfloat16

D_MODEL = 1024
D_PLE = 256
RMS_EPS = 1e-6
W_MIX = 256
LRU_BLOCKS = 4
LRU_CONV = 4
LRU_C = 8.0
SWA_HEADS = 4
SWA_KV_HEADS = 2
SWA_HD = 64
WINDOW = 128
REL_BUCKETS = 32
REL_MAX_DIST = 128
GLA_HEADS = 4
GLA_DK = 32
GLA_DV = 64
GLA_RANK = 16
GLA_TAU = 16.0
RWKV_HEADS = 4
RWKV_N = 64
RWKV_LORA = 32
RWKV_SHIFT = 3 * W_MIX + 2 * RWKV_LORA
RWKV_LN_EPS = 64e-5

ZA_W = 256
ZB_W = 512
ZC_W = 512
ZD_W = 896
ZG_W = 1024
ZD_TAIL = 768
N_IN_PACKED = ZA_W + ZB_W + ZC_W + ZD_W + ZG_W

LANES = 128
SUBLANES = 8
VMEM_LIMIT = 48 * 1024 * 1024

ROW_TILE = 512

NN = (((1,), (0,)), ((), ()))
NT = (((1,), (1,)), ((), ()))
TN = (((0,), (0,)), ((), ()))


def _dot(a, b, dn):
    return lax.dot_general(a, b, dn, preferred_element_type=f32)


def _split2(x):
    hi = x.astype(bf16)
    lo = (x - hi.astype(f32)).astype(bf16)
    return hi, lo


def _mm(a, b, dn=NN, passes=1):
    if passes == 1:
        return _dot(a.astype(bf16), b.astype(bf16), dn)
    ah, al = _split2(a)
    bh, bl = _split2(b)
    return _dot(ah, bh, dn) + (_dot(ah, bl, dn) + _dot(al, bh, dn))


def _mm_exact_rhs(a, b_bf16):
    a1 = a.astype(bf16)
    r1 = a - a1.astype(f32)
    a2 = r1.astype(bf16)
    a3 = (r1 - a2.astype(f32)).astype(bf16)
    return _dot(a1, b_bf16, NN) + (_dot(a2, b_bf16, NN) + _dot(a3, b_bf16, NN))


def _head_ones(n, width):
    r = lax.broadcasted_iota(jnp.int32, (n, n), 0) // width
    c = lax.broadcasted_iota(jnp.int32, (n, n), 1) // width
    return jnp.where(r == c, 1.0, 0.0).astype(bf16)


def _rms(x, g):
    return x * lax.rsqrt(jnp.mean(x * x, axis=-1, keepdims=True) + RMS_EPS) * g


def _seg_cumsum(x, seg):
    rows = lax.broadcasted_iota(jnp.int32, x.shape, 0) % seg
    d = 1
    while d < seg:
        x = x + jnp.where(rows >= d, pltpu.roll(x, d, 0), 0.0)
        d *= 2
    return x


def _cat(xs, axis):
    return xs[0] if len(xs) == 1 else jnp.concatenate(xs, axis=axis)


def _params(*sem):
    return pltpu.CompilerParams(dimension_semantics=sem, vmem_limit_bytes=VMEM_LIMIT)


def _inproj_body(x_ref, g_ref, w_ref, za_ref, zb_ref, zc_ref, zd_ref, zg_ref):
    h = _rms(x_ref[...], g_ref[...]).astype(bf16)
    off = 0
    for o_ref in (za_ref, zb_ref, zc_ref, zd_ref, zg_ref):
        n = o_ref.shape[-1]
        o_ref[...] = _dot(h, w_ref[:, off:off + n], NN)
        off += n


def _inproj(x, g, w):
    rows = x.shape[0]
    tm = min(ROW_TILE, rows)
    widths = (ZA_W, ZB_W, ZC_W, ZD_W, ZG_W)
    return pl.pallas_call(
        _inproj_body,
        grid=(rows // tm,),
        in_specs=[
            pl.BlockSpec((tm, D_MODEL), lambda i: (i, 0)),
            pl.BlockSpec((1, D_MODEL), lambda i: (0, 0)),
            pl.BlockSpec((D_MODEL, N_IN_PACKED), lambda i: (0, 0)),
        ],
        out_specs=[pl.BlockSpec((tm, n), lambda i: (i, 0)) for n in widths],
        out_shape=[jax.ShapeDtypeStruct((rows, n), f32) for n in widths],
        compiler_params=_params("parallel"),
        name="in_proj",
    )(x, g, w)


def _outproj_body(ya_ref, yb_ref, yc_ref, yd_ref, zg_ref, x_ref, pe_ref, wo_ref, gp_ref, wp_ref, wg_ref, o_ref):
    g = zg_ref[...]
    sg = g * jax.nn.sigmoid(g)
    acc = None
    for i, y_ref in enumerate((ya_ref, yb_ref, yc_ref, yd_ref)):
        lo = i * W_MIX
        y = (y_ref[...] * sg[:, lo:lo + W_MIX]).astype(bf16)
        part = _dot(y, wo_ref[lo:lo + W_MIX, :], NN)
        acc = part if acc is None else acc + part
    x1 = x_ref[...] + _rms(acc, gp_ref[...])
    emb = _dot(pe_ref[...].astype(bf16), wp_ref[...], NN)
    gate = jax.nn.sigmoid(_dot(x1.astype(bf16), wg_ref[...], NN))
    o_ref[...] = x1 + emb * gate


def _outproj(ya, yb, yc, yd, zg, x, pe, wo, gp, wp, wg):
    rows = x.shape[0]
    tm = min(ROW_TILE, rows)
    row = lambda n: pl.BlockSpec((tm, n), lambda i: (i, 0))
    full = lambda a: pl.BlockSpec(a.shape, lambda i: (0, 0))
    return pl.pallas_call(
        _outproj_body,
        grid=(rows // tm,),
        in_specs=[row(W_MIX), row(W_MIX), row(W_MIX), row(W_MIX), row(ZG_W), row(D_MODEL), row(D_PLE),
                  full(wo), full(gp), full(wp), full(wg)],
        out_specs=row(D_MODEL),
        out_shape=jax.ShapeDtypeStruct((rows, D_MODEL), f32),
        compiler_params=_params("parallel"),
        name="out_proj",
    )(ya, yb, yc, yd, zg, x, pe, wo, gp, wp, wg)


def _lru_body(x_ref, c0_ref, h0_ref, cw_ref, cb_ref, wa_ref, ba_ref, wx_ref, bx_ref, lam_ref, gn_ref,
              y_ref, hl_ref, ext_ref, hc_ref, *, bb, tc, t_valid, reset_first):
    j = pl.program_id(1)

    @pl.when(j == 0)
    def _():
        ext_ref[:, 0:SUBLANES, :] = c0_ref[...]
        hc_ref[...] = h0_ref[...]

    @pl.when(j > 0)
    def _():
        ext_ref[:, 0:SUBLANES, :] = ext_ref[:, tc:tc + SUBLANES, :]

    n = bb * tc
    ext_ref[:, SUBLANES:SUBLANES + tc, :] = x_ref[...]
    xc = cw_ref[3:4, :] * x_ref[...].reshape(n, W_MIX) + cb_ref[...]
    for k in range(LRU_CONV - 1):
        s = SUBLANES - (LRU_CONV - 1) + k
        xc = xc + cw_ref[k:k + 1, :] * _cat([ext_ref[i, s:s + tc, :] for i in range(bb)], 0)
    gate_r = _mm(xc, wa_ref[...]) + ba_ref[...]
    gate_i = _mm(xc, wx_ref[...]) + bx_ref[...]
    log_a = -LRU_C * jax.nn.sigmoid(gate_r) * jax.nn.softplus(-lam_ref[...])
    a = jnp.exp(log_a)
    mult = jnp.sqrt(1.0 - a * a)
    rows = lax.broadcasted_iota(jnp.int32, (n, W_MIX), 0) % tc
    if reset_first:
        mult = jnp.where(rows + j * tc == 0, 1.0, mult)
    b = mult * jax.nn.sigmoid(gate_i) * xc
    d = 1
    while d < tc:
        m = rows >= d
        a_s = pltpu.roll(a, d, 0)
        b_s = pltpu.roll(b, d, 0)
        b = jnp.where(m, a * b_s + b, b)
        a = jnp.where(m, a * a_s, a)
        d *= 2
    h = a * _cat([jnp.broadcast_to(hc_ref[i], (tc, W_MIX)) for i in range(bb)], 0) + b
    for i in range(bb):
        hc_ref[i] = h[(i + 1) * tc - 1:(i + 1) * tc, :]
    y_ref[...] = _rms(h, gn_ref[...]).reshape(bb, tc, W_MIX)
    r_last = (t_valid - 1) % tc

    @pl.when(j == (t_valid - 1) // tc)
    def _():
        for i in range(bb):
            hl_ref[i] = h[i * tc + r_last:i * tc + r_last + 1, :]


def _lru(za, conv0, h0, p, *, bb, tc, t_valid, reset_first):
    b, t, _ = za.shape
    vec = pl.BlockSpec((1, W_MIX), lambda i, j: (0, 0))
    mat = pl.BlockSpec((W_MIX, W_MIX), lambda i, j: (0, 0))
    return pl.pallas_call(
        functools.partial(_lru_body, bb=bb, tc=tc, t_valid=t_valid, reset_first=reset_first),
        grid=(b // bb, t // tc),
        in_specs=[
            pl.BlockSpec((bb, tc, W_MIX), lambda i, j: (i, j, 0)),
            pl.BlockSpec((bb, SUBLANES, W_MIX), lambda i, j: (i, 0, 0)),
            pl.BlockSpec((bb, 1, W_MIX), lambda i, j: (i, 0, 0)),
            pl.BlockSpec((LRU_CONV, W_MIX), lambda i, j: (0, 0)),
            vec, mat, vec, mat, vec, vec, vec,
        ],
        out_specs=[
            pl.BlockSpec((bb, tc, W_MIX), lambda i, j: (i, j, 0)),
            pl.BlockSpec((bb, 1, W_MIX), lambda i, j: (i, 0, 0)),
        ],
        out_shape=[jax.ShapeDtypeStruct((b, t, W_MIX), f32), jax.ShapeDtypeStruct((b, 1, W_MIX), f32)],
        scratch_shapes=[pltpu.VMEM((bb, tc + SUBLANES, W_MIX), f32), pltpu.VMEM((bb, 1, W_MIX), f32)],
        compiler_params=_params("parallel", "arbitrary"),
        name="rglru",
    )(za, conv0, h0, p["conv_w"], p["conv_b"], p["wa"], p["ba"], p["wx"], p["bx"], p["lam"], p["norm"])


def _expand_kv(x):
    g = SWA_HEADS // SWA_KV_HEADS
    return jnp.concatenate([x[:, (h // g) * SWA_HD:(h // g + 1) * SWA_HD] for h in range(SWA_HEADS)], axis=1)


def _swa_body(q_ref, ko_ref, vo_ref, kp_ref, vp_ref, bias_ref, sink_ref, gn_ref, y_ref, *, bb, tq, mask_first):
    n = pl.program_id(1)
    nk = WINDOW + tq
    lane_head = lax.broadcasted_iota(jnp.int32, (1, W_MIX), 1) // SWA_HD
    row_head = lax.broadcasted_iota(jnp.int32, (SWA_HEADS * tq, 1), 0) // tq
    sink = jnp.zeros((SWA_HEADS * tq, 1), f32)
    for h in range(SWA_HEADS):
        sink = jnp.where(row_head == h, sink_ref[:, h:h + 1], sink)
    bias = bias_ref[...]
    if mask_first:
        col = lax.broadcasted_iota(jnp.int32, (1, nk), 1)
        bias = jnp.where((n > 0) | (col >= WINDOW), bias, -jnp.inf)
    for i in range(bb):
        q = q_ref[i] * (SWA_HD ** -0.5)
        qs = jnp.concatenate([jnp.where(lane_head == h, q, 0.0) for h in range(SWA_HEADS)], axis=0)
        kx = _expand_kv(jnp.concatenate([kp_ref[i], ko_ref[i]], axis=0))
        vx = _expand_kv(jnp.concatenate([vp_ref[i], vo_ref[i]], axis=0))
        s = _mm(qs, kx, NT) + bias
        m = jnp.maximum(jnp.max(s, axis=-1, keepdims=True), sink)
        e = jnp.exp(s - m)
        den = jnp.sum(e, axis=-1, keepdims=True) + jnp.exp(sink - m)
        pv = _mm(e, vx) / den
        o = jnp.where(lane_head == 0, pv[0:tq], 0.0)
        for h in range(1, SWA_HEADS):
            o = o + jnp.where(lane_head == h, pv[h * tq:(h + 1) * tq], 0.0)
        y_ref[i] = _rms(o, gn_ref[...])


def _swa(zb, k_prev, v_prev, bias, sinks, gn, *, bb, tq, own_prev):
    b, t, _ = zb.shape
    kw = SWA_KV_HEADS * SWA_HD
    if own_prev:
        kp_spec = pl.BlockSpec((bb, WINDOW, kw), lambda i, n: (i, jnp.maximum(n - 1, 0), 2))
        vp_spec = pl.BlockSpec((bb, WINDOW, kw), lambda i, n: (i, jnp.maximum(n - 1, 0), 3))
    else:
        kp_spec = pl.BlockSpec((bb, WINDOW, kw), lambda i, n: (i, 0, 0))
        vp_spec = pl.BlockSpec((bb, WINDOW, kw), lambda i, n: (i, 0, 0))
    return pl.pallas_call(
        functools.partial(_swa_body, bb=bb, tq=tq, mask_first=own_prev),
        grid=(b // bb, t // tq),
        in_specs=[
            pl.BlockSpec((bb, tq, W_MIX), lambda i, n: (i, n, 0)),
            pl.BlockSpec((bb, tq, kw), lambda i, n: (i, n, 2)),
            pl.BlockSpec((bb, tq, kw), lambda i, n: (i, n, 3)),
            kp_spec, vp_spec,
            pl.BlockSpec(bias.shape, lambda i, n: (0, 0)),
            pl.BlockSpec((1, SWA_HEADS), lambda i, n: (0, 0)),
            pl.BlockSpec((1, W_MIX), lambda i, n: (0, 0)),
        ],
        out_specs=pl.BlockSpec((bb, tq, W_MIX), lambda i, n: (i, n, 0)),
        out_shape=jax.ShapeDtypeStruct((b, t, W_MIX), f32),
        compiler_params=_params("parallel", "arbitrary"),
        name="swa",
    )(zb, zb, zb, k_prev, v_prev, bias, sinks, gn)


def _gla_body(q_ref, k_ref, v_ref, t_ref, s0_ref, wup_ref, gb_ref, gn_ref, y_ref, sn_ref, st_ref,
              *, bb, tc, chunk, t_valid):
    j = pl.program_id(1)

    @pl.when(j == 0)
    def _():
        st_ref[...] = s0_ref[...]

    n = bb * tc
    qk = GLA_HEADS * GLA_DK
    rows = lax.broadcasted_iota(jnp.int32, (n, 1), 0) % tc
    live = rows + j * tc < t_valid
    la = jax.nn.log_sigmoid(_mm(t_ref[...].reshape(n, LANES), wup_ref[...], passes=3) + gb_ref[...]) / GLA_TAU
    la = jnp.where(live, la, 0.0)
    k = jnp.where(live, k_ref[...].reshape(n, qk), 0.0)
    v = jnp.where(live, v_ref[...].reshape(n, W_MIX), 0.0)
    bc = _seg_cumsum(la, chunk)
    q_dec = q_ref[...].reshape(n, qk) * (GLA_DK ** -0.5) * jnp.exp(bc)
    k_inv = k * jnp.exp(-bc)
    khead = lax.broadcasted_iota(jnp.int32, (1, qk), 1) // GLA_DK
    vhead = lax.broadcasted_iota(jnp.int32, (1, W_MIX), 1) // GLA_DV
    r2 = lax.broadcasted_iota(jnp.int32, (GLA_HEADS * tc, tc), 0) % tc
    c2 = lax.broadcasted_iota(jnp.int32, (GLA_HEADS * tc, tc), 1)
    causal = (r2 // chunk == c2 // chunk) & (r2 >= c2)
    sr = lax.broadcasted_iota(jnp.int32, (W_MIX, qk), 0) // GLA_DV
    sc = lax.broadcasted_iota(jnp.int32, (W_MIX, qk), 1) // GLA_DK
    diag = sr == sc
    nch = tc // chunk
    outs = []
    for i in range(bb):
        ts = slice(i * tc, (i + 1) * tc)
        qs = jnp.concatenate([jnp.where(khead == h, q_dec[ts], 0.0) for h in range(GLA_HEADS)], axis=0)
        att = jnp.where(causal, _mm(qs, k_inv[ts], NT), 0.0)
        full = _mm(att, v[ts])
        o = jnp.where(vhead == 0, full[0:tc], 0.0)
        for h in range(1, GLA_HEADS):
            o = o + jnp.where(vhead == h, full[h * tc:(h + 1) * tc], 0.0)
        st = st_ref[i]
        inter = []
        for c in range(nch):
            rs = slice(i * tc + c * chunk, i * tc + (c + 1) * chunk)
            b_end = bc[i * tc + (c + 1) * chunk - 1:i * tc + (c + 1) * chunk, :]
            inter.append(_mm(q_dec[rs], st, NT))
            k_end = k[rs] * jnp.exp(b_end - bc[rs])
            st = st * jnp.exp(b_end) + jnp.where(diag, _mm(v[rs], k_end, TN), 0.0)
        st_ref[i] = st
        outs.append(o + _cat(inter, 0))
    o = _cat(outs, 0)
    ms = _mm_exact_rhs(o * o, _head_ones(W_MIX, GLA_DV)) / GLA_DV
    y_ref[...] = (o * lax.rsqrt(ms + RMS_EPS) * gn_ref[...]).reshape(bb, tc, W_MIX)

    @pl.when(j == pl.num_programs(1) - 1)
    def _():
        sn_ref[...] = st_ref[...]


def _gla(zc, zd, s0, p, *, bb, tc, chunk, t_valid):
    b, t, _ = zc.shape
    qk = GLA_HEADS * GLA_DK
    state = pl.BlockSpec((bb, W_MIX, qk), lambda i, j: (i, 0, 0))
    return pl.pallas_call(
        functools.partial(_gla_body, bb=bb, tc=tc, chunk=chunk, t_valid=t_valid),
        grid=(b // bb, t // tc),
        in_specs=[
            pl.BlockSpec((bb, tc, qk), lambda i, j: (i, j, 0)),
            pl.BlockSpec((bb, tc, qk), lambda i, j: (i, j, 1)),
            pl.BlockSpec((bb, tc, W_MIX), lambda i, j: (i, j, 1)),
            pl.BlockSpec((bb, tc, LANES), lambda i, j: (i, j, ZD_TAIL // LANES)),
            state,
            pl.BlockSpec((LANES, qk), lambda i, j: (0, 0)),
            pl.BlockSpec((1, qk), lambda i, j: (0, 0)),
            pl.BlockSpec((1, W_MIX), lambda i, j: (0, 0)),
        ],
        out_specs=[pl.BlockSpec((bb, tc, W_MIX), lambda i, j: (i, j, 0)), state],
        out_shape=[jax.ShapeDtypeStruct((b, t, W_MIX), f32), jax.ShapeDtypeStruct((b, W_MIX, qk), f32)],
        scratch_shapes=[pltpu.VMEM((bb, W_MIX, qk), f32)],
        compiler_params=_params("parallel", "arbitrary"),
        name="gla",
    )(zc, zc, zc, zd, s0, p["w_up"], p["b"], p["norm"])


def _gla_state_pack(s):
    b = s.shape[0]
    out = jnp.zeros((b, W_MIX, GLA_HEADS * GLA_DK), f32)
    for h in range(GLA_HEADS):
        out = out.at[:, h * GLA_DV:(h + 1) * GLA_DV, h * GLA_DK:(h + 1) * GLA_DK].set(jnp.swapaxes(s[:, h], 1, 2))
    return out


def _gla_state_unpack(s):
    return jnp.stack([jnp.swapaxes(s[:, h * GLA_DV:(h + 1) * GLA_DV, h * GLA_DK:(h + 1) * GLA_DK], 1, 2)
                      for h in range(GLA_HEADS)], axis=1)


def _rwkv_body(u_ref, sh0_ref, s0_ref, mu_ref, w0_ref, w2_ref, a0_ref, a2_ref, kk_ref, ka_ref, rk_ref,
               lw_ref, lb_ref, y_ref, sn_ref, st_ref, up_ref, *, bb, tc, chunk, t_valid):
    j = pl.program_id(1)

    @pl.when(j == 0)
    def _():
        st_ref[...] = s0_ref[...]
        up_ref[...] = sh0_ref[...]

    n = bb * tc
    u = u_ref[...].reshape(n, ZD_W)
    rows = lax.broadcasted_iota(jnp.int32, (n, 1), 0)
    u_prev = pltpu.roll(u, 1, 0)
    for i in range(bb):
        u_prev = jnp.where(rows == i * tc, up_ref[i], u_prev)
        up_ref[i] = u[(i + 1) * tc - 1:(i + 1) * tc, :]
    us = u + (u_prev - u) * mu_ref[...]
    r = us[:, 0:W_MIX]
    k = us[:, W_MIX:2 * W_MIX]
    v = us[:, 2 * W_MIX:3 * W_MIX]
    tail = us[:, ZD_TAIL:ZD_TAIL + LANES]
    w = -jax.nn.softplus(-(w0_ref[...] + _mm(jnp.tanh(tail), w2_ref[...], passes=3))) - 0.5
    lw = -jnp.exp(w)
    a = jax.nn.sigmoid(a0_ref[...] + _mm(tail, a2_ref[...], passes=3))
    ones = _head_ones(W_MIX, RWKV_N)
    kk = k * kk_ref[...]
    kk = kk / jnp.maximum(jnp.sqrt(_mm_exact_rhs(kk * kk, ones)), 1e-12)
    k = k * (1.0 + (a - 1.0) * ka_ref[...])
    live = rows % tc + j * tc < t_valid
    lw = jnp.where(live, lw, 0.0)
    aa = jnp.where(live, -kk, 0.0)
    bl = jnp.where(live, kk * a, 0.0)
    kl = jnp.where(live, k, 0.0)
    vl = jnp.where(live, v, 0.0)

    cum = _seg_cumsum(lw, chunk)
    gi = jnp.exp(-cum)
    a_t = aa * jnp.exp(cum - lw)
    b_t = bl * gi
    k_t = kl * gi
    r_t = r * jnp.exp(cum)
    nch = tc // chunk
    r2 = lax.broadcasted_iota(jnp.int32, (tc, tc), 0)
    c2 = lax.broadcasted_iota(jnp.int32, (tc, tc), 1)
    same = r2 // chunk == c2 // chunk
    strict = same & (r2 > c2)
    incl = same & (r2 >= c2)

    chains = [(i, h) for i in range(bb) for h in range(RWKV_HEADS)]
    nq = len(chains)
    sl = lambda x, i, h: x[i * tc:(i + 1) * tc, h * RWKV_N:(h + 1) * RWKV_N]
    ah = [sl(a_t, i, h) for i, h in chains]
    rh = [sl(r_t, i, h) for i, h in chains]
    vh = [sl(vl, i, h) for i, h in chains]
    pair = [_mm(jnp.concatenate([ah[q], rh[q]], axis=0),
                jnp.concatenate([sl(b_t, i, h), sl(k_t, i, h)], axis=0), NT) for q, (i, h) in enumerate(chains)]
    l_ab = [jnp.where(strict, m[0:tc, 0:tc], 0.0) for m in pair]
    l_ak = [jnp.where(strict, m[0:tc, tc:2 * tc], 0.0) for m in pair]
    m_rb = [jnp.where(incl, m[tc:2 * tc, 0:tc], 0.0) for m in pair]
    m_rk = [jnp.where(incl, m[tc:2 * tc, tc:2 * tc], 0.0) for m in pair]
    sol = [jnp.concatenate([ah[q], _mm(l_ak[q], vh[q])], axis=1) for q in range(nq)]
    y_v = [_mm(m_rk[q], vh[q]) for q in range(nq)]
    pw = l_ab
    d = 1
    while True:
        sol = [s + _mm(m, s) for s, m in zip(sol, pw)]
        d *= 2
        if d >= chunk:
            break
        pw = [_mm(m, m) for m in pw]
    lhs_c, p2_c, mrb_c, yv_c, bdt_c, kv_c, gcol_c = [], [], [], [], [], [], []
    for c in range(nch):
        rs = slice(c * chunk, (c + 1) * chunk)
        ce = [sl(cum, i, h)[(c + 1) * chunk - 1:(c + 1) * chunk, :] for i, h in chains]
        dec = [jnp.exp(ce[q] - sl(cum, i, h)[rs]) for q, (i, h) in enumerate(chains)]
        lhs_c.append([jnp.concatenate([sol[q][rs, 0:RWKV_N], rh[q][rs]], axis=0) for q in range(nq)])
        p2_c.append([sol[q][rs, RWKV_N:2 * RWKV_N] for q in range(nq)])
        mrb_c.append([m_rb[q][rs, rs] for q in range(nq)])
        yv_c.append([y_v[q][rs] for q in range(nq)])
        bdt_c.append([(sl(bl, i, h)[rs] * dec[q]).T for q, (i, h) in enumerate(chains)])
        kv_c.append([_mm(sl(kl, i, h)[rs] * dec[q], vh[q][rs], TN) for q, (i, h) in enumerate(chains)])
        gcol_c.append([jnp.broadcast_to(jnp.exp(ce[q]), (RWKV_N, RWKV_N)).T for q in range(nq)])
    st = [st_ref[i, h] for i, h in chains]
    y_parts = [[] for _ in chains]
    for c in range(nch):
        both = [_mm(lhs_c[c][q], st[q]) for q in range(nq)]
        u_c = [both[q][0:chunk] + p2_c[c][q] for q in range(nq)]
        st = [gcol_c[c][q] * st[q] + kv_c[c][q] + _mm(bdt_c[c][q], u_c[q]) for q in range(nq)]
        for q in range(nq):
            y_parts[q].append(both[q][chunk:2 * chunk] + _mm(mrb_c[c][q], u_c[q]) + yv_c[c][q])
    for q, (i, h) in enumerate(chains):
        st_ref[i, h] = st[q]
    y = _cat([_cat([_cat(y_parts[i * RWKV_HEADS + h], 0) for h in range(RWKV_HEADS)], 1) for i in range(bb)], 0)
    mean = _mm_exact_rhs(y, ones) / RWKV_N
    yc = y - mean
    var = _mm_exact_rhs(yc * yc, ones) / RWKV_N
    out = yc * lax.rsqrt(var + RWKV_LN_EPS) * lw_ref[...] + lb_ref[...]
    bonus = _mm_exact_rhs(r * k * rk_ref[...], ones)
    y_ref[...] = (out + bonus * v).reshape(bb, tc, W_MIX)

    @pl.when(j == pl.num_programs(1) - 1)
    def _():
        sn_ref[...] = st_ref[...]


def _rwkv(zd, shift0, s0, p, *, bb, tc, chunk, t_valid):
    b, t, _ = zd.shape
    vec = lambda n: pl.BlockSpec((1, n), lambda i, j: (0, 0))
    lora = pl.BlockSpec((LANES, W_MIX), lambda i, j: (0, 0))
    state = pl.BlockSpec((bb, RWKV_HEADS, RWKV_N, RWKV_N), lambda i, j: (i, 0, 0, 0))
    return pl.pallas_call(
        functools.partial(_rwkv_body, bb=bb, tc=tc, chunk=chunk, t_valid=t_valid),
        grid=(b // bb, t // tc),
        in_specs=[
            pl.BlockSpec((bb, tc, ZD_W), lambda i, j: (i, j, 0)),
            pl.BlockSpec((bb, 1, ZD_W), lambda i, j: (i, 0, 0)),
            state,
            vec(ZD_W), vec(W_MIX), lora, vec(W_MIX), lora, vec(W_MIX), vec(W_MIX), vec(W_MIX),
            vec(W_MIX), vec(W_MIX),
        ],
        out_specs=[pl.BlockSpec((bb, tc, W_MIX), lambda i, j: (i, j, 0)), state],
        out_shape=[jax.ShapeDtypeStruct((b, t, W_MIX), f32),
                   jax.ShapeDtypeStruct((b, RWKV_HEADS, RWKV_N, RWKV_N), f32)],
        scratch_shapes=[pltpu.VMEM((bb, RWKV_HEADS, RWKV_N, RWKV_N), f32), pltpu.VMEM((bb, 1, ZD_W), f32)],
        compiler_params=_params("parallel", "arbitrary"),
        name="rwkv7",
    )(zd, shift0, s0, p["mu"], p["w0"], p["w2"], p["a0"], p["a2"], p["k_k"], p["k_a"], p["r_k"],
      p["ln_w"], p["ln_b"])


def _block_diag(w):
    n, c, _ = w.shape
    out = jnp.zeros((n * c, n * c), w.dtype)
    for i in range(n):
        out = out.at[i * c:(i + 1) * c, i * c:(i + 1) * c].set(w[i])
    return out


def _rel_bucket(dist):
    max_exact = REL_BUCKETS // 2
    n = jnp.maximum(dist, 0)
    log_ratio = jnp.log(jnp.maximum(n, 1).astype(f32) / max_exact) / math.log(REL_MAX_DIST / max_exact)
    large = jnp.minimum(max_exact + (log_ratio * (REL_BUCKETS - max_exact)).astype(jnp.int32), REL_BUCKETS - 1)
    return jnp.where(n < max_exact, n, large)


def _swa_bias(table, tq):
    dist = WINDOW + jnp.arange(tq)[:, None] - jnp.arange(WINDOW + tq)[None, :]
    valid = (dist >= 0) & (dist <= WINDOW)
    b = jnp.transpose(table[_rel_bucket(dist)].astype(f32), (2, 0, 1))
    return jnp.where(valid[None], b, -jnp.inf).reshape(SWA_HEADS * tq, WINDOW + tq)


def _pack_layer(i, w_in, lru_conv_w, lru_conv_b, lru_wa, lru_ba, lru_wx, lru_bx, lru_lambda, lru_norm,
                swa_sinks, swa_norm, gla_w_up, gla_b, gla_norm, rwkv_mu, rwkv_w0, rwkv_w2, rwkv_a0, rwkv_a2,
                rwkv_k_k, rwkv_k_a, rwkv_r_k, rwkv_ln_w, rwkv_ln_b, w_out, ple_proj, ple_gate,
                norm_pre, norm_post):
    w = w_in[i]
    c_ca = 5 * W_MIX
    c_ud = c_ca + GLA_RANK
    c_gate = c_ud + RWKV_SHIFT
    pad = jnp.zeros((D_MODEL, ZD_W - RWKV_SHIFT - GLA_RANK), w.dtype)
    w_packed = jnp.concatenate([w[:, :c_ca], w[:, c_ud:c_gate], w[:, c_ca:c_ud], pad, w[:, c_gate:]], axis=1)
    row = lambda a: a.reshape(1, -1)
    lora_lo = RWKV_SHIFT - 2 * RWKV_LORA - ZD_TAIL
    ca_lo = RWKV_SHIFT - ZD_TAIL
    zeros_tail = jnp.zeros((LANES, W_MIX), f32)
    return dict(
        w_in=w_packed.astype(bf16),
        norm_pre=row(norm_pre[i]), norm_post=row(norm_post[i]),
        w_out=w_out[i].astype(bf16), ple_proj=ple_proj[i].astype(bf16), ple_gate=ple_gate[i].astype(bf16),
        lru=dict(conv_w=lru_conv_w[i], conv_b=row(lru_conv_b[i]), wa=_block_diag(lru_wa[i]), ba=row(lru_ba[i]),
                 wx=_block_diag(lru_wx[i]), bx=row(lru_bx[i]), lam=row(lru_lambda[i]), norm=row(lru_norm[i])),
        swa_sinks=row(swa_sinks[i]), swa_norm=row(swa_norm[i]),
        gla=dict(w_up=jnp.zeros((LANES, GLA_HEADS * GLA_DK), f32).at[ca_lo:ca_lo + GLA_RANK].set(gla_w_up[i]),
                 b=row(gla_b[i]), norm=row(jnp.tile(gla_norm[i], GLA_HEADS))),
        rwkv=dict(mu=row(jnp.pad(rwkv_mu[i], (0, ZD_W - RWKV_SHIFT))), w0=row(rwkv_w0[i]),
                  w2=zeros_tail.at[lora_lo:lora_lo + RWKV_LORA].set(rwkv_w2[i]), a0=row(rwkv_a0[i]),
                  a2=zeros_tail.at[lora_lo + RWKV_LORA:lora_lo + 2 * RWKV_LORA].set(rwkv_a2[i]),
                  k_k=row(rwkv_k_k[i]), k_a=row(rwkv_k_a[i]), r_k=row(rwkv_r_k[i]),
                  ln_w=row(rwkv_ln_w[i]), ln_b=row(rwkv_ln_b[i])),
    )


PROMPT_TILES = dict(lru=(1, 256), swa=(1, WINDOW), gla=(1, 256, 64), rwkv=(4, 128, 64))


def _sample_tiles(t_pad):
    return dict(lru=(16, t_pad), swa=(8, t_pad), gla=(16, t_pad, t_pad), rwkv=(16, t_pad, t_pad))


def _layer(x, pe, lp, bias, state, *, tiles, t_valid):
    b, t, _ = x.shape
    prompt = state is None
    rows = b * t
    za, zb, zc, zd, zg = _inproj(x.reshape(rows, D_MODEL), lp["norm_pre"], lp["w_in"])
    za = za.reshape(b, t, ZA_W)
    zb = zb.reshape(b, t, ZB_W)
    zc = zc.reshape(b, t, ZC_W)
    zd = zd.reshape(b, t, ZD_W)
    if prompt:
        h0 = jnp.zeros((b, 1, W_MIX), f32)
        conv0 = jnp.zeros((b, SUBLANES, W_MIX), f32)
        kbuf = vbuf = zb
        sg0 = jnp.zeros((b, W_MIX, GLA_HEADS * GLA_DK), f32)
        sr0 = jnp.zeros((b, RWKV_HEADS, RWKV_N, RWKV_N), f32)
        sh0 = jnp.zeros((b, 1, ZD_W), f32)
    else:
        h0, conv0, kbuf, vbuf, sg0, sr0, sh0 = state
        h0 = h0.reshape(b, 1, W_MIX)
        conv0 = jnp.pad(conv0, ((0, 0), (SUBLANES - (LRU_CONV - 1), 0), (0, 0)))
        kbuf = kbuf.reshape(b, WINDOW, SWA_KV_HEADS * SWA_HD)
        vbuf = vbuf.reshape(b, WINDOW, SWA_KV_HEADS * SWA_HD)
        sg0 = _gla_state_pack(sg0)
        sr0 = jnp.swapaxes(sr0, 2, 3)
        sh0 = jnp.pad(sh0, ((0, 0), (0, ZD_W - RWKV_SHIFT))).reshape(b, 1, ZD_W)
    bb, tc = tiles["lru"]
    ya, h_new = _lru(za, conv0, h0, lp["lru"], bb=bb, tc=tc, t_valid=t_valid, reset_first=prompt)
    bb, tq = tiles["swa"]
    yb = _swa(zb, kbuf, vbuf, bias, lp["swa_sinks"], lp["swa_norm"], bb=bb, tq=tq, own_prev=prompt)
    bb, tc, chunk = tiles["gla"]
    yc, sg_new = _gla(zc, zd, sg0, lp["gla"], bb=bb, tc=tc, chunk=chunk, t_valid=t_valid)
    bb, tc, chunk = tiles["rwkv"]
    yd, sr_new = _rwkv(zd, sh0, sr0, lp["rwkv"], bb=bb, tc=tc, chunk=chunk, t_valid=t_valid)
    flat = lambda a: a.reshape(rows, a.shape[-1])
    x_new = _outproj(flat(ya), flat(yb), flat(yc), flat(yd), zg, flat(x), flat(pe),
                     lp["w_out"], lp["norm_post"], lp["ple_proj"], lp["ple_gate"]).reshape(b, t, D_MODEL)
    kv_rows = slice(t_valid - WINDOW, t_valid) if prompt else slice(0, t_valid)
    kw = SWA_KV_HEADS * SWA_HD
    k_new = zb[:, kv_rows, W_MIX:W_MIX + kw].reshape(b, -1, SWA_KV_HEADS, SWA_HD)
    v_new = zb[:, kv_rows, W_MIX + kw:W_MIX + 2 * kw].reshape(b, -1, SWA_KV_HEADS, SWA_HD)
    new_state = (h_new.reshape(b, W_MIX), za[:, t_valid - (LRU_CONV - 1):t_valid], k_new, v_new,
                 _gla_state_unpack(sg_new), jnp.swapaxes(sr_new, 2, 3), zd[:, t_valid - 1, :RWKV_SHIFT])
    return x_new, new_state


def kernel(x_prompt, x_sample, state_lru_h, state_lru_conv, cache_swa_k, cache_swa_v, state_gla, state_rwkv, state_rwkv_shift, p_prompt, p_sample, rel_bias, norm_pre, norm_post, w_in, lru_conv_w, lru_conv_b, lru_wa, lru_ba, lru_wx, lru_bx, lru_lambda, lru_norm, swa_sinks, swa_norm, gla_w_up, gla_b, gla_norm, rwkv_mu, rwkv_w0, rwkv_w2, rwkv_a0, rwkv_a2, rwkv_k_k, rwkv_k_a, rwkv_r_k, rwkv_ln_w, rwkv_ln_b, w_out, ple_proj, ple_gate):
    depth = w_in.shape[0]
    t_p = x_prompt.shape[1]
    t_s = x_sample.shape[1]
    t_s_pad = -(-t_s // SUBLANES) * SUBLANES
    pad_t = lambda a, axis: jnp.pad(a, [(0, t_s_pad - t_s) if d == axis else (0, 0) for d in range(a.ndim)])
    xs = pad_t(x_sample, 1)
    ps = pad_t(p_sample, 2)
    xp = x_prompt
    bias_p = _swa_bias(rel_bias, WINDOW)
    bias_s = _swa_bias(rel_bias, t_s_pad)
    tiles_s = _sample_tiles(t_s_pad)
    sts_p, sts_s = [], []
    for i in range(depth):
        lp = _pack_layer(i, w_in, lru_conv_w, lru_conv_b, lru_wa, lru_ba, lru_wx, lru_bx, lru_lambda, lru_norm,
                         swa_sinks, swa_norm, gla_w_up, gla_b, gla_norm, rwkv_mu, rwkv_w0, rwkv_w2, rwkv_a0,
                         rwkv_a2, rwkv_k_k, rwkv_k_a, rwkv_r_k, rwkv_ln_w, rwkv_ln_b, w_out, ple_proj, ple_gate,
                         norm_pre, norm_post)
        xp, st_p = _layer(xp, p_prompt[i], lp, bias_p, None, tiles=PROMPT_TILES, t_valid=t_p)
        xs, st_s = _layer(xs, ps[i], lp, bias_s,
                          (state_lru_h[i], state_lru_conv[i], cache_swa_k[i], cache_swa_v[i],
                           state_gla[i], state_rwkv[i], state_rwkv_shift[i]),
                          tiles=tiles_s, t_valid=t_s)
        sts_p.append(st_p)
        sts_s.append(st_s)
    hp, cp, kp, vp, gp, rp, shp = [jnp.stack([s[j] for s in sts_p]) for j in range(7)]
    hs, cs, ks, vs, gs, rs, shs = [jnp.stack([s[j] for s in sts_s]) for j in range(7)]
    return (xp, xs[:, :t_s], hp, hs, cp, cs, kp, ks, vp, vs, gp, gs, rp, rs, shp, shs)
```

```python
import functools
import math

import jax
import jax.numpy as jnp
import numpy as np
from jax import lax
from jax.experimental import pallas as pl
from jax.experimental.pallas import tpu as pltpu

f32 = jnp.float32
bf16 = jnp.bfloat16

D_MODEL = 1024
D_PLE = 256
RMS_EPS = 1e-6
W_MIX = 256
LRU_BLOCKS = 4
LRU_CONV = 4
LRU_C = 8.0
SWA_HEADS = 4
SWA_KV_HEADS = 2
SWA_HD = 64
WINDOW = 128
REL_BUCKETS = 32
REL_MAX_DIST = 128
GLA_HEADS = 4
GLA_DK = 32
GLA_DV = 64
GLA_RANK = 16
GLA_TAU = 16.0
RWKV_HEADS = 4
RWKV_N = 64
RWKV_LORA = 32
RWKV_SHIFT = 3 * W_MIX + 2 * RWKV_LORA
RWKV_LN_EPS = 64e-5

ZA_W = 256
ZB_W = 512
ZC_W = 512
ZD_W = 896
ZG_W = 1024
ZD_TAIL = 768
N_IN_PACKED = ZA_W + ZB_W + ZC_W + ZD_W + ZG_W

LANES = 128
SUBLANES = 8
VMEM_LIMIT = 48 * 1024 * 1024

ROW_TILE = 512

NN = (((1,), (0,)), ((), ()))
NT = (((1,), (1,)), ((), ()))
TN = (((0,), (0,)), ((), ()))


def _dot(a, b, dn):
    return lax.dot_general(a, b, dn, preferred_element_type=f32)


def _split2(x):
    hi = x.astype(bf16)
    lo = (x - hi.astype(f32)).astype(bf16)
    return hi, lo


def _mm(a, b, dn=NN, passes=1):
    if passes == 1:
        return _dot(a.astype(bf16), b.astype(bf16), dn)
    ah, al = _split2(a)
    bh, bl = _split2(b)
    return _dot(ah, bh, dn) + (_dot(ah, bl, dn) + _dot(al, bh, dn))


def _mm_exact_rhs(a, b_bf16):
    a1 = a.astype(bf16)
    r1 = a - a1.astype(f32)
    a2 = r1.astype(bf16)
    a3 = (r1 - a2.astype(f32)).astype(bf16)
    return _dot(a1, b_bf16, NN) + (_dot(a2, b_bf16, NN) + _dot(a3, b_bf16, NN))


def _head_ones(n, width):
    r = lax.broadcasted_iota(jnp.int32, (n, n), 0) // width
    c = lax.broadcasted_iota(jnp.int32, (n, n), 1) // width
    return jnp.where(r == c, 1.0, 0.0).astype(bf16)


def _rms(x, g):
    return x * lax.rsqrt(jnp.mean(x * x, axis=-1, keepdims=True) + RMS_EPS) * g


def _seg_cumsum(x, seg):
    rows = lax.broadcasted_iota(jnp.int32, x.shape, 0) % seg
    d = 1
    while d < seg:
        x = x + jnp.where(rows >= d, pltpu.roll(x, d, 0), 0.0)
        d *= 2
    return x


def _cat(xs, axis):
    return xs[0] if len(xs) == 1 else jnp.concatenate(xs, axis=axis)


def _params(*sem):
    return pltpu.CompilerParams(dimension_semantics=sem, vmem_limit_bytes=VMEM_LIMIT)


def _inproj_body(x_ref, g_ref, w_ref, za_ref, zb_ref, zc_ref, zd_ref, zg_ref):
    h = _rms(x_ref[...], g_ref[...]).astype(bf16)
    off = 0
    for o_ref in (za_ref, zb_ref, zc_ref, zd_ref, zg_ref):
        n = o_ref.shape[-1]
        o_ref[...] = _dot(h, w_ref[:, off:off + n], NN)
        off += n


def _inproj(x, g, w):
    rows = x.shape[0]
    tm = min(ROW_TILE, rows)
    widths = (ZA_W, ZB_W, ZC_W, ZD_W, ZG_W)
    return pl.pallas_call(
        _inproj_body,
        grid=(rows // tm,),
        in_specs=[
            pl.BlockSpec((tm, D_MODEL), lambda i: (i, 0)),
            pl.BlockSpec((1, D_MODEL), lambda i: (0, 0)),
            pl.BlockSpec((D_MODEL, N_IN_PACKED), lambda i: (0, 0)),
        ],
        out_specs=[pl.BlockSpec((tm, n), lambda i: (i, 0)) for n in widths],
        out_shape=[jax.ShapeDtypeStruct((rows, n), f32) for n in widths],
        compiler_params=_params("parallel"),
        name="in_proj",
    )(x, g, w)


def _outproj_body(ya_ref, yb_ref, yc_ref, yd_ref, zg_ref, x_ref, pe_ref, wo_ref, gp_ref, wp_ref, wg_ref, o_ref):
    g = zg_ref[...]
    sg = g * jax.nn.sigmoid(g)
    acc = None
    for i, y_ref in enumerate((ya_ref, yb_ref, yc_ref, yd_ref)):
        lo = i * W_MIX
        y = (y_ref[...] * sg[:, lo:lo + W_MIX]).astype(bf16)
        part = _dot(y, wo_ref[lo:lo + W_MIX, :], NN)
        acc = part if acc is None else acc + part
    x1 = x_ref[...] + _rms(acc, gp_ref[...])
    emb = _dot(pe_ref[...].astype(bf16), wp_ref[...], NN)
    gate = jax.nn.sigmoid(_dot(x1.astype(bf16), wg_ref[...], NN))
    o_ref[...] = x1 + emb * gate


def _outproj(ya, yb, yc, yd, zg, x, pe, wo, gp, wp, wg):
    rows = x.shape[0]
    tm = min(ROW_TILE, rows)
    row = lambda n: pl.BlockSpec((tm, n), lambda i: (i, 0))
    full = lambda a: pl.BlockSpec(a.shape, lambda i: (0, 0))
    return pl.pallas_call(
        _outproj_body,
        grid=(rows // tm,),
        in_specs=[row(W_MIX), row(W_MIX), row(W_MIX), row(W_MIX), row(ZG_W), row(D_MODEL), row(D_PLE),
                  full(wo), full(gp), full(wp), full(wg)],
        out_specs=row(D_MODEL),
        out_shape=jax.ShapeDtypeStruct((rows, D_MODEL), f32),
        compiler_params=_params("parallel"),
        name="out_proj",
    )(ya, yb, yc, yd, zg, x, pe, wo, gp, wp, wg)


def _lru_body(x_ref, c0_ref, h0_ref, cw_ref, cb_ref, wa_ref, ba_ref, wx_ref, bx_ref, lam_ref, gn_ref,
              y_ref, hl_ref, ext_ref, hc_ref, *, bb, tc, t_valid, reset_first):
    j = pl.program_id(1)

    @pl.when(j == 0)
    def _():
        ext_ref[:, 0:SUBLANES, :] = c0_ref[...]
        hc_ref[...] = h0_ref[...]

    @pl.when(j > 0)
    def _():
        ext_ref[:, 0:SUBLANES, :] = ext_ref[:, tc:tc + SUBLANES, :]

    n = bb * tc
    ext_ref[:, SUBLANES:SUBLANES + tc, :] = x_ref[...]
    xc = cw_ref[3:4, :] * x_ref[...].reshape(n, W_MIX) + cb_ref[...]
    for k in range(LRU_CONV - 1):
        s = SUBLANES - (LRU_CONV - 1) + k
        xc = xc + cw_ref[k:k + 1, :] * _cat([ext_ref[i, s:s + tc, :] for i in range(bb)], 0)
    gate_r = _mm(xc, wa_ref[...]) + ba_ref[...]
    gate_i = _mm(xc, wx_ref[...]) + bx_ref[...]
    log_a = -LRU_C * jax.nn.sigmoid(gate_r) * jax.nn.softplus(-lam_ref[...])
    a = jnp.exp(log_a)
    mult = jnp.sqrt(1.0 - a * a)
    if reset_first:
        rows = lax.broadcasted_iota(jnp.int32, (n, W_MIX), 0)
        mult = jnp.where(rows % tc + j * tc == 0, 1.0, mult)
    b = mult * jax.nn.sigmoid(gate_i) * xc
    ng = tc // SUBLANES
    a = a.reshape(bb * ng, SUBLANES, W_MIX)
    b = b.reshape(bb * ng, SUBLANES, W_MIX)
    sub = lax.broadcasted_iota(jnp.int32, (1, SUBLANES, W_MIX), 1)
    d = 1
    while d < SUBLANES:
        m = sub >= d
        b = a * jnp.where(m, pltpu.roll(b, d, 1), 0.0) + b
        a = a * jnp.where(m, pltpu.roll(a, d, 1), 1.0)
        d *= 2
    hs = []
    for i in range(bb):
        carry = jnp.broadcast_to(hc_ref[i], (SUBLANES, W_MIX))
        for g in range(ng):
            hg = a[i * ng + g] * carry + b[i * ng + g]
            hs.append(hg)
            carry = jnp.broadcast_to(hg[SUBLANES - 1:SUBLANES], (SUBLANES, W_MIX))
        hc_ref[i] = carry[0:1]
    h = _cat(hs, 0)
    y_ref[...] = _rms(h, gn_ref[...]).reshape(bb, tc, W_MIX)
    r_last = (t_valid - 1) % tc

    @pl.when(j == (t_valid - 1) // tc)
    def _():
        for i in range(bb):
            hl_ref[i] = h[i * tc + r_last:i * tc + r_last + 1, :]


def _lru(za, conv0, h0, p, *, bb, tc, t_valid, reset_first):
    b, t, _ = za.shape
    vec = pl.BlockSpec((1, W_MIX), lambda i, j: (0, 0))
    mat = pl.BlockSpec((W_MIX, W_MIX), lambda i, j: (0, 0))
    return pl.pallas_call(
        functools.partial(_lru_body, bb=bb, tc=tc, t_valid=t_valid, reset_first=reset_first),
        grid=(b // bb, t // tc),
        in_specs=[
            pl.BlockSpec((bb, tc, W_MIX), lambda i, j: (i, j, 0)),
            pl.BlockSpec((bb, SUBLANES, W_MIX), lambda i, j: (i, 0, 0)),
            pl.BlockSpec((bb, 1, W_MIX), lambda i, j: (i, 0, 0)),
            pl.BlockSpec((LRU_CONV, W_MIX), lambda i, j: (0, 0)),
            vec, mat, vec, mat, vec, vec, vec,
        ],
        out_specs=[
            pl.BlockSpec((bb, tc, W_MIX), lambda i, j: (i, j, 0)),
            pl.BlockSpec((bb, 1, W_MIX), lambda i, j: (i, 0, 0)),
        ],
        out_shape=[jax.ShapeDtypeStruct((b, t, W_MIX), f32), jax.ShapeDtypeStruct((b, 1, W_MIX), f32)],
        scratch_shapes=[pltpu.VMEM((bb, tc + SUBLANES, W_MIX), f32), pltpu.VMEM((bb, 1, W_MIX), f32)],
        compiler_params=_params("parallel", "arbitrary"),
        name="rglru",
    )(za, conv0, h0, p["conv_w"], p["conv_b"], p["wa"], p["ba"], p["wx"], p["bx"], p["lam"], p["norm"])


def _swa_body(q_ref, ko_ref, vo_ref, kp_ref, vp_ref, bias_ref, sink_ref, gn_ref, y_ref, *, bb, tq, mask_first):
    n = pl.program_id(1)
    nk = WINDOW + tq
    kw = SWA_KV_HEADS * SWA_HD
    lo = lax.broadcasted_iota(jnp.int32, (1, kw), 1) < SWA_HD
    row_head = lax.broadcasted_iota(jnp.int32, (SWA_HEADS * tq, 1), 0) // tq
    sink = jnp.zeros((SWA_HEADS * tq, 1), f32)
    for h in range(SWA_HEADS):
        sink = jnp.where(row_head == h, sink_ref[:, h:h + 1], sink)
    bias = bias_ref[...]
    if mask_first:
        col = lax.broadcasted_iota(jnp.int32, (1, nk), 1)
        bias = jnp.where((n > 0) | (col >= WINDOW), bias, -jnp.inf)
    swap = lambda x: pltpu.roll(x, SWA_HD, 1)
    qs = []
    for i in range(bb):
        q = q_ref[i] * (SWA_HD ** -0.5)
        q01, q23 = q[:, 0:kw], q[:, kw:2 * kw]
        qs.append(jnp.concatenate([jnp.where(lo, q01, 0.0), jnp.where(lo, swap(q01), 0.0),
                                   jnp.where(lo, 0.0, swap(q23)), jnp.where(lo, 0.0, q23)], axis=0))
    s = [_mm(qs[i], jnp.concatenate([kp_ref[i], ko_ref[i]], axis=0), NT) + bias for i in range(bb)]
    m = [jnp.maximum(jnp.max(x, axis=-1, keepdims=True), sink) for x in s]
    e = [jnp.exp(x - mx) for x, mx in zip(s, m)]
    den = [jnp.sum(x, axis=-1, keepdims=True) + jnp.exp(sink - mx) for x, mx in zip(e, m)]
    pv = [_mm(e[i], jnp.concatenate([vp_ref[i], vo_ref[i]], axis=0)) / den[i] for i in range(bb)]
    for i in range(bb):
        p = pv[i]
        o01 = jnp.where(lo, p[0:tq], swap(p[tq:2 * tq]))
        o23 = jnp.where(lo, swap(p[2 * tq:3 * tq]), p[3 * tq:4 * tq])
        y_ref[i] = _rms(jnp.concatenate([o01, o23], axis=1), gn_ref[...])


def _swa(zb, k_prev, v_prev, bias, sinks, gn, *, bb, tq, own_prev):
    b, t, _ = zb.shape
    kw = SWA_KV_HEADS * SWA_HD
    if own_prev:
        kp_spec = pl.BlockSpec((bb, WINDOW, kw), lambda i, n: (i, jnp.maximum(n - 1, 0), 2))
        vp_spec = pl.BlockSpec((bb, WINDOW, kw), lambda i, n: (i, jnp.maximum(n - 1, 0), 3))
    else:
        kp_spec = pl.BlockSpec((bb, WINDOW, kw), lambda i, n: (i, 0, 0))
        vp_spec = pl.BlockSpec((bb, WINDOW, kw), lambda i, n: (i, 0, 0))
    return pl.pallas_call(
        functools.partial(_swa_body, bb=bb, tq=tq, mask_first=own_prev),
        grid=(b // bb, t // tq),
        in_specs=[
            pl.BlockSpec((bb, tq, W_MIX), lambda i, n: (i, n, 0)),
            pl.BlockSpec((bb, tq, kw), lambda i, n: (i, n, 2)),
            pl.BlockSpec((bb, tq, kw), lambda i, n: (i, n, 3)),
            kp_spec, vp_spec,
            pl.BlockSpec(bias.shape, lambda i, n: (0, 0)),
            pl.BlockSpec((1, SWA_HEADS), lambda i, n: (0, 0)),
            pl.BlockSpec((1, W_MIX), lambda i, n: (0, 0)),
        ],
        out_specs=pl.BlockSpec((bb, tq, W_MIX), lambda i, n: (i, n, 0)),
        out_shape=jax.ShapeDtypeStruct((b, t, W_MIX), f32),
        compiler_params=_params("parallel", "arbitrary"),
        name="swa",
    )(zb, zb, zb, k_prev, v_prev, bias, sinks, gn)


def _gla_body(q_ref, k_ref, v_ref, t_ref, s0_ref, wup_ref, gb_ref, gn_ref, y_ref, sn_ref, st_ref,
              *, bb, tc, chunk, t_valid):
    j = pl.program_id(1)

    n = bb * tc
    qk = GLA_HEADS * GLA_DK
    sr = lax.broadcasted_iota(jnp.int32, (W_MIX, qk), 0) // GLA_DV
    sc = lax.broadcasted_iota(jnp.int32, (W_MIX, qk), 1) // GLA_DK
    diag = sr == sc

    @pl.when(j == 0)
    def _():
        for i in range(bb):
            st_ref[i] = jnp.where(diag, jnp.concatenate([s0_ref[i]] * GLA_HEADS, axis=1), 0.0)

    rows = lax.broadcasted_iota(jnp.int32, (n, 1), 0) % tc
    live = rows + j * tc < t_valid
    la = jax.nn.log_sigmoid(_mm(t_ref[...].reshape(n, LANES), wup_ref[...], passes=3) + gb_ref[...]) / GLA_TAU
    la = jnp.where(live, la, 0.0)
    k = jnp.where(live, k_ref[...].reshape(n, qk), 0.0)
    v = jnp.where(live, v_ref[...].reshape(n, W_MIX), 0.0)
    bc = _seg_cumsum(la, chunk)
    q_dec = q_ref[...].reshape(n, qk) * (GLA_DK ** -0.5) * jnp.exp(bc)
    k_inv = k * jnp.exp(-bc)
    khead = lax.broadcasted_iota(jnp.int32, (1, qk), 1) // GLA_DK
    vhead = lax.broadcasted_iota(jnp.int32, (1, W_MIX), 1) // GLA_DV
    r2 = lax.broadcasted_iota(jnp.int32, (GLA_HEADS * tc, tc), 0) % tc
    c2 = lax.broadcasted_iota(jnp.int32, (GLA_HEADS * tc, tc), 1)
    causal = (r2 // chunk == c2 // chunk) & (r2 >= c2)
    nch = tc // chunk
    ts = [slice(i * tc, (i + 1) * tc) for i in range(bb)]
    qs = [jnp.concatenate([jnp.where(khead == h, q_dec[t], 0.0) for h in range(GLA_HEADS)], axis=0) for t in ts]
    att = [jnp.where(causal, _mm(qs[i], k_inv[ts[i]], NT), 0.0) for i in range(bb)]
    full = [_mm(att[i], v[ts[i]]) for i in range(bb)]
    kv = []
    for c in range(nch):
        kv_c = []
        for i in range(bb):
            rs = slice(i * tc + c * chunk, i * tc + (c + 1) * chunk)
            b_end = bc[i * tc + (c + 1) * chunk - 1:i * tc + (c + 1) * chunk, :]
            k_end = k[rs] * jnp.exp(b_end - bc[rs])
            kv_c.append((jnp.exp(b_end), jnp.where(diag, _mm(v[rs], k_end, TN), 0.0)))
        kv.append(kv_c)
    st = [st_ref[i] for i in range(bb)]
    inter = [[] for _ in range(bb)]
    for c in range(nch):
        for i in range(bb):
            inter[i].append(_mm(q_dec[i * tc + c * chunk:i * tc + (c + 1) * chunk], st[i], NT))
            st[i] = st[i] * kv[c][i][0] + kv[c][i][1]
    outs = []
    for i in range(bb):
        st_ref[i] = st[i]
        o = jnp.where(vhead == 0, full[i][0:tc], 0.0)
        for h in range(1, GLA_HEADS):
            o = o + jnp.where(vhead == h, full[i][h * tc:(h + 1) * tc], 0.0)
        outs.append(o + _cat(inter[i], 0))
    o = _cat(outs, 0)
    ms = _mm_exact_rhs(o * o, _head_ones(W_MIX, GLA_DV)) / GLA_DV
    y_ref[...] = (o * lax.rsqrt(ms + RMS_EPS) * gn_ref[...]).reshape(bb, tc, W_MIX)

    @pl.when(j == pl.num_programs(1) - 1)
    def _():
        for i in range(bb):
            x = st_ref[i]
            sn_ref[i] = sum(x[:, h * GLA_DK:(h + 1) * GLA_DK] for h in range(1, GLA_HEADS)) + x[:, 0:GLA_DK]


def _gla(zc, zd, s0, p, *, bb, tc, chunk, t_valid):
    b, t, _ = zc.shape
    qk = GLA_HEADS * GLA_DK
    state = pl.BlockSpec((bb, W_MIX, GLA_DK), lambda i, j: (i, 0, 0))
    return pl.pallas_call(
        functools.partial(_gla_body, bb=bb, tc=tc, chunk=chunk, t_valid=t_valid),
        grid=(b // bb, t // tc),
        in_specs=[
            pl.BlockSpec((bb, tc, qk), lambda i, j: (i, j, 0)),
            pl.BlockSpec((bb, tc, qk), lambda i, j: (i, j, 1)),
            pl.BlockSpec((bb, tc, W_MIX), lambda i, j: (i, j, 1)),
            pl.BlockSpec((bb, tc, LANES), lambda i, j: (i, j, ZD_TAIL // LANES)),
            state,
            pl.BlockSpec((LANES, qk), lambda i, j: (0, 0)),
            pl.BlockSpec((1, qk), lambda i, j: (0, 0)),
            pl.BlockSpec((1, W_MIX), lambda i, j: (0, 0)),
        ],
        out_specs=[pl.BlockSpec((bb, tc, W_MIX), lambda i, j: (i, j, 0)), state],
        out_shape=[jax.ShapeDtypeStruct((b, t, W_MIX), f32), jax.ShapeDtypeStruct((b, W_MIX, GLA_DK), f32)],
        scratch_shapes=[pltpu.VMEM((bb, W_MIX, qk), f32)],
        compiler_params=_params("parallel", "arbitrary"),
        name="gla",
    )(zc, zc, zc, zd, s0, p["w_up"], p["b"], p["norm"])


def _gla_state_pack(s):
    return jnp.swapaxes(s, 2, 3).reshape(s.shape[0], W_MIX, GLA_DK)


def _gla_state_unpack(s):
    return jnp.swapaxes(s.reshape(s.shape[0], GLA_HEADS, GLA_DV, GLA_DK), 2, 3)


def _rwkv_body(u_ref, sh0_ref, s0_ref, mu_ref, w0_ref, w2_ref, a0_ref, a2_ref, kk_ref, ka_ref, rk_ref,
               lw_ref, lb_ref, y_ref, sn_ref, st_ref, up_ref, *, bb, tc, chunk, t_valid):
    j = pl.program_id(1)

    @pl.when(j == 0)
    def _():
        st_ref[...] = s0_ref[...]
        up_ref[...] = sh0_ref[...]

    n = bb * tc
    u = u_ref[...].reshape(n, ZD_W)
    rows = lax.broadcasted_iota(jnp.int32, (n, 1), 0)
    u_prev = pltpu.roll(u, 1, 0)
    for i in range(bb):
        u_prev = jnp.where(rows == i * tc, up_ref[i], u_prev)
        up_ref[i] = u[(i + 1) * tc - 1:(i + 1) * tc, :]
    us = u + (u_prev - u) * mu_ref[...]
    r = us[:, 0:W_MIX]
    k = us[:, W_MIX:2 * W_MIX]
    v = us[:, 2 * W_MIX:3 * W_MIX]
    tail = us[:, ZD_TAIL:ZD_TAIL + LANES]
    w = -jax.nn.softplus(-(w0_ref[...] + _mm(jnp.tanh(tail), w2_ref[...], passes=3))) - 0.5
    lw = -jnp.exp(w)
    a = jax.nn.sigmoid(a0_ref[...] + _mm(tail, a2_ref[...], passes=3))
    ones = _head_ones(W_MIX, RWKV_N)
    kk = k * kk_ref[...]
    kk = kk / jnp.maximum(jnp.sqrt(_mm_exact_rhs(kk * kk, ones)), 1e-12)
    k = k * (1.0 + (a - 1.0) * ka_ref[...])
    live = rows % tc + j * tc < t_valid
    lw = jnp.where(live, lw, 0.0)
    aa = jnp.where(live, -kk, 0.0)
    bl = jnp.where(live, kk * a, 0.0)
    kl = jnp.where(live, k, 0.0)
    vl = jnp.where(live, v, 0.0)

    cum = _seg_cumsum(lw, chunk)
    gi = jnp.exp(-cum)
    a_t = aa * jnp.exp(cum - lw)
    b_t = bl * gi
    k_t = kl * gi
    r_t = r * jnp.exp(cum)
    nch = tc // chunk
    r2 = lax.broadcasted_iota(jnp.int32, (tc, tc), 0)
    c2 = lax.broadcasted_iota(jnp.int32, (tc, tc), 1)
    same = r2 // chunk == c2 // chunk
    strict = same & (r2 > c2)
    incl = same & (r2 >= c2)

    chains = [(i, h) for i in range(bb) for h in range(RWKV_HEADS)]
    nq = len(chains)
    sl = lambda x, i, h: x[i * tc:(i + 1) * tc, h * RWKV_N:(h + 1) * RWKV_N]
    ah = [sl(a_t, i, h) for i, h in chains]
    rh = [sl(r_t, i, h) for i, h in chains]
    vh = [sl(vl, i, h) for i, h in chains]
    pair = [_mm(jnp.concatenate([ah[q], rh[q]], axis=0),
                jnp.concatenate([sl(b_t, i, h), sl(k_t, i, h)], axis=0), NT) for q, (i, h) in enumerate(chains)]
    l_ab = [jnp.where(strict, m[0:tc, 0:tc], 0.0) for m in pair]
    l_ak = [jnp.where(strict, m[0:tc, tc:2 * tc], 0.0) for m in pair]
    m_rb = [jnp.where(incl, m[tc:2 * tc, 0:tc], 0.0) for m in pair]
    m_rk = [jnp.where(incl, m[tc:2 * tc, tc:2 * tc], 0.0) for m in pair]
    sol = [jnp.concatenate([ah[q], _mm(l_ak[q], vh[q])], axis=1) for q in range(nq)]
    y_v = [_mm(m_rk[q], vh[q]) for q in range(nq)]
    pw = l_ab
    d = 1
    while True:
        sol = [s + _mm(m, s) for s, m in zip(sol, pw)]
        d *= 2
        if d >= chunk:
            break
        pw = [_mm(m, m) for m in pw]
    lhs_c, p2_c, mrb_c, yv_c, bdt_c, kv_c, gcol_c = [], [], [], [], [], [], []
    for c in range(nch):
        rs = slice(c * chunk, (c + 1) * chunk)
        ce = [sl(cum, i, h)[(c + 1) * chunk - 1:(c + 1) * chunk, :] for i, h in chains]
        dec = [jnp.exp(ce[q] - sl(cum, i, h)[rs]) for q, (i, h) in enumerate(chains)]
        lhs_c.append([jnp.concatenate([sol[q][rs, 0:RWKV_N], rh[q][rs]], axis=0) for q in range(nq)])
        p2_c.append([sol[q][rs, RWKV_N:2 * RWKV_N] for q in range(nq)])
        mrb_c.append([m_rb[q][rs, rs] for q in range(nq)])
        yv_c.append([y_v[q][rs] for q in range(nq)])
        bdt_c.append([(sl(bl, i, h)[rs] * dec[q]).T for q, (i, h) in enumerate(chains)])
        kv_c.append([_mm(sl(kl, i, h)[rs] * dec[q], vh[q][rs], TN) for q, (i, h) in enumerate(chains)])
        gcol_c.append([jnp.broadcast_to(jnp.exp(ce[q]), (RWKV_N, RWKV_N)).T for q in range(nq)])
    st = [st_ref[i, h] for i, h in chains]
    y_parts = [[] for _ in chains]
    for c in range(nch):
        both = [_mm(lhs_c[c][q], st[q]) for q in range(nq)]
        u_c = [both[q][0:chunk] + p2_c[c][q] for q in range(nq)]
        st = [gcol_c[c][q] * st[q] + kv_c[c][q] + _mm(bdt_c[c][q], u_c[q]) for q in range(nq)]
        for q in range(nq):
            y_parts[q].append(both[q][chunk:2 * chunk] + _mm(mrb_c[c][q], u_c[q]) + yv_c[c][q])
    for q, (i, h) in enumerate(chains):
        st_ref[i, h] = st[q]
    y = _cat([_cat([_cat(y_parts[i * RWKV_HEADS + h], 0) for h in range(RWKV_HEADS)], 1) for i in range(bb)], 0)
    mean = _mm_exact_rhs(y, ones) / RWKV_N
    yc = y - mean
    var = _mm_exact_rhs(yc * yc, ones) / RWKV_N
    out = yc * lax.rsqrt(var + RWKV_LN_EPS) * lw_ref[...] + lb_ref[...]
    bonus = _mm_exact_rhs(r * k * rk_ref[...], ones)
    y_ref[...] = (out + bonus * v).reshape(bb, tc, W_MIX)

    @pl.when(j == pl.num_programs(1) - 1)
    def _():
        sn_ref[...] = st_ref[...]


def _rwkv(zd, shift0, s0, p, *, bb, tc, chunk, t_valid):
    b, t, _ = zd.shape
    vec = lambda n: pl.BlockSpec((1, n), lambda i, j: (0, 0))
    lora = pl.BlockSpec((LANES, W_MIX), lambda i, j: (0, 0))
    state = pl.BlockSpec((bb, RWKV_HEADS, RWKV_N, RWKV_N), lambda i, j: (i, 0, 0, 0))
    return pl.pallas_call(
        functools.partial(_rwkv_body, bb=bb, tc=tc, chunk=chunk, t_valid=t_valid),
        grid=(b // bb, t // tc),
        in_specs=[
            pl.BlockSpec((bb, tc, ZD_W), lambda i, j: (i, j, 0)),
            pl.BlockSpec((bb, 1, ZD_W), lambda i, j: (i, 0, 0)),
            state,
            vec(ZD_W), vec(W_MIX), lora, vec(W_MIX), lora, vec(W_MIX), vec(W_MIX), vec(W_MIX),
            vec(W_MIX), vec(W_MIX),
        ],
        out_specs=[pl.BlockSpec((bb, tc, W_MIX), lambda i, j: (i, j, 0)), state],
        out_shape=[jax.ShapeDtypeStruct((b, t, W_MIX), f32),
                   jax.ShapeDtypeStruct((b, RWKV_HEADS, RWKV_N, RWKV_N), f32)],
        scratch_shapes=[pltpu.VMEM((bb, RWKV_HEADS, RWKV_N, RWKV_N), f32), pltpu.VMEM((bb, 1, ZD_W), f32)],
        compiler_params=_params("parallel", "arbitrary"),
        name="rwkv7",
    )(zd, shift0, s0, p["mu"], p["w0"], p["w2"], p["a0"], p["a2"], p["k_k"], p["k_a"], p["r_k"],
      p["ln_w"], p["ln_b"])


def _block_diag(w):
    n, c, _ = w.shape
    out = jnp.zeros((n * c, n * c), w.dtype)
    for i in range(n):
        out = out.at[i * c:(i + 1) * c, i * c:(i + 1) * c].set(w[i])
    return out


def _rel_bucket_ids(tq):
    dist = WINDOW + np.arange(tq)[:, None] - np.arange(WINDOW + tq)[None, :]
    max_exact = REL_BUCKETS // 2
    n = np.maximum(dist, 0)
    log_ratio = (np.log(np.maximum(n, 1).astype(np.float32) / np.float32(max_exact))
                 / np.float32(math.log(REL_MAX_DIST / max_exact)))
    large = np.minimum(max_exact + (log_ratio * np.float32(REL_BUCKETS - max_exact)).astype(np.int32),
                       REL_BUCKETS - 1)
    bucket = np.where(n < max_exact, n, large)
    return np.where((dist >= 0) & (dist <= WINDOW), bucket, -1).astype(np.int32)


def _swa_bias(table, tq):
    ids = jnp.asarray(_rel_bucket_ids(tq))[None]
    out = jnp.full((SWA_HEADS, tq, WINDOW + tq), -jnp.inf, f32)
    for k in range(REL_BUCKETS):
        out = jnp.where(ids == k, table[k].astype(f32)[:, None, None], out)
    return out.reshape(SWA_HEADS * tq, WINDOW + tq)


def _pack_layer(i, w_in, lru_conv_w, lru_conv_b, lru_wa, lru_ba, lru_wx, lru_bx, lru_lambda, lru_norm,
                swa_sinks, swa_norm, gla_w_up, gla_b, gla_norm, rwkv_mu, rwkv_w0, rwkv_w2, rwkv_a0, rwkv_a2,
                rwkv_k_k, rwkv_k_a, rwkv_r_k, rwkv_ln_w, rwkv_ln_b, w_out, ple_proj, ple_gate,
                norm_pre, norm_post):
    w = w_in[i]
    c_ca = 5 * W_MIX
    c_ud = c_ca + GLA_RANK
    c_gate = c_ud + RWKV_SHIFT
    pad = jnp.zeros((D_MODEL, ZD_W - RWKV_SHIFT - GLA_RANK), w.dtype)
    w_packed = jnp.concatenate([w[:, :c_ca], w[:, c_ud:c_gate], w[:, c_ca:c_ud], pad, w[:, c_gate:]], axis=1)
    row = lambda a: a.reshape(1, -1)
    lora_lo = RWKV_SHIFT - 2 * RWKV_LORA - ZD_TAIL
    ca_lo = RWKV_SHIFT - ZD_TAIL
    zeros_tail = jnp.zeros((LANES, W_MIX), f32)
    return dict(
        w_in=w_packed.astype(bf16),
        norm_pre=row(norm_pre[i]), norm_post=row(norm_post[i]),
        w_out=w_out[i].astype(bf16), ple_proj=ple_proj[i].astype(bf16), ple_gate=ple_gate[i].astype(bf16),
        lru=dict(conv_w=lru_conv_w[i], conv_b=row(lru_conv_b[i]), wa=_block_diag(lru_wa[i]), ba=row(lru_ba[i]),
                 wx=_block_diag(lru_wx[i]), bx=row(lru_bx[i]), lam=row(lru_lambda[i]), norm=row(lru_norm[i])),
        swa_sinks=row(swa_sinks[i]), swa_norm=row(swa_norm[i]),
        gla=dict(w_up=jnp.zeros((LANES, GLA_HEADS * GLA_DK), f32).at[ca_lo:ca_lo + GLA_RANK].set(gla_w_up[i]),
                 b=row(gla_b[i]), norm=row(jnp.tile(gla_norm[i], GLA_HEADS))),
        rwkv=dict(mu=row(jnp.pad(rwkv_mu[i], (0, ZD_W - RWKV_SHIFT))), w0=row(rwkv_w0[i]),
                  w2=zeros_tail.at[lora_lo:lora_lo + RWKV_LORA].set(rwkv_w2[i]), a0=row(rwkv_a0[i]),
                  a2=zeros_tail.at[lora_lo + RWKV_LORA:lora_lo + 2 * RWKV_LORA].set(rwkv_a2[i]),
                  k_k=row(rwkv_k_k[i]), k_a=row(rwkv_k_a[i]), r_k=row(rwkv_r_k[i]),
                  ln_w=row(rwkv_ln_w[i]), ln_b=row(rwkv_ln_b[i])),
    )


PROMPT_TILES = dict(lru=(1, 256), swa=(4, WINDOW), gla=(4, 128, 64), rwkv=(4, 128, 64))


def _sample_tiles(t_pad):
    return dict(lru=(16, t_pad), swa=(8, t_pad), gla=(16, t_pad, t_pad), rwkv=(16, t_pad, t_pad))


def _layer(x, pe, lp, bias, state, *, tiles, t_valid):
    b, t, _ = x.shape
    prompt = state is None
    rows = b * t
    za, zb, zc, zd, zg = _inproj(x.reshape(rows, D_MODEL), lp["norm_pre"], lp["w_in"])
    za = za.reshape(b, t, ZA_W)
    zb = zb.reshape(b, t, ZB_W)
    zc = zc.reshape(b, t, ZC_W)
    zd = zd.reshape(b, t, ZD_W)
    if prompt:
        h0 = jnp.zeros((b, 1, W_MIX), f32)
        conv0 = jnp.zeros((b, SUBLANES, W_MIX), f32)
        kbuf = vbuf = zb
        sg0 = jnp.zeros((b, W_MIX, GLA_DK), f32)
        sr0 = jnp.zeros((b, RWKV_HEADS, RWKV_N, RWKV_N), f32)
        sh0 = jnp.zeros((b, 1, ZD_W), f32)
    else:
        h0, conv0, kbuf, vbuf, sg0, sr0, sh0 = state
        h0 = h0.reshape(b, 1, W_MIX)
        conv0 = jnp.pad(conv0, ((0, 0), (SUBLANES - (LRU_CONV - 1), 0), (0, 0)))
        kbuf = kbuf.reshape(b, WINDOW, SWA_KV_HEADS * SWA_HD)
        vbuf = vbuf.reshape(b, WINDOW, SWA_KV_HEADS * SWA_HD)
        sg0 = _gla_state_pack(sg0)
        sr0 = jnp.swapaxes(sr0, 2, 3)
        sh0 = jnp.pad(sh0, ((0, 0), (0, ZD_W - RWKV_SHIFT))).reshape(b, 1, ZD_W)
    bb, tc = tiles["lru"]
    ya, h_new = _lru(za, conv0, h0, lp["lru"], bb=bb, tc=tc, t_valid=t_valid, reset_first=prompt)
    bb, tq = tiles["swa"]
    yb = _swa(zb, kbuf, vbuf, bias, lp["swa_sinks"], lp["swa_norm"], bb=bb, tq=tq, own_prev=prompt)
    bb, tc, chunk = tiles["gla"]
    yc, sg_new = _gla(zc, zd, sg0, lp["gla"], bb=bb, tc=tc, chunk=chunk, t_valid=t_valid)
    bb, tc, chunk = tiles["rwkv"]
    yd, sr_new = _rwkv(zd, sh0, sr0, lp["rwkv"], bb=bb, tc=tc, chunk=chunk, t_valid=t_valid)
    flat = lambda a: a.reshape(rows, a.shape[-1])
    x_new = _outproj(flat(ya), flat(yb), flat(yc), flat(yd), zg, flat(x), flat(pe),
                     lp["w_out"], lp["norm_post"], lp["ple_proj"], lp["ple_gate"]).reshape(b, t, D_MODEL)
    kv_rows = slice(t_valid - WINDOW, t_valid) if prompt else slice(0, t_valid)
    kw = SWA_KV_HEADS * SWA_HD
    k_new = zb[:, kv_rows, W_MIX:W_MIX + kw].reshape(b, -1, SWA_KV_HEADS, SWA_HD)
    v_new = zb[:, kv_rows, W_MIX + kw:W_MIX + 2 * kw].reshape(b, -1, SWA_KV_HEADS, SWA_HD)
    new_state = (h_new.reshape(b, W_MIX), za[:, t_valid - (LRU_CONV - 1):t_valid], k_new, v_new,
                 _gla_state_unpack(sg_new), jnp.swapaxes(sr_new, 2, 3), zd[:, t_valid - 1, :RWKV_SHIFT])
    return x_new, new_state


def kernel(x_prompt, x_sample, state_lru_h, state_lru_conv, cache_swa_k, cache_swa_v, state_gla, state_rwkv, state_rwkv_shift, p_prompt, p_sample, rel_bias, norm_pre, norm_post, w_in, lru_conv_w, lru_conv_b, lru_wa, lru_ba, lru_wx, lru_bx, lru_lambda, lru_norm, swa_sinks, swa_norm, gla_w_up, gla_b, gla_norm, rwkv_mu, rwkv_w0, rwkv_w2, rwkv_a0, rwkv_a2, rwkv_k_k, rwkv_k_a, rwkv_r_k, rwkv_ln_w, rwkv_ln_b, w_out, ple_proj, ple_gate):
    depth = w_in.shape[0]
    t_p = x_prompt.shape[1]
    t_s = x_sample.shape[1]
    t_s_pad = -(-t_s // SUBLANES) * SUBLANES
    pad_t = lambda a, axis: jnp.pad(a, [(0, t_s_pad - t_s) if d == axis else (0, 0) for d in range(a.ndim)])
    xs = pad_t(x_sample, 1)
    ps = pad_t(p_sample, 2)
    xp = x_prompt
    bias_p = _swa_bias(rel_bias, WINDOW)
    bias_s = _swa_bias(rel_bias, t_s_pad)
    tiles_s = _sample_tiles(t_s_pad)
    sts_p, sts_s = [], []
    for i in range(depth):
        lp = _pack_layer(i, w_in, lru_conv_w, lru_conv_b, lru_wa, lru_ba, lru_wx, lru_bx, lru_lambda, lru_norm,
                         swa_sinks, swa_norm, gla_w_up, gla_b, gla_norm, rwkv_mu, rwkv_w0, rwkv_w2, rwkv_a0,
                         rwkv_a2, rwkv_k_k, rwkv_k_a, rwkv_r_k, rwkv_ln_w, rwkv_ln_b, w_out, ple_proj, ple_gate,
                         norm_pre, norm_post)
        xp, st_p = _layer(xp, p_prompt[i], lp, bias_p, None, tiles=PROMPT_TILES, t_valid=t_p)
        xs, st_s = _layer(xs, ps[i], lp, bias_s,
                          (state_lru_h[i], state_lru_conv[i], cache_swa_k[i], cache_swa_v[i],
                           state_gla[i], state_rwkv[i], state_rwkv_shift[i]),
                          tiles=tiles_s, t_valid=t_s)
        sts_p.append(st_p)
        sts_s.append(st_s)
    hp, cp, kp, vp, gp, rp, shp = [jnp.stack([s[j] for s in sts_p]) for j in range(7)]
    hs, cs, ks, vs, gs, rs, shs = [jnp.stack([s[j] for s in sts_s]) for j in range(7)]
    return (xp, xs[:, :t_s], hp, hs, cp, cs, kp, ks, vp, vs, gp, gs, rp, rs, shp, shs)
```

```python
import functools
import math

import jax
import jax.numpy as jnp
import numpy as np
from jax import lax
from jax.experimental import pallas as pl
from jax.experimental.pallas import tpu as pltpu

f32 = jnp.float32
bf16 = jnp.bfloat16

D_MODEL = 1024
D_PLE = 256
RMS_EPS = 1e-6
W_MIX = 256
LRU_BLOCKS = 4
LRU_CONV = 4
LRU_C = 8.0
SWA_HEADS = 4
SWA_KV_HEADS = 2
SWA_HD = 64
WINDOW = 128
REL_BUCKETS = 32
REL_MAX_DIST = 128
GLA_HEADS = 4
GLA_DK = 32
GLA_DV = 64
GLA_RANK = 16
GLA_TAU = 16.0
RWKV_HEADS = 4
RWKV_N = 64
RWKV_LORA = 32
RWKV_SHIFT = 3 * W_MIX + 2 * RWKV_LORA
RWKV_LN_EPS = 64e-5

ZA_W = 256
ZB_W = 512
ZC_W = 512
ZD_W = 896
ZG_W = 1024
ZD_TAIL = 768
N_IN_PACKED = ZA_W + ZB_W + ZC_W + ZD_W + ZG_W

LANES = 128
SUBLANES = 8
VMEM_LIMIT = 48 * 1024 * 1024

ROW_TILE = 512

NN = (((1,), (0,)), ((), ()))
NT = (((1,), (1,)), ((), ()))
TN = (((0,), (0,)), ((), ()))


def _dot(a, b, dn):
    return lax.dot_general(a, b, dn, preferred_element_type=f32)


def _split2(x):
    hi = x.astype(bf16)
    lo = (x - hi.astype(f32)).astype(bf16)
    return hi, lo


def _mm(a, b, dn=NN, passes=1):
    if passes == 1:
        return _dot(a.astype(bf16), b.astype(bf16), dn)
    ah, al = _split2(a)
    bh, bl = _split2(b)
    return _dot(ah, bh, dn) + (_dot(ah, bl, dn) + _dot(al, bh, dn))


def _mm_exact_rhs(a, b_bf16):
    hi, lo = _split2(a)
    return _dot(hi, b_bf16, NN) + _dot(lo, b_bf16, NN)


def _head_ones(n, width):
    r = lax.broadcasted_iota(jnp.int32, (n, n), 0) // width
    c = lax.broadcasted_iota(jnp.int32, (n, n), 1) // width
    return jnp.where(r == c, 1.0, 0.0).astype(bf16)


def _rms(x, g):
    return x * lax.rsqrt(jnp.mean(x * x, axis=-1, keepdims=True) + RMS_EPS) * g


def _seg_cumsum(x, seg):
    n, w = x.shape
    x3 = x.reshape(n // SUBLANES, SUBLANES, w)
    sub = lax.broadcasted_iota(jnp.int32, (1, SUBLANES, w), 1)
    d = 1
    while d < SUBLANES:
        x3 = x3 + jnp.where(sub >= d, pltpu.roll(x3, d, 1), 0.0)
        d *= 2
    x = x3.reshape(n, w)
    rows = lax.broadcasted_iota(jnp.int32, (n, 1), 0) % seg
    tot = jnp.broadcast_to(x3[:, SUBLANES - 1:SUBLANES, :], x3.shape).reshape(n, w)
    while d < seg:
        add = jnp.where(rows >= d, pltpu.roll(tot, d, 0), 0.0)
        x = x + add
        tot = tot + add
        d *= 2
    return x


def _cat(xs, axis):
    return xs[0] if len(xs) == 1 else jnp.concatenate(xs, axis=axis)


def _params(*sem):
    return pltpu.CompilerParams(dimension_semantics=sem, vmem_limit_bytes=VMEM_LIMIT)


def _inproj_body(x_ref, g_ref, w_ref, za_ref, zb_ref, zc_ref, zd_ref, zg_ref):
    h = _rms(x_ref[...], g_ref[...]).astype(bf16)
    off = 0
    for o_ref in (za_ref, zb_ref, zc_ref, zd_ref, zg_ref):
        n = o_ref.shape[-1]
        o_ref[...] = _dot(h, w_ref[:, off:off + n], NN)
        off += n


def _inproj(x, g, w):
    rows = x.shape[0]
    tm = min(ROW_TILE, rows)
    widths = (ZA_W, ZB_W, ZC_W, ZD_W, ZG_W)
    return pl.pallas_call(
        _inproj_body,
        grid=(rows // tm,),
        in_specs=[
            pl.BlockSpec((tm, D_MODEL), lambda i: (i, 0)),
            pl.BlockSpec((1, D_MODEL), lambda i: (0, 0)),
            pl.BlockSpec((D_MODEL, N_IN_PACKED), lambda i: (0, 0)),
        ],
        out_specs=[pl.BlockSpec((tm, n), lambda i: (i, 0)) for n in widths],
        out_shape=[jax.ShapeDtypeStruct((rows, n), f32) for n in widths],
        compiler_params=_params("parallel"),
        name="in_proj",
    )(x, g, w)


def _outproj_body(ya_ref, yb_ref, yc_ref, yd_ref, zg_ref, x_ref, pe_ref, wo_ref, gp_ref, wp_ref, wg_ref, o_ref):
    g = zg_ref[...]
    sg = g * jax.nn.sigmoid(g)
    acc = None
    for i, y_ref in enumerate((ya_ref, yb_ref, yc_ref, yd_ref)):
        lo = i * W_MIX
        y = (y_ref[...] * sg[:, lo:lo + W_MIX]).astype(bf16)
        part = _dot(y, wo_ref[lo:lo + W_MIX, :], NN)
        acc = part if acc is None else acc + part
    x1 = x_ref[...] + _rms(acc, gp_ref[...])
    emb = _dot(pe_ref[...].astype(bf16), wp_ref[...], NN)
    gate = jax.nn.sigmoid(_dot(x1.astype(bf16), wg_ref[...], NN))
    o_ref[...] = x1 + emb * gate


def _outproj(ya, yb, yc, yd, zg, x, pe, layer, wo, gp, wp, wg):
    rows = x.shape[0]
    tm = min(ROW_TILE, rows)
    row = lambda n: pl.BlockSpec((tm, n), lambda i: (i, 0))
    full = lambda a: pl.BlockSpec(a.shape, lambda i: (0, 0))
    return pl.pallas_call(
        _outproj_body,
        grid=(rows // tm,),
        in_specs=[row(W_MIX), row(W_MIX), row(W_MIX), row(W_MIX), row(ZG_W), row(D_MODEL),
                  pl.BlockSpec((None, tm, D_PLE), lambda i: (layer, i, 0)),
                  full(wo), full(gp), full(wp), full(wg)],
        out_specs=row(D_MODEL),
        out_shape=jax.ShapeDtypeStruct((rows, D_MODEL), f32),
        compiler_params=_params("parallel"),
        name="out_proj",
    )(ya, yb, yc, yd, zg, x, pe, wo, gp, wp, wg)


def _lru_body(x_ref, c0_ref, h0_ref, cw_ref, cb_ref, wa_ref, ba_ref, wx_ref, bx_ref, lam_ref, gn_ref,
              y_ref, hl_ref, ext_ref, hc_ref, *, bb, tc, t_valid, reset_first):
    j = pl.program_id(1)

    @pl.when(j == 0)
    def _():
        ext_ref[:, 0:SUBLANES, :] = c0_ref[...]
        hc_ref[...] = h0_ref[...]

    @pl.when(j > 0)
    def _():
        ext_ref[:, 0:SUBLANES, :] = ext_ref[:, tc:tc + SUBLANES, :]

    n = bb * tc
    ext_ref[:, SUBLANES:SUBLANES + tc, :] = x_ref[...]
    xc = cw_ref[3:4, :] * x_ref[...].reshape(n, W_MIX) + cb_ref[...]
    for k in range(LRU_CONV - 1):
        s = SUBLANES - (LRU_CONV - 1) + k
        xc = xc + cw_ref[k:k + 1, :] * _cat([ext_ref[i, s:s + tc, :] for i in range(bb)], 0)
    gate_r = _mm(xc, wa_ref[...]) + ba_ref[...]
    gate_i = _mm(xc, wx_ref[...]) + bx_ref[...]
    log_a = -LRU_C * jax.nn.sigmoid(gate_r) * jax.nn.softplus(-lam_ref[...])
    a = jnp.exp(log_a)
    mult = jnp.sqrt(1.0 - a * a)
    if reset_first:
        rows = lax.broadcasted_iota(jnp.int32, (n, W_MIX), 0)
        mult = jnp.where(rows % tc + j * tc == 0, 1.0, mult)
    b = mult * jax.nn.sigmoid(gate_i) * xc
    ng = tc // SUBLANES
    a = a.reshape(bb * ng, SUBLANES, W_MIX)
    b = b.reshape(bb * ng, SUBLANES, W_MIX)
    sub = lax.broadcasted_iota(jnp.int32, (1, SUBLANES, W_MIX), 1)
    d = 1
    while d < SUBLANES:
        m = sub >= d
        b = a * jnp.where(m, pltpu.roll(b, d, 1), 0.0) + b
        a = a * jnp.where(m, pltpu.roll(a, d, 1), 1.0)
        d *= 2
    hs = []
    for i in range(bb):
        carry = jnp.broadcast_to(hc_ref[i], (SUBLANES, W_MIX))
        for g in range(ng):
            hg = a[i * ng + g] * carry + b[i * ng + g]
            hs.append(hg)
            carry = jnp.broadcast_to(hg[SUBLANES - 1:SUBLANES], (SUBLANES, W_MIX))
        hc_ref[i] = carry[0:1]
    h = _cat(hs, 0)
    y_ref[...] = _rms(h, gn_ref[...]).reshape(bb, tc, W_MIX)
    r_last = (t_valid - 1) % tc

    @pl.when(j == (t_valid - 1) // tc)
    def _():
        for i in range(bb):
            hl_ref[i] = h[i * tc + r_last:i * tc + r_last + 1, :]


def _lru(za, conv0, h0, p, *, bb, tc, t_valid, reset_first):
    b, t, _ = za.shape
    vec = pl.BlockSpec((1, W_MIX), lambda i, j: (0, 0))
    mat = pl.BlockSpec((W_MIX, W_MIX), lambda i, j: (0, 0))
    return pl.pallas_call(
        functools.partial(_lru_body, bb=bb, tc=tc, t_valid=t_valid, reset_first=reset_first),
        grid=(b // bb, t // tc),
        in_specs=[
            pl.BlockSpec((bb, tc, W_MIX), lambda i, j: (i, j, 0)),
            pl.BlockSpec((bb, SUBLANES, W_MIX), lambda i, j: (i, 0, 0)),
            pl.BlockSpec((bb, 1, W_MIX), lambda i, j: (i, 0, 0)),
            pl.BlockSpec((LRU_CONV, W_MIX), lambda i, j: (0, 0)),
            vec, mat, vec, mat, vec, vec, vec,
        ],
        out_specs=[
            pl.BlockSpec((bb, tc, W_MIX), lambda i, j: (i, j, 0)),
            pl.BlockSpec((bb, 1, W_MIX), lambda i, j: (i, 0, 0)),
        ],
        out_shape=[jax.ShapeDtypeStruct((b, t, W_MIX), f32), jax.ShapeDtypeStruct((b, 1, W_MIX), f32)],
        scratch_shapes=[pltpu.VMEM((bb, tc + SUBLANES, W_MIX), f32), pltpu.VMEM((bb, 1, W_MIX), f32)],
        compiler_params=_params("parallel", "arbitrary"),
        name="rglru",
    )(za, conv0, h0, p["conv_w"], p["conv_b"], p["wa"], p["ba"], p["wx"], p["bx"], p["lam"], p["norm"])


def _swa_body(q_ref, ko_ref, vo_ref, kp_ref, vp_ref, bias_ref, sink_ref, gn_ref, y_ref, *, bb, tq, mask_first):
    n = pl.program_id(1)
    nk = WINDOW + tq
    kw = SWA_KV_HEADS * SWA_HD
    lo = lax.broadcasted_iota(jnp.int32, (1, kw), 1) < SWA_HD
    row_head = lax.broadcasted_iota(jnp.int32, (SWA_HEADS * tq, 1), 0) // tq
    sink = jnp.zeros((SWA_HEADS * tq, 1), f32)
    for h in range(SWA_HEADS):
        sink = jnp.where(row_head == h, sink_ref[:, h:h + 1], sink)
    bias = bias_ref[...]
    if mask_first:
        col = lax.broadcasted_iota(jnp.int32, (1, nk), 1)
        bias = jnp.where((n > 0) | (col >= WINDOW), bias, -jnp.inf)
    swap = lambda x: pltpu.roll(x, SWA_HD, 1)
    qs = []
    for i in range(bb):
        q = q_ref[i] * (SWA_HD ** -0.5)
        q01, q23 = q[:, 0:kw], q[:, kw:2 * kw]
        qs.append(jnp.concatenate([jnp.where(lo, q01, 0.0), jnp.where(lo, swap(q01), 0.0),
                                   jnp.where(lo, 0.0, swap(q23)), jnp.where(lo, 0.0, q23)], axis=0))
    s = [_mm(qs[i], jnp.concatenate([kp_ref[i], ko_ref[i]], axis=0), NT) + bias for i in range(bb)]
    m = [jnp.maximum(jnp.max(x, axis=-1, keepdims=True), sink) for x in s]
    e = [jnp.exp(x - mx) for x, mx in zip(s, m)]
    den = [jnp.sum(x, axis=-1, keepdims=True) + jnp.exp(sink - mx) for x, mx in zip(e, m)]
    pv = [_mm(e[i], jnp.concatenate([vp_ref[i], vo_ref[i]], axis=0)) / den[i] for i in range(bb)]
    for i in range(bb):
        p = pv[i]
        o01 = jnp.where(lo, p[0:tq], swap(p[tq:2 * tq]))
        o23 = jnp.where(lo, swap(p[2 * tq:3 * tq]), p[3 * tq:4 * tq])
        y_ref[i] = _rms(jnp.concatenate([o01, o23], axis=1), gn_ref[...])


def _swa(zb, k_prev, v_prev, bias, sinks, gn, *, bb, tq, own_prev):
    b, t, _ = zb.shape
    kw = SWA_KV_HEADS * SWA_HD
    if own_prev:
        kp_spec = pl.BlockSpec((bb, WINDOW, kw), lambda i, n: (i, jnp.maximum(n - 1, 0), 2))
        vp_spec = pl.BlockSpec((bb, WINDOW, kw), lambda i, n: (i, jnp.maximum(n - 1, 0), 3))
    else:
        kp_spec = pl.BlockSpec((bb, WINDOW, kw), lambda i, n: (i, 0, 0))
        vp_spec = pl.BlockSpec((bb, WINDOW, kw), lambda i, n: (i, 0, 0))
    return pl.pallas_call(
        functools.partial(_swa_body, bb=bb, tq=tq, mask_first=own_prev),
        grid=(b // bb, t // tq),
        in_specs=[
            pl.BlockSpec((bb, tq, W_MIX), lambda i, n: (i, n, 0)),
            pl.BlockSpec((bb, tq, kw), lambda i, n: (i, n, 2)),
            pl.BlockSpec((bb, tq, kw), lambda i, n: (i, n, 3)),
            kp_spec, vp_spec,
            pl.BlockSpec(bias.shape, lambda i, n: (0, 0)),
            pl.BlockSpec((1, SWA_HEADS), lambda i, n: (0, 0)),
            pl.BlockSpec((1, W_MIX), lambda i, n: (0, 0)),
        ],
        out_specs=pl.BlockSpec((bb, tq, W_MIX), lambda i, n: (i, n, 0)),
        out_shape=jax.ShapeDtypeStruct((b, t, W_MIX), f32),
        compiler_params=_params("parallel", "arbitrary"),
        name="swa",
    )(zb, zb, zb, k_prev, v_prev, bias, sinks, gn)


def _col_bcast(row):
    n = row.shape[-1]
    return jnp.broadcast_to(row, (n, n)).T


def _gla_body(q_ref, k_ref, v_ref, t_ref, s0_ref, wup_ref, gb_ref, gn_ref, y_ref, sn_ref, st_ref,
              *, bb, tc, chunk, t_valid, masked):
    j = pl.program_id(1)
    n = bb * tc
    qk = GLA_HEADS * GLA_DK
    sr = lax.broadcasted_iota(jnp.int32, (qk, W_MIX), 0) // GLA_DK
    sc = lax.broadcasted_iota(jnp.int32, (qk, W_MIX), 1) // GLA_DV
    diag = sr == sc

    @pl.when(j == 0)
    def _():
        for i in range(bb):
            x = s0_ref[i]
            st_ref[i] = jnp.where(diag, jnp.concatenate([x] * GLA_HEADS, axis=1), 0.0)

    la = jax.nn.log_sigmoid(_mm(t_ref[...].reshape(n, LANES), wup_ref[...], passes=3) + gb_ref[...]) / GLA_TAU
    k = k_ref[...].reshape(n, qk)
    v = v_ref[...].reshape(n, W_MIX)
    if masked:
        live = lax.broadcasted_iota(jnp.int32, (n, 1), 0) % tc + j * tc < t_valid
        la, k, v = (jnp.where(live, x, 0.0) for x in (la, k, v))
    bc = _seg_cumsum(la, chunk)
    q_dec = q_ref[...].reshape(n, qk) * (GLA_DK ** -0.5) * jnp.exp(bc)
    k_inv = k * jnp.exp(-bc)
    khead = lax.broadcasted_iota(jnp.int32, (1, qk), 1) // GLA_DK
    vhead = lax.broadcasted_iota(jnp.int32, (1, W_MIX), 1) // GLA_DV
    r2 = lax.broadcasted_iota(jnp.int32, (GLA_HEADS * tc, tc), 0) % tc
    c2 = lax.broadcasted_iota(jnp.int32, (GLA_HEADS * tc, tc), 1)
    causal = (r2 // chunk == c2 // chunk) & (r2 >= c2)
    nch = tc // chunk
    ts = [slice(i * tc, (i + 1) * tc) for i in range(bb)]
    qs = [jnp.concatenate([jnp.where(khead == h, q_dec[t], 0.0) for h in range(GLA_HEADS)], axis=0) for t in ts]
    att = [jnp.where(causal, _mm(qs[i], k_inv[ts[i]], NT), 0.0) for i in range(bb)]
    full = [_mm(att[i], v[ts[i]]) for i in range(bb)]
    kv = []
    for c in range(nch):
        kv_c = []
        for i in range(bb):
            rs = slice(i * tc + c * chunk, i * tc + (c + 1) * chunk)
            b_end = bc[i * tc + (c + 1) * chunk - 1:i * tc + (c + 1) * chunk, :]
            k_end = k[rs] * jnp.exp(b_end - bc[rs])
            dec = _col_bcast(jnp.exp(b_end))
            kv_c.append((jnp.concatenate([dec] * (W_MIX // qk), axis=1),
                         jnp.where(diag, _mm(k_end, v[rs], TN), 0.0)))
        kv.append(kv_c)
    st = [st_ref[i] for i in range(bb)]
    inter = [[] for _ in range(bb)]
    for c in range(nch):
        for i in range(bb):
            inter[i].append(_mm(q_dec[i * tc + c * chunk:i * tc + (c + 1) * chunk], st[i]))
            st[i] = st[i] * kv[c][i][0] + kv[c][i][1]
    outs = []
    for i in range(bb):
        st_ref[i] = st[i]
        o = jnp.where(vhead == 0, full[i][0:tc], 0.0)
        for h in range(1, GLA_HEADS):
            o = o + jnp.where(vhead == h, full[i][h * tc:(h + 1) * tc], 0.0)
        outs.append(o + _cat(inter[i], 0))
    o = _cat(outs, 0)
    ms = _mm_exact_rhs(o * o, _head_ones(W_MIX, GLA_DV)) / GLA_DV
    y_ref[...] = (o * lax.rsqrt(ms + RMS_EPS) * gn_ref[...]).reshape(bb, tc, W_MIX)

    @pl.when(j == pl.num_programs(1) - 1)
    def _():
        for i in range(bb):
            x = st_ref[i]
            sn_ref[i] = sum(x[:, h * GLA_DV:(h + 1) * GLA_DV] for h in range(1, GLA_HEADS)) + x[:, 0:GLA_DV]


def _gla(zc, zd, s0, p, *, bb, tc, chunk, t_valid):
    b, t, _ = zc.shape
    qk = GLA_HEADS * GLA_DK
    state = pl.BlockSpec((bb, qk, GLA_DV), lambda i, j: (i, 0, 0))
    return pl.pallas_call(
        functools.partial(_gla_body, bb=bb, tc=tc, chunk=chunk, t_valid=t_valid, masked=t_valid < t),
        grid=(b // bb, t // tc),
        in_specs=[
            pl.BlockSpec((bb, tc, qk), lambda i, j: (i, j, 0)),
            pl.BlockSpec((bb, tc, qk), lambda i, j: (i, j, 1)),
            pl.BlockSpec((bb, tc, W_MIX), lambda i, j: (i, j, 1)),
            pl.BlockSpec((bb, tc, LANES), lambda i, j: (i, j, ZD_TAIL // LANES)),
            state,
            pl.BlockSpec((LANES, qk), lambda i, j: (0, 0)),
            pl.BlockSpec((1, qk), lambda i, j: (0, 0)),
            pl.BlockSpec((1, W_MIX), lambda i, j: (0, 0)),
        ],
        out_specs=[pl.BlockSpec((bb, tc, W_MIX), lambda i, j: (i, j, 0)), state],
        out_shape=[jax.ShapeDtypeStruct((b, t, W_MIX), f32), jax.ShapeDtypeStruct((b, qk, GLA_DV), f32)],
        scratch_shapes=[pltpu.VMEM((bb, qk, W_MIX), f32)],
        compiler_params=_params("parallel", "arbitrary"),
        name="gla",
    )(zc, zc, zc, zd, s0, p["w_up"], p["b"], p["norm"])


def _rwkv_body(u_ref, sh0_ref, s0_ref, mu_ref, w0_ref, w2_ref, a0_ref, a2_ref, kk_ref, ka_ref, rk_ref,
               lw_ref, lb_ref, y_ref, sn_ref, st_ref, up_ref, *, bb, tc, chunk, t_valid, masked):
    j = pl.program_id(1)

    @pl.when(j == 0)
    def _():
        for i in range(bb):
            for h in range(RWKV_HEADS):
                st_ref[i, h] = s0_ref[i, h].T
        up_ref[...] = sh0_ref[...]

    n = bb * tc
    u = u_ref[...].reshape(n, ZD_W)
    rolled = pltpu.roll(u, 1, 0)
    first = lax.broadcasted_iota(jnp.int32, (SUBLANES, 1), 0) == 0
    parts = []
    for i in range(bb):
        parts.append(jnp.where(first, up_ref[i], rolled[i * tc:i * tc + SUBLANES]))
        if tc > SUBLANES:
            parts.append(rolled[i * tc + SUBLANES:(i + 1) * tc])
        up_ref[i] = u[(i + 1) * tc - 1:(i + 1) * tc, :]
    u_prev = _cat(parts, 0)
    us = u + (u_prev - u) * mu_ref[...]
    r = us[:, 0:W_MIX]
    k = us[:, W_MIX:2 * W_MIX]
    v = us[:, 2 * W_MIX:3 * W_MIX]
    tail = us[:, ZD_TAIL:ZD_TAIL + LANES]
    z = w0_ref[...] + _mm(jnp.tanh(tail), w2_ref[...], passes=3)
    lw = -math.exp(-0.5) * jax.nn.sigmoid(z)
    a = jax.nn.sigmoid(a0_ref[...] + _mm(tail, a2_ref[...], passes=3))
    ones = _head_ones(W_MIX, RWKV_N)
    kk = k * kk_ref[...]
    kk = kk / jnp.maximum(jnp.sqrt(_mm_exact_rhs(kk * kk, ones)), 1e-12)
    k = k * (1.0 + (a - 1.0) * ka_ref[...])
    aa, bl, kl, vl = -kk, kk * a, k, v
    if masked:
        live = lax.broadcasted_iota(jnp.int32, (n, 1), 0) % tc + j * tc < t_valid
        lw, aa, bl, kl, vl = (jnp.where(live, x, 0.0) for x in (lw, aa, bl, kl, vl))

    cum = _seg_cumsum(lw, chunk)
    gi = jnp.exp(-cum)
    a_t = aa * jnp.exp(cum - lw)
    b_t = bl * gi
    k_t = kl * gi
    r_t = r * jnp.exp(cum)
    nch = tc // chunk
    r2 = lax.broadcasted_iota(jnp.int32, (tc, tc), 0)
    c2 = lax.broadcasted_iota(jnp.int32, (tc, tc), 1)
    same = r2 // chunk == c2 // chunk
    strict = same & (r2 > c2)
    incl = same & (r2 >= c2)

    chains = [(i, h) for i in range(bb) for h in range(RWKV_HEADS)]
    nq = len(chains)
    sl = lambda x, i, h: x[i * tc:(i + 1) * tc, h * RWKV_N:(h + 1) * RWKV_N]
    ah = [sl(a_t, i, h) for i, h in chains]
    rh = [sl(r_t, i, h) for i, h in chains]
    vh = [sl(vl, i, h) for i, h in chains]
    pair = [_mm(jnp.concatenate([ah[q], rh[q]], axis=0),
                jnp.concatenate([sl(b_t, i, h), sl(k_t, i, h)], axis=0), NT) for q, (i, h) in enumerate(chains)]
    pw = [jnp.where(strict, m[0:tc, 0:tc], 0.0) for m in pair]
    m_rb = [jnp.where(incl, m[tc:2 * tc, 0:tc], 0.0) for m in pair]
    akrk = [jnp.concatenate([jnp.where(strict, m[0:tc, tc:2 * tc], 0.0),
                             jnp.where(incl, m[tc:2 * tc, tc:2 * tc], 0.0)], axis=0) for m in pair]
    both_v = [_mm(akrk[q], vh[q]) for q in range(nq)]
    sol = [jnp.concatenate([ah[q], both_v[q][0:tc]], axis=1) for q in range(nq)]
    y_v = [x[tc:2 * tc] for x in both_v]
    wide = sol[0].shape[1]
    d = 1
    while True:
        last = 2 * d >= chunk
        res = [_mm(pw[q], sol[q] if last else jnp.concatenate([sol[q], pw[q]], axis=1)) for q in range(nq)]
        sol = [sol[q] + res[q][:, 0:wide] for q in range(nq)]
        if last:
            break
        pw = [x[:, wide:wide + tc] for x in res]
        d *= 2
    lhs_c, p2_c, mix_c, yv_c, kv_c, gcol_c = [], [], [], [], [], []
    for c in range(nch):
        rs = slice(c * chunk, (c + 1) * chunk)
        ce = [sl(cum, i, h)[(c + 1) * chunk - 1:(c + 1) * chunk, :] for i, h in chains]
        dec = [jnp.exp(ce[q] - sl(cum, i, h)[rs]) for q, (i, h) in enumerate(chains)]
        lhs_c.append([jnp.concatenate([sol[q][rs, 0:RWKV_N], rh[q][rs]], axis=0) for q in range(nq)])
        p2_c.append([sol[q][rs, RWKV_N:2 * RWKV_N] for q in range(nq)])
        mix_c.append([jnp.concatenate([m_rb[q][rs, rs], (sl(bl, i, h)[rs] * dec[q]).T], axis=0)
                      for q, (i, h) in enumerate(chains)])
        yv_c.append([y_v[q][rs] for q in range(nq)])
        kv_c.append([_mm(sl(kl, i, h)[rs] * dec[q], vh[q][rs], TN) for q, (i, h) in enumerate(chains)])
        gcol_c.append([jnp.broadcast_to(jnp.exp(ce[q]), (RWKV_N, RWKV_N)).T for q in range(nq)])
    st = [st_ref[i, h] for i, h in chains]
    y_parts = [[] for _ in chains]
    for c in range(nch):
        both = [_mm(lhs_c[c][q], st[q]) for q in range(nq)]
        u_c = [both[q][0:chunk] + p2_c[c][q] for q in range(nq)]
        mix = [_mm(mix_c[c][q], u_c[q]) for q in range(nq)]
        st = [gcol_c[c][q] * st[q] + kv_c[c][q] + mix[q][chunk:chunk + RWKV_N] for q in range(nq)]
        for q in range(nq):
            y_parts[q].append(both[q][chunk:2 * chunk] + mix[q][0:chunk] + yv_c[c][q])
    for q, (i, h) in enumerate(chains):
        st_ref[i, h] = st[q]
    y = _cat([_cat([_cat(y_parts[i * RWKV_HEADS + h], 0) for h in range(RWKV_HEADS)], 1) for i in range(bb)], 0)
    mean = _mm_exact_rhs(y, ones) / RWKV_N
    yc = y - mean
    var = _mm_exact_rhs(yc * yc, ones) / RWKV_N
    out = yc * lax.rsqrt(var + RWKV_LN_EPS) * lw_ref[...] + lb_ref[...]
    bonus = _mm_exact_rhs(r * k * rk_ref[...], ones)
    y_ref[...] = (out + bonus * v).reshape(bb, tc, W_MIX)

    @pl.when(j == pl.num_programs(1) - 1)
    def _():
        for i in range(bb):
            for h in range(RWKV_HEADS):
                sn_ref[i, h] = st_ref[i, h].T


def _rwkv(zd, shift0, s0, p, *, bb, tc, chunk, t_valid):
    b, t, _ = zd.shape
    vec = lambda n: pl.BlockSpec((1, n), lambda i, j: (0, 0))
    lora = pl.BlockSpec((LANES, W_MIX), lambda i, j: (0, 0))
    state = pl.BlockSpec((bb, RWKV_HEADS, RWKV_N, RWKV_N), lambda i, j: (i, 0, 0, 0))
    return pl.pallas_call(
        functools.partial(_rwkv_body, bb=bb, tc=tc, chunk=chunk, t_valid=t_valid, masked=t_valid < t),
        grid=(b // bb, t // tc),
        in_specs=[
            pl.BlockSpec((bb, tc, ZD_W), lambda i, j: (i, j, 0)),
            pl.BlockSpec((bb, 1, ZD_W), lambda i, j: (i, 0, 0)),
            state,
            vec(ZD_W), vec(W_MIX), lora, vec(W_MIX), lora, vec(W_MIX), vec(W_MIX), vec(W_MIX),
            vec(W_MIX), vec(W_MIX),
        ],
        out_specs=[pl.BlockSpec((bb, tc, W_MIX), lambda i, j: (i, j, 0)), state],
        out_shape=[jax.ShapeDtypeStruct((b, t, W_MIX), f32),
                   jax.ShapeDtypeStruct((b, RWKV_HEADS, RWKV_N, RWKV_N), f32)],
        scratch_shapes=[pltpu.VMEM((bb, RWKV_HEADS, RWKV_N, RWKV_N), f32), pltpu.VMEM((bb, 1, ZD_W), f32)],
        compiler_params=_params("parallel", "arbitrary"),
        name="rwkv7",
    )(zd, shift0, s0, p["mu"], p["w0"], p["w2"], p["a0"], p["a2"], p["k_k"], p["k_a"], p["r_k"],
      p["ln_w"], p["ln_b"])


def _block_diag(w):
    n, c, _ = w.shape
    out = jnp.zeros((n * c, n * c), w.dtype)
    for i in range(n):
        out = out.at[i * c:(i + 1) * c, i * c:(i + 1) * c].set(w[i])
    return out


def _rel_bucket_ids(tq):
    dist = WINDOW + np.arange(tq)[:, None] - np.arange(WINDOW + tq)[None, :]
    max_exact = REL_BUCKETS // 2
    n = np.maximum(dist, 0)
    log_ratio = (np.log(np.maximum(n, 1).astype(np.float32) / np.float32(max_exact))
                 / np.float32(math.log(REL_MAX_DIST / max_exact)))
    large = np.minimum(max_exact + (log_ratio * np.float32(REL_BUCKETS - max_exact)).astype(np.int32),
                       REL_BUCKETS - 1)
    bucket = np.where(n < max_exact, n, large)
    return np.where((dist >= 0) & (dist <= WINDOW), bucket, -1).astype(np.int32)


def _swa_bias(table, tq):
    ids = jnp.asarray(_rel_bucket_ids(tq))[None]
    out = jnp.full((SWA_HEADS, tq, WINDOW + tq), -jnp.inf, f32)
    for k in range(REL_BUCKETS):
        out = jnp.where(ids == k, table[k].astype(f32)[:, None, None], out)
    return out.reshape(SWA_HEADS * tq, WINDOW + tq)


def _pack_layer(i, w_in, lru_conv_w, lru_conv_b, lru_wa, lru_ba, lru_wx, lru_bx, lru_lambda, lru_norm,
                swa_sinks, swa_norm, gla_w_up, gla_b, gla_norm, rwkv_mu, rwkv_w0, rwkv_w2, rwkv_a0, rwkv_a2,
                rwkv_k_k, rwkv_k_a, rwkv_r_k, rwkv_ln_w, rwkv_ln_b, w_out, ple_proj, ple_gate,
                norm_pre, norm_post):
    w = w_in[i]
    c_ca = 5 * W_MIX
    c_ud = c_ca + GLA_RANK
    c_gate = c_ud + RWKV_SHIFT
    pad = jnp.zeros((D_MODEL, ZD_W - RWKV_SHIFT - GLA_RANK), w.dtype)
    w_packed = jnp.concatenate([w[:, :c_ca], w[:, c_ud:c_gate], w[:, c_ca:c_ud], pad, w[:, c_gate:]], axis=1)
    row = lambda a: a.reshape(1, -1)
    lora_lo = RWKV_SHIFT - 2 * RWKV_LORA - ZD_TAIL
    ca_lo = RWKV_SHIFT - ZD_TAIL
    zeros_tail = jnp.zeros((LANES, W_MIX), f32)
    return dict(
        w_in=w_packed.astype(bf16),
        norm_pre=row(norm_pre[i]), norm_post=row(norm_post[i]),
        w_out=w_out[i].astype(bf16), ple_proj=ple_proj[i].astype(bf16), ple_gate=ple_gate[i].astype(bf16),
        lru=dict(conv_w=lru_conv_w[i], conv_b=row(lru_conv_b[i]), wa=_block_diag(lru_wa[i]), ba=row(lru_ba[i]),
                 wx=_block_diag(lru_wx[i]), bx=row(lru_bx[i]), lam=row(lru_lambda[i]), norm=row(lru_norm[i])),
        swa_sinks=row(swa_sinks[i]), swa_norm=row(swa_norm[i]),
        gla=dict(w_up=jnp.zeros((LANES, GLA_HEADS * GLA_DK), f32).at[ca_lo:ca_lo + GLA_RANK].set(gla_w_up[i]),
                 b=row(gla_b[i]), norm=row(jnp.tile(gla_norm[i], GLA_HEADS))),
        rwkv=dict(mu=row(jnp.pad(rwkv_mu[i], (0, ZD_W - RWKV_SHIFT))), w0=row(rwkv_w0[i]),
                  w2=zeros_tail.at[lora_lo:lora_lo + RWKV_LORA].set(rwkv_w2[i]), a0=row(rwkv_a0[i]),
                  a2=zeros_tail.at[lora_lo + RWKV_LORA:lora_lo + 2 * RWKV_LORA].set(rwkv_a2[i]),
                  k_k=row(rwkv_k_k[i]), k_a=row(rwkv_k_a[i]), r_k=row(rwkv_r_k[i]),
                  ln_w=row(rwkv_ln_w[i]), ln_b=row(rwkv_ln_b[i])),
    )


PROMPT_TILES = dict(lru=(1, 256), swa=(4, WINDOW), gla=(4, 128, 64), rwkv=(4, 128, 64))


def _sample_tiles(t_pad):
    return dict(lru=(16, t_pad), swa=(8, t_pad), gla=(16, t_pad, t_pad), rwkv=(16, t_pad, t_pad))


def _layer(x, pe, layer, lp, bias, state, *, tiles, t_valid):
    b, t, _ = x.shape
    prompt = state is None
    rows = b * t
    za, zb, zc, zd, zg = _inproj(x.reshape(rows, D_MODEL), lp["norm_pre"], lp["w_in"])
    za = za.reshape(b, t, ZA_W)
    zb = zb.reshape(b, t, ZB_W)
    zc = zc.reshape(b, t, ZC_W)
    zd = zd.reshape(b, t, ZD_W)
    if prompt:
        h0 = jnp.zeros((b, 1, W_MIX), f32)
        conv0 = jnp.zeros((b, SUBLANES, W_MIX), f32)
        kbuf = vbuf = zb
        sg0 = jnp.zeros((b, GLA_HEADS * GLA_DK, GLA_DV), f32)
        sr0 = jnp.zeros((b, RWKV_HEADS, RWKV_N, RWKV_N), f32)
        sh0 = jnp.zeros((b, 1, ZD_W), f32)
    else:
        h0, conv0, kbuf, vbuf, sg0, sr0, sh0 = state
        h0 = h0.reshape(b, 1, W_MIX)
        conv0 = jnp.pad(conv0, ((0, 0), (SUBLANES - (LRU_CONV - 1), 0), (0, 0)))
        kbuf = kbuf.reshape(b, WINDOW, SWA_KV_HEADS * SWA_HD)
        vbuf = vbuf.reshape(b, WINDOW, SWA_KV_HEADS * SWA_HD)
        sg0 = sg0.reshape(b, GLA_HEADS * GLA_DK, GLA_DV)
        sh0 = jnp.pad(sh0, ((0, 0), (0, ZD_W - RWKV_SHIFT))).reshape(b, 1, ZD_W)
    bb, tc = tiles["lru"]
    ya, h_new = _lru(za, conv0, h0, lp["lru"], bb=bb, tc=tc, t_valid=t_valid, reset_first=prompt)
    bb, tq = tiles["swa"]
    yb = _swa(zb, kbuf, vbuf, bias, lp["swa_sinks"], lp["swa_norm"], bb=bb, tq=tq, own_prev=prompt)
    bb, tc, chunk = tiles["gla"]
    yc, sg_new = _gla(zc, zd, sg0, lp["gla"], bb=bb, tc=tc, chunk=chunk, t_valid=t_valid)
    bb, tc, chunk = tiles["rwkv"]
    yd, sr_new = _rwkv(zd, sh0, sr0, lp["rwkv"], bb=bb, tc=tc, chunk=chunk, t_valid=t_valid)
    flat = lambda a: a.reshape(rows, a.shape[-1])
    x_new = _outproj(flat(ya), flat(yb), flat(yc), flat(yd), zg, flat(x), pe.reshape(pe.shape[0], rows, D_PLE), layer,
                     lp["w_out"], lp["norm_post"], lp["ple_proj"], lp["ple_gate"]).reshape(b, t, D_MODEL)
    kv_rows = slice(t_valid - WINDOW, t_valid) if prompt else slice(0, t_valid)
    kw = SWA_KV_HEADS * SWA_HD
    k_new = zb[:, kv_rows, W_MIX:W_MIX + kw].reshape(b, -1, SWA_KV_HEADS, SWA_HD)
    v_new = zb[:, kv_rows, W_MIX + kw:W_MIX + 2 * kw].reshape(b, -1, SWA_KV_HEADS, SWA_HD)
    new_state = (h_new.reshape(b, W_MIX), za[:, t_valid - (LRU_CONV - 1):t_valid], k_new, v_new,
                 sg_new.reshape(b, GLA_HEADS, GLA_DK, GLA_DV), sr_new, zd[:, t_valid - 1, :RWKV_SHIFT])
    return x_new, new_state


def kernel(x_prompt, x_sample, state_lru_h, state_lru_conv, cache_swa_k, cache_swa_v, state_gla, state_rwkv, state_rwkv_shift, p_prompt, p_sample, rel_bias, norm_pre, norm_post, w_in, lru_conv_w, lru_conv_b, lru_wa, lru_ba, lru_wx, lru_bx, lru_lambda, lru_norm, swa_sinks, swa_norm, gla_w_up, gla_b, gla_norm, rwkv_mu, rwkv_w0, rwkv_w2, rwkv_a0, rwkv_a2, rwkv_k_k, rwkv_k_a, rwkv_r_k, rwkv_ln_w, rwkv_ln_b, w_out, ple_proj, ple_gate):
    depth = w_in.shape[0]
    t_p = x_prompt.shape[1]
    t_s = x_sample.shape[1]
    t_s_pad = -(-t_s // SUBLANES) * SUBLANES
    pad_t = lambda a, axis: jnp.pad(a, [(0, t_s_pad - t_s) if d == axis else (0, 0) for d in range(a.ndim)])
    xs = pad_t(x_sample, 1)
    ps = pad_t(p_sample, 2)
    xp = x_prompt
    bias_p = _swa_bias(rel_bias, WINDOW)
    bias_s = _swa_bias(rel_bias, t_s_pad)
    tiles_s = _sample_tiles(t_s_pad)
    sts_p, sts_s = [], []
    for i in range(depth):
        lp = _pack_layer(i, w_in, lru_conv_w, lru_conv_b, lru_wa, lru_ba, lru_wx, lru_bx, lru_lambda, lru_norm,
                         swa_sinks, swa_norm, gla_w_up, gla_b, gla_norm, rwkv_mu, rwkv_w0, rwkv_w2, rwkv_a0,
                         rwkv_a2, rwkv_k_k, rwkv_k_a, rwkv_r_k, rwkv_ln_w, rwkv_ln_b, w_out, ple_proj, ple_gate,
                         norm_pre, norm_post)
        xp, st_p = _layer(xp, p_prompt, i, lp, bias_p, None, tiles=PROMPT_TILES, t_valid=t_p)
        xs, st_s = _layer(xs, ps, i, lp, bias_s,
                          (state_lru_h[i], state_lru_conv[i], cache_swa_k[i], cache_swa_v[i],
                           state_gla[i], state_rwkv[i], state_rwkv_shift[i]),
                          tiles=tiles_s, t_valid=t_s)
        sts_p.append(st_p)
        sts_s.append(st_s)
    hp, cp, kp, vp, gp, rp, shp = [jnp.stack([s[j] for s in sts_p]) for j in range(7)]
    hs, cs, ks, vs, gs, rs, shs = [jnp.stack([s[j] for s in sts_s]) for j in range(7)]
    return (xp, xs[:, :t_s], hp, hs, cp, cs, kp, ks, vp, vs, gp, gs, rp, rs, shp, shs)
```

```python
import functools
import math

import jax
import jax.numpy as jnp
import numpy as np
from jax import lax
from jax.experimental import pallas as pl
from jax.experimental.pallas import tpu as pltpu

f32 = jnp.float32
bf16 = jnp.bfloat16

D_MODEL = 1024
D_PLE = 256
RMS_EPS = 1e-6
W_MIX = 256
LRU_BLOCKS = 4
LRU_CONV = 4
LRU_C = 8.0
SWA_HEADS = 4
SWA_KV_HEADS = 2
SWA_HD = 64
WINDOW = 128
REL_BUCKETS = 32
REL_MAX_DIST = 128
GLA_HEADS = 4
GLA_DK = 32
GLA_DV = 64
GLA_RANK = 16
GLA_TAU = 16.0
RWKV_HEADS = 4
RWKV_N = 64
RWKV_LORA = 32
RWKV_SHIFT = 3 * W_MIX + 2 * RWKV_LORA
RWKV_LN_EPS = 64e-5

ZA_W = 256
ZB_W = 512
ZC_W = 512
ZD_W = 896
ZG_W = 1024
ZD_TAIL = 768
N_IN_PACKED = ZA_W + ZB_W + ZC_W + ZD_W + ZG_W

LANES = 128
SUBLANES = 8
VMEM_LIMIT = 48 * 1024 * 1024

ROW_TILE = 512

NN = (((1,), (0,)), ((), ()))
NT = (((1,), (1,)), ((), ()))
TN = (((0,), (0,)), ((), ()))


def _dot(a, b, dn):
    return lax.dot_general(a, b, dn, preferred_element_type=f32)


def _split2(x):
    hi = x.astype(bf16)
    lo = (x - hi.astype(f32)).astype(bf16)
    return hi, lo


def _mm(a, b, dn=NN, passes=1):
    if passes == 1:
        return _dot(a.astype(bf16), b.astype(bf16), dn)
    ah, al = _split2(a)
    bh, bl = _split2(b)
    return _dot(ah, bh, dn) + (_dot(ah, bl, dn) + _dot(al, bh, dn))


def _mm_exact_rhs(a, b_bf16):
    hi, lo = _split2(a)
    return _dot(hi, b_bf16, NN) + _dot(lo, b_bf16, NN)


def _head_ones(n, width):
    r = lax.broadcasted_iota(jnp.int32, (n, n), 0) // width
    c = lax.broadcasted_iota(jnp.int32, (n, n), 1) // width
    return jnp.where(r == c, 1.0, 0.0).astype(bf16)


def _rms(x, g):
    return x * lax.rsqrt(jnp.mean(x * x, axis=-1, keepdims=True) + RMS_EPS) * g


def _seg_cumsum(x, seg):
    n, w = x.shape
    x3 = x.reshape(n // SUBLANES, SUBLANES, w)
    sub = lax.broadcasted_iota(jnp.int32, (1, SUBLANES, w), 1)
    d = 1
    while d < SUBLANES:
        x3 = x3 + jnp.where(sub >= d, pltpu.roll(x3, d, 1), 0.0)
        d *= 2
    x = x3.reshape(n, w)
    rows = lax.broadcasted_iota(jnp.int32, (n, 1), 0) % seg
    tot = jnp.broadcast_to(x3[:, SUBLANES - 1:SUBLANES, :], x3.shape).reshape(n, w)
    while d < seg:
        add = jnp.where(rows >= d, pltpu.roll(tot, d, 0), 0.0)
        x = x + add
        tot = tot + add
        d *= 2
    return x


def _cat(xs, axis):
    return xs[0] if len(xs) == 1 else jnp.concatenate(xs, axis=axis)


def _params(*sem):
    return pltpu.CompilerParams(dimension_semantics=sem, vmem_limit_bytes=VMEM_LIMIT)


def _inproj_body(x_ref, g_ref, w_ref, za_ref, zb_ref, zc_ref, zd_ref, zg_ref):
    h = _rms(x_ref[...], g_ref[...]).astype(bf16)
    off = 0
    for o_ref in (za_ref, zb_ref, zc_ref, zd_ref, zg_ref):
        n = o_ref.shape[-1]
        o_ref[...] = _dot(h, w_ref[:, off:off + n], NN).astype(o_ref.dtype)
        off += n


def _inproj(x, g, w, gate_dtype):
    rows = x.shape[0]
    tm = min(ROW_TILE, rows)
    widths = (ZA_W, ZB_W, ZC_W, ZD_W, ZG_W)
    return pl.pallas_call(
        _inproj_body,
        grid=(rows // tm,),
        in_specs=[
            pl.BlockSpec((tm, D_MODEL), lambda i: (i, 0)),
            pl.BlockSpec((1, D_MODEL), lambda i: (0, 0)),
            pl.BlockSpec((D_MODEL, N_IN_PACKED), lambda i: (0, 0)),
        ],
        out_specs=[pl.BlockSpec((tm, n), lambda i: (i, 0)) for n in widths],
        out_shape=[jax.ShapeDtypeStruct((rows, n), gate_dtype if n == ZG_W else f32) for n in widths],
        compiler_params=_params("parallel"),
        name="in_proj",
    )(x, g, w)


def _outproj_body(ya_ref, yb_ref, yc_ref, yd_ref, zg_ref, x_ref, pe_ref, wo_ref, gp_ref, wp_ref, wg_ref, o_ref):
    g = zg_ref[...].astype(f32)
    sg = g * jax.nn.sigmoid(g)
    acc = None
    for i, y_ref in enumerate((ya_ref, yb_ref, yc_ref, yd_ref)):
        lo = i * W_MIX
        y = (y_ref[...].astype(f32) * sg[:, lo:lo + W_MIX]).astype(bf16)
        part = _dot(y, wo_ref[lo:lo + W_MIX, :], NN)
        acc = part if acc is None else acc + part
    x1 = x_ref[...] + _rms(acc, gp_ref[...])
    emb = _dot(pe_ref[...].astype(bf16), wp_ref[...], NN)
    gate = jax.nn.sigmoid(_dot(x1.astype(bf16), wg_ref[...], NN))
    o_ref[...] = x1 + emb * gate


def _outproj(ya, yb, yc, yd, zg, x, pe, layer, wo, gp, wp, wg):
    rows = x.shape[0]
    tm = min(ROW_TILE, rows)
    row = lambda n: pl.BlockSpec((tm, n), lambda i: (i, 0))
    full = lambda a: pl.BlockSpec(a.shape, lambda i: (0, 0))
    return pl.pallas_call(
        _outproj_body,
        grid=(rows // tm,),
        in_specs=[row(W_MIX), row(W_MIX), row(W_MIX), row(W_MIX), row(ZG_W), row(D_MODEL),
                  pl.BlockSpec((None, tm, D_PLE), lambda i: (layer, i, 0)),
                  full(wo), full(gp), full(wp), full(wg)],
        out_specs=row(D_MODEL),
        out_shape=jax.ShapeDtypeStruct((rows, D_MODEL), f32),
        compiler_params=_params("parallel"),
        name="out_proj",
    )(ya, yb, yc, yd, zg, x, pe, wo, gp, wp, wg)


def _lru_body(x_ref, c0_ref, h0_ref, cw_ref, cb_ref, wa_ref, ba_ref, wx_ref, bx_ref, lam_ref, gn_ref,
              y_ref, hl_ref, ext_ref, hc_ref, *, bb, tc, t_valid, reset_first):
    j = pl.program_id(1)

    @pl.when(j == 0)
    def _():
        ext_ref[:, 0:SUBLANES, :] = c0_ref[...]
        hc_ref[...] = h0_ref[...]

    @pl.when(j > 0)
    def _():
        ext_ref[:, 0:SUBLANES, :] = ext_ref[:, tc:tc + SUBLANES, :]

    n = bb * tc
    ext_ref[:, SUBLANES:SUBLANES + tc, :] = x_ref[...]
    xc = cw_ref[3:4, :] * x_ref[...].reshape(n, W_MIX) + cb_ref[...]
    for k in range(LRU_CONV - 1):
        s = SUBLANES - (LRU_CONV - 1) + k
        xc = xc + cw_ref[k:k + 1, :] * _cat([ext_ref[i, s:s + tc, :] for i in range(bb)], 0)
    gate_r = _mm(xc, wa_ref[...]) + ba_ref[...]
    gate_i = _mm(xc, wx_ref[...]) + bx_ref[...]
    log_a = -LRU_C * jax.nn.sigmoid(gate_r) * jax.nn.softplus(-lam_ref[...])
    a = jnp.exp(log_a)
    mult = jnp.sqrt(1.0 - a * a)
    if reset_first:
        rows = lax.broadcasted_iota(jnp.int32, (n, W_MIX), 0)
        mult = jnp.where(rows % tc + j * tc == 0, 1.0, mult)
    b = mult * jax.nn.sigmoid(gate_i) * xc
    ng = tc // SUBLANES
    a = a.reshape(bb * ng, SUBLANES, W_MIX)
    b = b.reshape(bb * ng, SUBLANES, W_MIX)
    sub = lax.broadcasted_iota(jnp.int32, (1, SUBLANES, W_MIX), 1)
    d = 1
    while d < SUBLANES:
        m = sub >= d
        b = a * jnp.where(m, pltpu.roll(b, d, 1), 0.0) + b
        a = a * jnp.where(m, pltpu.roll(a, d, 1), 1.0)
        d *= 2
    hs = []
    for i in range(bb):
        carry = jnp.broadcast_to(hc_ref[i], (SUBLANES, W_MIX))
        for g in range(ng):
            hg = a[i * ng + g] * carry + b[i * ng + g]
            hs.append(hg)
            carry = jnp.broadcast_to(hg[SUBLANES - 1:SUBLANES], (SUBLANES, W_MIX))
        hc_ref[i] = carry[0:1]
    h = _cat(hs, 0)
    y_ref[...] = _rms(h, gn_ref[...]).reshape(bb, tc, W_MIX).astype(y_ref.dtype)
    r_last = (t_valid - 1) % tc

    @pl.when(j == (t_valid - 1) // tc)
    def _():
        for i in range(bb):
            hl_ref[i] = h[i * tc + r_last:i * tc + r_last + 1, :]


def _lru(za, conv0, h0, p, *, bb, tc, t_valid, reset_first, y_dtype):
    b, t, _ = za.shape
    vec = pl.BlockSpec((1, W_MIX), lambda i, j: (0, 0))
    mat = pl.BlockSpec((W_MIX, W_MIX), lambda i, j: (0, 0))
    return pl.pallas_call(
        functools.partial(_lru_body, bb=bb, tc=tc, t_valid=t_valid, reset_first=reset_first),
        grid=(b // bb, t // tc),
        in_specs=[
            pl.BlockSpec((bb, tc, W_MIX), lambda i, j: (i, j, 0)),
            pl.BlockSpec((bb, SUBLANES, W_MIX), lambda i, j: (i, 0, 0)),
            pl.BlockSpec((bb, 1, W_MIX), lambda i, j: (i, 0, 0)),
            pl.BlockSpec((LRU_CONV, W_MIX), lambda i, j: (0, 0)),
            vec, mat, vec, mat, vec, vec, vec,
        ],
        out_specs=[
            pl.BlockSpec((bb, tc, W_MIX), lambda i, j: (i, j, 0)),
            pl.BlockSpec((bb, 1, W_MIX), lambda i, j: (i, 0, 0)),
        ],
        out_shape=[jax.ShapeDtypeStruct((b, t, W_MIX), y_dtype), jax.ShapeDtypeStruct((b, 1, W_MIX), f32)],
        scratch_shapes=[pltpu.VMEM((bb, tc + SUBLANES, W_MIX), f32), pltpu.VMEM((bb, 1, W_MIX), f32)],
        compiler_params=_params("parallel", "arbitrary"),
        name="rglru",
    )(za, conv0, h0, p["conv_w"], p["conv_b"], p["wa"], p["ba"], p["wx"], p["bx"], p["lam"], p["norm"])


def _swa_body(q_ref, ko_ref, vo_ref, kp_ref, vp_ref, bias_ref, sink_ref, gn_ref, y_ref, *, bb, tq, mask_first):
    n = pl.program_id(1)
    nk = WINDOW + tq
    kw = SWA_KV_HEADS * SWA_HD
    lo = lax.broadcasted_iota(jnp.int32, (1, kw), 1) < SWA_HD
    row_head = lax.broadcasted_iota(jnp.int32, (SWA_HEADS * tq, 1), 0) // tq
    sink = jnp.zeros((SWA_HEADS * tq, 1), f32)
    for h in range(SWA_HEADS):
        sink = jnp.where(row_head == h, sink_ref[:, h:h + 1], sink)
    bias = bias_ref[...]
    if mask_first:
        col = lax.broadcasted_iota(jnp.int32, (1, nk), 1)
        bias = jnp.where((n > 0) | (col >= WINDOW), bias, -jnp.inf)
    swap = lambda x: pltpu.roll(x, SWA_HD, 1)
    qs = []
    for i in range(bb):
        q = q_ref[i] * (SWA_HD ** -0.5)
        q01, q23 = q[:, 0:kw], q[:, kw:2 * kw]
        qs.append(jnp.concatenate([jnp.where(lo, q01, 0.0), jnp.where(lo, swap(q01), 0.0),
                                   jnp.where(lo, 0.0, swap(q23)), jnp.where(lo, 0.0, q23)], axis=0))
    s = [_mm(qs[i], jnp.concatenate([kp_ref[i], ko_ref[i]], axis=0), NT) + bias for i in range(bb)]
    m = [jnp.maximum(jnp.max(x, axis=-1, keepdims=True), sink) for x in s]
    e = [jnp.exp(x - mx) for x, mx in zip(s, m)]
    den = [jnp.sum(x, axis=-1, keepdims=True) + jnp.exp(sink - mx) for x, mx in zip(e, m)]
    pv = [_mm(e[i], jnp.concatenate([vp_ref[i], vo_ref[i]], axis=0)) / den[i] for i in range(bb)]
    for i in range(bb):
        p = pv[i]
        o01 = jnp.where(lo, p[0:tq], swap(p[tq:2 * tq]))
        o23 = jnp.where(lo, swap(p[2 * tq:3 * tq]), p[3 * tq:4 * tq])
        y_ref[i] = _rms(jnp.concatenate([o01, o23], axis=1), gn_ref[...]).astype(y_ref.dtype)


def _swa(zb, k_prev, v_prev, layer, bias, sinks, gn, *, bb, tq, own_prev, y_dtype):
    b, t, _ = zb.shape
    kw = SWA_KV_HEADS * SWA_HD
    if own_prev:
        kp_spec = pl.BlockSpec((None, bb, WINDOW, kw), lambda i, n: (0, i, jnp.maximum(n - 1, 0), 2))
        vp_spec = pl.BlockSpec((None, bb, WINDOW, kw), lambda i, n: (0, i, jnp.maximum(n - 1, 0), 3))
    else:
        kp_spec = pl.BlockSpec((None, bb, WINDOW, kw), lambda i, n: (layer, i, 0, 0))
        vp_spec = pl.BlockSpec((None, bb, WINDOW, kw), lambda i, n: (layer, i, 0, 0))
    return pl.pallas_call(
        functools.partial(_swa_body, bb=bb, tq=tq, mask_first=own_prev),
        grid=(b // bb, t // tq),
        in_specs=[
            pl.BlockSpec((bb, tq, W_MIX), lambda i, n: (i, n, 0)),
            pl.BlockSpec((bb, tq, kw), lambda i, n: (i, n, 2)),
            pl.BlockSpec((bb, tq, kw), lambda i, n: (i, n, 3)),
            kp_spec, vp_spec,
            pl.BlockSpec(bias.shape, lambda i, n: (0, 0)),
            pl.BlockSpec((1, SWA_HEADS), lambda i, n: (0, 0)),
            pl.BlockSpec((1, W_MIX), lambda i, n: (0, 0)),
        ],
        out_specs=pl.BlockSpec((bb, tq, W_MIX), lambda i, n: (i, n, 0)),
        out_shape=jax.ShapeDtypeStruct((b, t, W_MIX), y_dtype),
        compiler_params=_params("parallel", "arbitrary"),
        name="swa",
    )(zb, zb, zb, k_prev, v_prev, bias, sinks, gn)


def _col_bcast(row):
    n = row.shape[-1]
    return jnp.broadcast_to(row, (n, n)).T


def _gla_body(q_ref, k_ref, v_ref, t_ref, s0_ref, wup_ref, gb_ref, gn_ref, y_ref, sn_ref, st_ref,
              *, bb, tc, chunk, t_valid, masked):
    j = pl.program_id(1)
    n = bb * tc
    qk = GLA_HEADS * GLA_DK
    sr = lax.broadcasted_iota(jnp.int32, (qk, W_MIX), 0) // GLA_DK
    sc = lax.broadcasted_iota(jnp.int32, (qk, W_MIX), 1) // GLA_DV
    diag = sr == sc

    @pl.when(j == 0)
    def _():
        for i in range(bb):
            x = s0_ref[i]
            st_ref[i] = jnp.where(diag, jnp.concatenate([x] * GLA_HEADS, axis=1), 0.0)

    la = jax.nn.log_sigmoid(_mm(t_ref[...].reshape(n, LANES), wup_ref[...], passes=3) + gb_ref[...]) / GLA_TAU
    k = k_ref[...].reshape(n, qk)
    v = v_ref[...].reshape(n, W_MIX)
    if masked:
        live = lax.broadcasted_iota(jnp.int32, (n, 1), 0) % tc + j * tc < t_valid
        la, k, v = (jnp.where(live, x, 0.0) for x in (la, k, v))
    bc = _seg_cumsum(la, chunk)
    q_dec = q_ref[...].reshape(n, qk) * (GLA_DK ** -0.5) * jnp.exp(bc)
    k_inv = k * jnp.exp(-bc)
    khead = lax.broadcasted_iota(jnp.int32, (1, qk), 1) // GLA_DK
    vhead = lax.broadcasted_iota(jnp.int32, (1, W_MIX), 1) // GLA_DV
    r2 = lax.broadcasted_iota(jnp.int32, (GLA_HEADS * tc, tc), 0) % tc
    c2 = lax.broadcasted_iota(jnp.int32, (GLA_HEADS * tc, tc), 1)
    causal = (r2 // chunk == c2 // chunk) & (r2 >= c2)
    nch = tc // chunk
    ts = [slice(i * tc, (i + 1) * tc) for i in range(bb)]
    qs = [jnp.concatenate([jnp.where(khead == h, q_dec[t], 0.0) for h in range(GLA_HEADS)], axis=0) for t in ts]
    att = [jnp.where(causal, _mm(qs[i], k_inv[ts[i]], NT), 0.0) for i in range(bb)]
    full = [_mm(att[i], v[ts[i]]) for i in range(bb)]
    kv = []
    for c in range(nch):
        kv_c = []
        for i in range(bb):
            rs = slice(i * tc + c * chunk, i * tc + (c + 1) * chunk)
            b_end = bc[i * tc + (c + 1) * chunk - 1:i * tc + (c + 1) * chunk, :]
            k_end = k[rs] * jnp.exp(b_end - bc[rs])
            dec = _col_bcast(jnp.exp(b_end))
            kv_c.append((jnp.concatenate([dec] * (W_MIX // qk), axis=1),
                         jnp.where(diag, _mm(k_end, v[rs], TN), 0.0)))
        kv.append(kv_c)
    st = [st_ref[i] for i in range(bb)]
    inter = [[] for _ in range(bb)]
    for c in range(nch):
        for i in range(bb):
            inter[i].append(_mm(q_dec[i * tc + c * chunk:i * tc + (c + 1) * chunk], st[i]))
            st[i] = st[i] * kv[c][i][0] + kv[c][i][1]
    outs = []
    for i in range(bb):
        st_ref[i] = st[i]
        o = jnp.where(vhead == 0, full[i][0:tc], 0.0)
        for h in range(1, GLA_HEADS):
            o = o + jnp.where(vhead == h, full[i][h * tc:(h + 1) * tc], 0.0)
        outs.append(o + _cat(inter[i], 0))
    o = _cat(outs, 0)
    ms = _mm_exact_rhs(o * o, _head_ones(W_MIX, GLA_DV)) / GLA_DV
    y_ref[...] = (o * lax.rsqrt(ms + RMS_EPS) * gn_ref[...]).reshape(bb, tc, W_MIX).astype(y_ref.dtype)

    @pl.when(j == pl.num_programs(1) - 1)
    def _():
        for i in range(bb):
            x = st_ref[i]
            sn_ref[i] = sum(x[:, h * GLA_DV:(h + 1) * GLA_DV] for h in range(1, GLA_HEADS)) + x[:, 0:GLA_DV]


def _gla(zc, zd, s0, layer, p, *, bb, tc, chunk, t_valid, y_dtype):
    b, t, _ = zc.shape
    qk = GLA_HEADS * GLA_DK
    state = pl.BlockSpec((bb, qk, GLA_DV), lambda i, j: (i, 0, 0))
    state_in = pl.BlockSpec((None, bb, qk, GLA_DV), lambda i, j: (layer, i, 0, 0))
    return pl.pallas_call(
        functools.partial(_gla_body, bb=bb, tc=tc, chunk=chunk, t_valid=t_valid, masked=t_valid < t),
        grid=(b // bb, t // tc),
        in_specs=[
            pl.BlockSpec((bb, tc, qk), lambda i, j: (i, j, 0)),
            pl.BlockSpec((bb, tc, qk), lambda i, j: (i, j, 1)),
            pl.BlockSpec((bb, tc, W_MIX), lambda i, j: (i, j, 1)),
            pl.BlockSpec((bb, tc, LANES), lambda i, j: (i, j, ZD_TAIL // LANES)),
            state_in,
            pl.BlockSpec((LANES, qk), lambda i, j: (0, 0)),
            pl.BlockSpec((1, qk), lambda i, j: (0, 0)),
            pl.BlockSpec((1, W_MIX), lambda i, j: (0, 0)),
        ],
        out_specs=[pl.BlockSpec((bb, tc, W_MIX), lambda i, j: (i, j, 0)), state],
        out_shape=[jax.ShapeDtypeStruct((b, t, W_MIX), y_dtype), jax.ShapeDtypeStruct((b, qk, GLA_DV), f32)],
        scratch_shapes=[pltpu.VMEM((bb, qk, W_MIX), f32)],
        compiler_params=_params("parallel", "arbitrary"),
        name="gla",
    )(zc, zc, zc, zd, s0, p["w_up"], p["b"], p["norm"])


def _rwkv_body(u_ref, sh0_ref, s0_ref, mu_ref, w0_ref, w2_ref, a0_ref, a2_ref, kk_ref, ka_ref, rk_ref,
               lw_ref, lb_ref, y_ref, sn_ref, st_ref, up_ref, *, bb, tc, chunk, t_valid, masked):
    j = pl.program_id(1)

    @pl.when(j == 0)
    def _():
        for i in range(bb):
            for h in range(RWKV_HEADS):
                st_ref[i, h] = s0_ref[i, h].T
        up_ref[...] = sh0_ref[...]

    n = bb * tc
    u = u_ref[...].reshape(n, ZD_W)
    rolled = pltpu.roll(u, 1, 0)
    first = lax.broadcasted_iota(jnp.int32, (SUBLANES, 1), 0) == 0
    parts = []
    for i in range(bb):
        parts.append(jnp.where(first, up_ref[i], rolled[i * tc:i * tc + SUBLANES]))
        if tc > SUBLANES:
            parts.append(rolled[i * tc + SUBLANES:(i + 1) * tc])
        up_ref[i] = u[(i + 1) * tc - 1:(i + 1) * tc, :]
    u_prev = _cat(parts, 0)
    us = u + (u_prev - u) * mu_ref[...]
    r = us[:, 0:W_MIX]
    k = us[:, W_MIX:2 * W_MIX]
    v = us[:, 2 * W_MIX:3 * W_MIX]
    tail = us[:, ZD_TAIL:ZD_TAIL + LANES]
    z = w0_ref[...] + _mm(jnp.tanh(tail), w2_ref[...], passes=3)
    lw = -math.exp(-0.5) * jax.nn.sigmoid(z)
    a = jax.nn.sigmoid(a0_ref[...] + _mm(tail, a2_ref[...], passes=3))
    ones = _head_ones(W_MIX, RWKV_N)
    kk = k * kk_ref[...]
    kk = kk / jnp.maximum(jnp.sqrt(_mm_exact_rhs(kk * kk, ones)), 1e-12)
    k = k * (1.0 + (a - 1.0) * ka_ref[...])
    aa, bl, kl, vl = -kk, kk * a, k, v
    if masked:
        live = lax.broadcasted_iota(jnp.int32, (n, 1), 0) % tc + j * tc < t_valid
        lw, aa, bl, kl, vl = (jnp.where(live, x, 0.0) for x in (lw, aa, bl, kl, vl))

    cum = _seg_cumsum(lw, chunk)
    gi = jnp.exp(-cum)
    a_t = aa * jnp.exp(cum - lw)
    b_t = bl * gi
    k_t = kl * gi
    r_t = r * jnp.exp(cum)
    nch = tc // chunk
    r2 = lax.broadcasted_iota(jnp.int32, (tc, tc), 0)
    c2 = lax.broadcasted_iota(jnp.int32, (tc, tc), 1)
    same = r2 // chunk == c2 // chunk
    strict = same & (r2 > c2)
    incl = same & (r2 >= c2)

    chains = [(i, h) for i in range(bb) for h in range(RWKV_HEADS)]
    nq = len(chains)
    sl = lambda x, i, h: x[i * tc:(i + 1) * tc, h * RWKV_N:(h + 1) * RWKV_N]
    ah = [sl(a_t, i, h) for i, h in chains]
    rh = [sl(r_t, i, h) for i, h in chains]
    vh = [sl(vl, i, h) for i, h in chains]
    pair = [_mm(jnp.concatenate([ah[q], rh[q]], axis=0),
                jnp.concatenate([sl(b_t, i, h), sl(k_t, i, h)], axis=0), NT) for q, (i, h) in enumerate(chains)]
    pw = [jnp.where(strict, m[0:tc, 0:tc], 0.0) for m in pair]
    m_rb = [jnp.where(incl, m[tc:2 * tc, 0:tc], 0.0) for m in pair]
    akrk = [jnp.concatenate([jnp.where(strict, m[0:tc, tc:2 * tc], 0.0),
                             jnp.where(incl, m[tc:2 * tc, tc:2 * tc], 0.0)], axis=0) for m in pair]
    both_v = [_mm(akrk[q], vh[q]) for q in range(nq)]
    sol = [jnp.concatenate([ah[q], both_v[q][0:tc]], axis=1) for q in range(nq)]
    y_v = [x[tc:2 * tc] for x in both_v]
    wide = sol[0].shape[1]
    d = 1
    while True:
        last = 2 * d >= chunk
        res = [_mm(pw[q], sol[q] if last else jnp.concatenate([sol[q], pw[q]], axis=1)) for q in range(nq)]
        sol = [sol[q] + res[q][:, 0:wide] for q in range(nq)]
        if last:
            break
        pw = [x[:, wide:wide + tc] for x in res]
        d *= 2
    lhs_c, p2_c, mix_c, yv_c, kv_c, gcol_c = [], [], [], [], [], []
    for c in range(nch):
        rs = slice(c * chunk, (c + 1) * chunk)
        ce = [sl(cum, i, h)[(c + 1) * chunk - 1:(c + 1) * chunk, :] for i, h in chains]
        dec = [jnp.exp(ce[q] - sl(cum, i, h)[rs]) for q, (i, h) in enumerate(chains)]
        lhs_c.append([jnp.concatenate([sol[q][rs, 0:RWKV_N], rh[q][rs]], axis=0) for q in range(nq)])
        p2_c.append([sol[q][rs, RWKV_N:2 * RWKV_N] for q in range(nq)])
        mix_c.append([jnp.concatenate([m_rb[q][rs, rs], (sl(bl, i, h)[rs] * dec[q]).T], axis=0)
                      for q, (i, h) in enumerate(chains)])
        yv_c.append([y_v[q][rs] for q in range(nq)])
        kv_c.append([_mm(sl(kl, i, h)[rs] * dec[q], vh[q][rs], TN) for q, (i, h) in enumerate(chains)])
        gcol_c.append([jnp.broadcast_to(jnp.exp(ce[q]), (RWKV_N, RWKV_N)).T for q in range(nq)])
    st = [st_ref[i, h] for i, h in chains]
    y_parts = [[] for _ in chains]
    for c in range(nch):
        both = [_mm(lhs_c[c][q], st[q]) for q in range(nq)]
        u_c = [both[q][0:chunk] + p2_c[c][q] for q in range(nq)]
        mix = [_mm(mix_c[c][q], u_c[q]) for q in range(nq)]
        st = [gcol_c[c][q] * st[q] + kv_c[c][q] + mix[q][chunk:chunk + RWKV_N] for q in range(nq)]
        for q in range(nq):
            y_parts[q].append(both[q][chunk:2 * chunk] + mix[q][0:chunk] + yv_c[c][q])
    for q, (i, h) in enumerate(chains):
        st_ref[i, h] = st[q]
    y = _cat([_cat([_cat(y_parts[i * RWKV_HEADS + h], 0) for h in range(RWKV_HEADS)], 1) for i in range(bb)], 0)
    mean = _mm_exact_rhs(y, ones) / RWKV_N
    yc = y - mean
    var = _mm_exact_rhs(yc * yc, ones) / RWKV_N
    out = yc * lax.rsqrt(var + RWKV_LN_EPS) * lw_ref[...] + lb_ref[...]
    bonus = _mm_exact_rhs(r * k * rk_ref[...], ones)
    y_ref[...] = (out + bonus * v).reshape(bb, tc, W_MIX).astype(y_ref.dtype)

    @pl.when(j == pl.num_programs(1) - 1)
    def _():
        for i in range(bb):
            for h in range(RWKV_HEADS):
                sn_ref[i, h] = st_ref[i, h].T


def _rwkv(zd, shift0, s0, layer, p, *, bb, tc, chunk, t_valid, y_dtype):
    b, t, _ = zd.shape
    vec = lambda n: pl.BlockSpec((1, n), lambda i, j: (0, 0))
    lora = pl.BlockSpec((LANES, W_MIX), lambda i, j: (0, 0))
    state = pl.BlockSpec((bb, RWKV_HEADS, RWKV_N, RWKV_N), lambda i, j: (i, 0, 0, 0))
    state_in = pl.BlockSpec((None, bb, RWKV_HEADS, RWKV_N, RWKV_N), lambda i, j: (layer, i, 0, 0, 0))
    return pl.pallas_call(
        functools.partial(_rwkv_body, bb=bb, tc=tc, chunk=chunk, t_valid=t_valid, masked=t_valid < t),
        grid=(b // bb, t // tc),
        in_specs=[
            pl.BlockSpec((bb, tc, ZD_W), lambda i, j: (i, j, 0)),
            pl.BlockSpec((bb, 1, ZD_W), lambda i, j: (i, 0, 0)),
            state_in,
            vec(ZD_W), vec(W_MIX), lora, vec(W_MIX), lora, vec(W_MIX), vec(W_MIX), vec(W_MIX),
            vec(W_MIX), vec(W_MIX),
        ],
        out_specs=[pl.BlockSpec((bb, tc, W_MIX), lambda i, j: (i, j, 0)), state],
        out_shape=[jax.ShapeDtypeStruct((b, t, W_MIX), y_dtype),
                   jax.ShapeDtypeStruct((b, RWKV_HEADS, RWKV_N, RWKV_N), f32)],
        scratch_shapes=[pltpu.VMEM((bb, RWKV_HEADS, RWKV_N, RWKV_N), f32), pltpu.VMEM((bb, 1, ZD_W), f32)],
        compiler_params=_params("parallel", "arbitrary"),
        name="rwkv7",
    )(zd, shift0, s0, p["mu"], p["w0"], p["w2"], p["a0"], p["a2"], p["k_k"], p["k_a"], p["r_k"],
      p["ln_w"], p["ln_b"])


def _block_diag(w):
    n, c, _ = w.shape
    out = jnp.zeros((n * c, n * c), w.dtype)
    for i in range(n):
        out = out.at[i * c:(i + 1) * c, i * c:(i + 1) * c].set(w[i])
    return out


def _rel_bucket_ids(tq):
    dist = WINDOW + np.arange(tq)[:, None] - np.arange(WINDOW + tq)[None, :]
    max_exact = REL_BUCKETS // 2
    n = np.maximum(dist, 0)
    log_ratio = (np.log(np.maximum(n, 1).astype(np.float32) / np.float32(max_exact))
                 / np.float32(math.log(REL_MAX_DIST / max_exact)))
    large = np.minimum(max_exact + (log_ratio * np.float32(REL_BUCKETS - max_exact)).astype(np.int32),
                       REL_BUCKETS - 1)
    bucket = np.where(n < max_exact, n, large)
    return np.where((dist >= 0) & (dist <= WINDOW), bucket, -1).astype(np.int32)


def _swa_bias(table, tq):
    ids = jnp.asarray(_rel_bucket_ids(tq))[None]
    out = jnp.full((SWA_HEADS, tq, WINDOW + tq), -jnp.inf, f32)
    for k in range(REL_BUCKETS):
        out = jnp.where(ids == k, table[k].astype(f32)[:, None, None], out)
    return out.reshape(SWA_HEADS * tq, WINDOW + tq)


def _pack_layer(i, w_in, lru_conv_w, lru_conv_b, lru_wa, lru_ba, lru_wx, lru_bx, lru_lambda, lru_norm,
                swa_sinks, swa_norm, gla_w_up, gla_b, gla_norm, rwkv_mu, rwkv_w0, rwkv_w2, rwkv_a0, rwkv_a2,
                rwkv_k_k, rwkv_k_a, rwkv_r_k, rwkv_ln_w, rwkv_ln_b, w_out, ple_proj, ple_gate,
                norm_pre, norm_post):
    w = w_in[i]
    c_ca = 5 * W_MIX
    c_ud = c_ca + GLA_RANK
    c_gate = c_ud + RWKV_SHIFT
    pad = jnp.zeros((D_MODEL, ZD_W - RWKV_SHIFT - GLA_RANK), w.dtype)
    w_packed = jnp.concatenate([w[:, :c_ca], w[:, c_ud:c_gate], w[:, c_ca:c_ud], pad, w[:, c_gate:]], axis=1)
    row = lambda a: a.reshape(1, -1)
    lora_lo = RWKV_SHIFT - 2 * RWKV_LORA - ZD_TAIL
    ca_lo = RWKV_SHIFT - ZD_TAIL
    zeros_tail = jnp.zeros((LANES, W_MIX), f32)
    return dict(
        w_in=w_packed.astype(bf16),
        norm_pre=row(norm_pre[i]), norm_post=row(norm_post[i]),
        w_out=w_out[i].astype(bf16), ple_proj=ple_proj[i].astype(bf16), ple_gate=ple_gate[i].astype(bf16),
        lru=dict(conv_w=lru_conv_w[i], conv_b=row(lru_conv_b[i]), wa=_block_diag(lru_wa[i]), ba=row(lru_ba[i]),
                 wx=_block_diag(lru_wx[i]), bx=row(lru_bx[i]), lam=row(lru_lambda[i]), norm=row(lru_norm[i])),
        swa_sinks=row(swa_sinks[i]), swa_norm=row(swa_norm[i]),
        gla=dict(w_up=jnp.zeros((LANES, GLA_HEADS * GLA_DK), f32).at[ca_lo:ca_lo + GLA_RANK].set(gla_w_up[i]),
                 b=row(gla_b[i]), norm=row(jnp.tile(gla_norm[i], GLA_HEADS))),
        rwkv=dict(mu=row(jnp.pad(rwkv_mu[i], (0, ZD_W - RWKV_SHIFT))), w0=row(rwkv_w0[i]),
                  w2=zeros_tail.at[lora_lo:lora_lo + RWKV_LORA].set(rwkv_w2[i]), a0=row(rwkv_a0[i]),
                  a2=zeros_tail.at[lora_lo + RWKV_LORA:lora_lo + 2 * RWKV_LORA].set(rwkv_a2[i]),
                  k_k=row(rwkv_k_k[i]), k_a=row(rwkv_k_a[i]), r_k=row(rwkv_r_k[i]),
                  ln_w=row(rwkv_ln_w[i]), ln_b=row(rwkv_ln_b[i])),
    )


PROMPT_TILES = dict(lru=(1, 256), swa=(4, WINDOW), gla=(4, 128, 64), rwkv=(4, 128, 64))


def _sample_tiles(t_pad):
    return dict(lru=(32, t_pad), swa=(16, t_pad), gla=(16, t_pad, t_pad), rwkv=(16, t_pad, t_pad))


def _layer(x, pe, layer, lp, bias, state, *, tiles, t_valid):
    b, t, _ = x.shape
    prompt = state is None
    rows = b * t
    act = bf16 if prompt else f32
    za, zb, zc, zd, zg = _inproj(x.reshape(rows, D_MODEL), lp["norm_pre"], lp["w_in"], act)
    za = za.reshape(b, t, ZA_W)
    zb = zb.reshape(b, t, ZB_W)
    zc = zc.reshape(b, t, ZC_W)
    zd = zd.reshape(b, t, ZD_W)
    if prompt:
        h0 = jnp.zeros((b, 1, W_MIX), f32)
        conv0 = jnp.zeros((b, SUBLANES, W_MIX), f32)
        kbuf = vbuf = zb[None]
        sg0 = jnp.zeros((1, b, GLA_HEADS * GLA_DK, GLA_DV), f32)
        sr0 = jnp.zeros((1, b, RWKV_HEADS, RWKV_N, RWKV_N), f32)
        sh0 = jnp.zeros((b, 1, ZD_W), f32)
        st_layer = 0
    else:
        h0, conv0, kbuf, vbuf, sg0, sr0, sh0 = state
        h0 = h0[layer].reshape(b, 1, W_MIX)
        conv0 = jnp.pad(conv0[layer], ((0, 0), (SUBLANES - (LRU_CONV - 1), 0), (0, 0)))
        sh0 = jnp.pad(sh0[layer], ((0, 0), (0, ZD_W - RWKV_SHIFT))).reshape(b, 1, ZD_W)
        st_layer = layer
    bb, tc = tiles["lru"]
    ya, h_new = _lru(za, conv0, h0, lp["lru"], bb=bb, tc=tc, t_valid=t_valid, reset_first=prompt, y_dtype=act)
    bb, tq = tiles["swa"]
    yb = _swa(zb, kbuf, vbuf, st_layer, bias, lp["swa_sinks"], lp["swa_norm"], bb=bb, tq=tq, own_prev=prompt,
              y_dtype=act)
    bb, tc, chunk = tiles["gla"]
    yc, sg_new = _gla(zc, zd, sg0, st_layer, lp["gla"], bb=bb, tc=tc, chunk=chunk, t_valid=t_valid, y_dtype=act)
    bb, tc, chunk = tiles["rwkv"]
    yd, sr_new = _rwkv(zd, sh0, sr0, st_layer, lp["rwkv"], bb=bb, tc=tc, chunk=chunk, t_valid=t_valid, y_dtype=act)
    flat = lambda a: a.reshape(rows, a.shape[-1])
    x_new = _outproj(flat(ya), flat(yb), flat(yc), flat(yd), zg, flat(x), pe.reshape(pe.shape[0], rows, D_PLE), layer,
                     lp["w_out"], lp["norm_post"], lp["ple_proj"], lp["ple_gate"]).reshape(b, t, D_MODEL)
    kv_rows = slice(t_valid - WINDOW, t_valid) if prompt else slice(0, t_valid)
    kw = SWA_KV_HEADS * SWA_HD
    k_new = zb[:, kv_rows, W_MIX:W_MIX + kw].reshape(b, -1, SWA_KV_HEADS, SWA_HD)
    v_new = zb[:, kv_rows, W_MIX + kw:W_MIX + 2 * kw].reshape(b, -1, SWA_KV_HEADS, SWA_HD)
    new_state = (h_new.reshape(b, W_MIX), za[:, t_valid - (LRU_CONV - 1):t_valid], k_new, v_new,
                 sg_new.reshape(b, GLA_HEADS, GLA_DK, GLA_DV), sr_new, zd[:, t_valid - 1, :RWKV_SHIFT])
    return x_new, new_state


def kernel(x_prompt, x_sample, state_lru_h, state_lru_conv, cache_swa_k, cache_swa_v, state_gla, state_rwkv, state_rwkv_shift, p_prompt, p_sample, rel_bias, norm_pre, norm_post, w_in, lru_conv_w, lru_conv_b, lru_wa, lru_ba, lru_wx, lru_bx, lru_lambda, lru_norm, swa_sinks, swa_norm, gla_w_up, gla_b, gla_norm, rwkv_mu, rwkv_w0, rwkv_w2, rwkv_a0, rwkv_a2, rwkv_k_k, rwkv_k_a, rwkv_r_k, rwkv_ln_w, rwkv_ln_b, w_out, ple_proj, ple_gate):
    depth = w_in.shape[0]
    t_p = x_prompt.shape[1]
    t_s = x_sample.shape[1]
    t_s_pad = -(-t_s // SUBLANES) * SUBLANES
    pad_t = lambda a, axis: jnp.pad(a, [(0, t_s_pad - t_s) if d == axis else (0, 0) for d in range(a.ndim)])
    xs = pad_t(x_sample, 1)
    ps = pad_t(p_sample, 2)
    xp = x_prompt
    bias_p = _swa_bias(rel_bias, WINDOW)
    bias_s = _swa_bias(rel_bias, t_s_pad)
    tiles_s = _sample_tiles(t_s_pad)
    n_l, b_s = cache_swa_k.shape[:2]
    kw = SWA_KV_HEADS * SWA_HD
    state_s = (state_lru_h, state_lru_conv, cache_swa_k.reshape(n_l, b_s, WINDOW, kw),
               cache_swa_v.reshape(n_l, b_s, WINDOW, kw),
               state_gla.reshape(n_l, b_s, GLA_HEADS * GLA_DK, GLA_DV), state_rwkv, state_rwkv_shift)
    sts_p, sts_s = [], []
    for i in range(depth):
        lp = _pack_layer(i, w_in, lru_conv_w, lru_conv_b, lru_wa, lru_ba, lru_wx, lru_bx, lru_lambda, lru_norm,
                         swa_sinks, swa_norm, gla_w_up, gla_b, gla_norm, rwkv_mu, rwkv_w0, rwkv_w2, rwkv_a0,
                         rwkv_a2, rwkv_k_k, rwkv_k_a, rwkv_r_k, rwkv_ln_w, rwkv_ln_b, w_out, ple_proj, ple_gate,
                         norm_pre, norm_post)
        xp, st_p = _layer(xp, p_prompt, i, lp, bias_p, None, tiles=PROMPT_TILES, t_valid=t_p)
        xs, st_s = _layer(xs, ps, i, lp, bias_s, state_s, tiles=tiles_s, t_valid=t_s)
        sts_p.append(st_p)
        sts_s.append(st_s)
    hp, cp, kp, vp, gp, rp, shp = [jnp.stack([s[j] for s in sts_p]) for j in range(7)]
    hs, cs, ks, vs, gs, rs, shs = [jnp.stack([s[j] for s in sts_s]) for j in range(7)]
    return (xp, xs[:, :t_s], hp, hs, cp, cs, kp, ks, vp, vs, gp, gs, rp, rs, shp, shs)
```

```python
import functools
import math

import jax
import jax.numpy as jnp
import numpy as np
from jax import lax
from jax.experimental import pallas as pl
from jax.experimental.pallas import tpu as pltpu

f32 = jnp.float32
bf16 = jnp.bfloat16

D_MODEL = 1024
D_PLE = 256
RMS_EPS = 1e-6
W_MIX = 256
LRU_BLOCKS = 4
LRU_CONV = 4
LRU_C = 8.0
SWA_HEADS = 4
SWA_KV_HEADS = 2
SWA_HD = 64
WINDOW = 128
REL_BUCKETS = 32
REL_MAX_DIST = 128
GLA_HEADS = 4
GLA_DK = 32
GLA_DV = 64
GLA_RANK = 16
GLA_TAU = 16.0
RWKV_HEADS = 4
RWKV_N = 64
RWKV_LORA = 32
RWKV_SHIFT = 3 * W_MIX + 2 * RWKV_LORA
RWKV_LN_EPS = 64e-5

ZA_W = 256
ZB_W = 512
ZC_W = 512
ZD_W = 896
ZG_W = 1024
ZD_TAIL = 768
N_IN_PACKED = ZA_W + ZB_W + ZC_W + ZD_W + ZG_W

LANES = 128
SUBLANES = 8
VMEM_LIMIT = 48 * 1024 * 1024

ROW_TILE = 512

NN = (((1,), (0,)), ((), ()))
NT = (((1,), (1,)), ((), ()))
TN = (((0,), (0,)), ((), ()))


def _dot(a, b, dn):
    return lax.dot_general(a, b, dn, preferred_element_type=f32)


def _split2(x):
    hi = x.astype(bf16)
    lo = (x - hi.astype(f32)).astype(bf16)
    return hi, lo


def _mm(a, b, dn=NN, passes=1):
    if passes == 1:
        return _dot(a.astype(bf16), b.astype(bf16), dn)
    ah, al = _split2(a)
    bh, bl = _split2(b)
    return _dot(ah, bh, dn) + (_dot(ah, bl, dn) + _dot(al, bh, dn))


def _mm_exact_rhs(a, b_bf16):
    hi, lo = _split2(a)
    return _dot(hi, b_bf16, NN) + _dot(lo, b_bf16, NN)


def _head_ones(n, width):
    r = lax.broadcasted_iota(jnp.int32, (n, n), 0) // width
    c = lax.broadcasted_iota(jnp.int32, (n, n), 1) // width
    return jnp.where(r == c, 1.0, 0.0).astype(bf16)


def _rms(x, g):
    return x * lax.rsqrt(jnp.mean(x * x, axis=-1, keepdims=True) + RMS_EPS) * g


def _seg_cumsum(x, seg):
    n, w = x.shape
    x3 = x.reshape(n // SUBLANES, SUBLANES, w)
    sub = lax.broadcasted_iota(jnp.int32, (1, SUBLANES, w), 1)
    d = 1
    while d < SUBLANES:
        x3 = x3 + jnp.where(sub >= d, pltpu.roll(x3, d, 1), 0.0)
        d *= 2
    x = x3.reshape(n, w)
    rows = lax.broadcasted_iota(jnp.int32, (n, 1), 0) % seg
    tot = jnp.broadcast_to(x3[:, SUBLANES - 1:SUBLANES, :], x3.shape).reshape(n, w)
    while d < seg:
        add = jnp.where(rows >= d, pltpu.roll(tot, d, 0), 0.0)
        x = x + add
        tot = tot + add
        d *= 2
    return x


def _cat(xs, axis):
    return xs[0] if len(xs) == 1 else jnp.concatenate(xs, axis=axis)


def _params(*sem):
    return pltpu.CompilerParams(dimension_semantics=sem, vmem_limit_bytes=VMEM_LIMIT)


def _inproj_body(x_ref, g_ref, w_ref, za_ref, zb_ref, zc_ref, zd_ref, zg_ref):
    h = _rms(x_ref[...], g_ref[...]).astype(bf16)
    off = 0
    for o_ref in (za_ref, zb_ref, zc_ref, zd_ref, zg_ref):
        n = o_ref.shape[-1]
        o_ref[...] = _dot(h, w_ref[:, off:off + n], NN).astype(o_ref.dtype)
        off += n


def _inproj(x, g, w, gate_dtype):
    rows = x.shape[0]
    tm = min(ROW_TILE, rows)
    widths = (ZA_W, ZB_W, ZC_W, ZD_W, ZG_W)
    return pl.pallas_call(
        _inproj_body,
        grid=(rows // tm,),
        in_specs=[
            pl.BlockSpec((tm, D_MODEL), lambda i: (i, 0)),
            pl.BlockSpec((1, D_MODEL), lambda i: (0, 0)),
            pl.BlockSpec((D_MODEL, N_IN_PACKED), lambda i: (0, 0)),
        ],
        out_specs=[pl.BlockSpec((tm, n), lambda i: (i, 0)) for n in widths],
        out_shape=[jax.ShapeDtypeStruct((rows, n), gate_dtype if n == ZG_W else f32) for n in widths],
        compiler_params=_params("parallel"),
        name="in_proj",
    )(x, g, w)


def _outproj_body(ya_ref, yb_ref, yc_ref, yd_ref, zg_ref, x_ref, pe_ref, wo_ref, gp_ref, wp_ref, wg_ref, o_ref):
    g = zg_ref[...].astype(f32)
    sg = g * jax.nn.sigmoid(g)
    acc = None
    for i, y_ref in enumerate((ya_ref, yb_ref, yc_ref, yd_ref)):
        lo = i * W_MIX
        y = (y_ref[...].astype(f32) * sg[:, lo:lo + W_MIX]).astype(bf16)
        part = _dot(y, wo_ref[lo:lo + W_MIX, :], NN)
        acc = part if acc is None else acc + part
    x1 = x_ref[...] + _rms(acc, gp_ref[...])
    emb = _dot(pe_ref[...].astype(bf16), wp_ref[...], NN)
    gate = jax.nn.sigmoid(_dot(x1.astype(bf16), wg_ref[...], NN))
    o_ref[...] = x1 + emb * gate


def _outproj(ya, yb, yc, yd, zg, x, pe, layer, wo, gp, wp, wg):
    rows = x.shape[0]
    tm = min(ROW_TILE, rows)
    row = lambda n: pl.BlockSpec((tm, n), lambda i: (i, 0))
    full = lambda a: pl.BlockSpec(a.shape, lambda i: (0, 0))
    return pl.pallas_call(
        _outproj_body,
        grid=(rows // tm,),
        in_specs=[row(W_MIX), row(W_MIX), row(W_MIX), row(W_MIX), row(ZG_W), row(D_MODEL),
                  pl.BlockSpec((None, tm, D_PLE), lambda i: (layer, i, 0)),
                  full(wo), full(gp), full(wp), full(wg)],
        out_specs=row(D_MODEL),
        out_shape=jax.ShapeDtypeStruct((rows, D_MODEL), f32),
        compiler_params=_params("parallel"),
        name="out_proj",
    )(ya, yb, yc, yd, zg, x, pe, wo, gp, wp, wg)


def _lru_body(x_ref, c0_ref, h0_ref, cw_ref, cb_ref, wa_ref, ba_ref, wx_ref, bx_ref, lam_ref, gn_ref,
              y_ref, hl_ref, ext_ref, hc_ref, *, bb, tc, t_valid, reset_first):
    j = pl.program_id(1)

    @pl.when(j == 0)
    def _():
        ext_ref[:, 0:SUBLANES, :] = c0_ref[...]
        hc_ref[...] = h0_ref[...]

    @pl.when(j > 0)
    def _():
        ext_ref[:, 0:SUBLANES, :] = ext_ref[:, tc:tc + SUBLANES, :]

    n = bb * tc
    ext_ref[:, SUBLANES:SUBLANES + tc, :] = x_ref[...]
    xc = cw_ref[3:4, :] * x_ref[...].reshape(n, W_MIX) + cb_ref[...]
    for k in range(LRU_CONV - 1):
        s = SUBLANES - (LRU_CONV - 1) + k
        xc = xc + cw_ref[k:k + 1, :] * _cat([ext_ref[i, s:s + tc, :] for i in range(bb)], 0)
    gate_r = _mm(xc, wa_ref[...]) + ba_ref[...]
    gate_i = _mm(xc, wx_ref[...]) + bx_ref[...]
    log_a = -LRU_C * jax.nn.sigmoid(gate_r) * jax.nn.softplus(-lam_ref[...])
    a = jnp.exp(log_a)
    mult = jnp.sqrt(1.0 - a * a)
    if reset_first:
        rows = lax.broadcasted_iota(jnp.int32, (n, W_MIX), 0)
        mult = jnp.where(rows % tc + j * tc == 0, 1.0, mult)
    b = mult * jax.nn.sigmoid(gate_i) * xc
    ng = tc // SUBLANES
    a = a.reshape(bb * ng, SUBLANES, W_MIX)
    b = b.reshape(bb * ng, SUBLANES, W_MIX)
    sub = lax.broadcasted_iota(jnp.int32, (1, SUBLANES, W_MIX), 1)
    d = 1
    while d < SUBLANES:
        m = sub >= d
        b = a * jnp.where(m, pltpu.roll(b, d, 1), 0.0) + b
        a = a * jnp.where(m, pltpu.roll(a, d, 1), 1.0)
        d *= 2
    hs = []
    for i in range(bb):
        carry = jnp.broadcast_to(hc_ref[i], (SUBLANES, W_MIX))
        for g in range(ng):
            hg = a[i * ng + g] * carry + b[i * ng + g]
            hs.append(hg)
            carry = jnp.broadcast_to(hg[SUBLANES - 1:SUBLANES], (SUBLANES, W_MIX))
        hc_ref[i] = carry[0:1]
    h = _cat(hs, 0)
    y_ref[...] = _rms(h, gn_ref[...]).reshape(bb, tc, W_MIX).astype(y_ref.dtype)
    r_last = (t_valid - 1) % tc

    @pl.when(j == (t_valid - 1) // tc)
    def _():
        for i in range(bb):
            hl_ref[i] = h[i * tc + r_last:i * tc + r_last + 1, :]


def _lru(za, conv0, h0, p, *, bb, tc, t_valid, reset_first, y_dtype):
    b, t, _ = za.shape
    vec = pl.BlockSpec((1, W_MIX), lambda i, j: (0, 0))
    mat = pl.BlockSpec((W_MIX, W_MIX), lambda i, j: (0, 0))
    return pl.pallas_call(
        functools.partial(_lru_body, bb=bb, tc=tc, t_valid=t_valid, reset_first=reset_first),
        grid=(b // bb, t // tc),
        in_specs=[
            pl.BlockSpec((bb, tc, W_MIX), lambda i, j: (i, j, 0)),
            pl.BlockSpec((bb, SUBLANES, W_MIX), lambda i, j: (i, 0, 0)),
            pl.BlockSpec((bb, 1, W_MIX), lambda i, j: (i, 0, 0)),
            pl.BlockSpec((LRU_CONV, W_MIX), lambda i, j: (0, 0)),
            vec, mat, vec, mat, vec, vec, vec,
        ],
        out_specs=[
            pl.BlockSpec((bb, tc, W_MIX), lambda i, j: (i, j, 0)),
            pl.BlockSpec((bb, 1, W_MIX), lambda i, j: (i, 0, 0)),
        ],
        out_shape=[jax.ShapeDtypeStruct((b, t, W_MIX), y_dtype), jax.ShapeDtypeStruct((b, 1, W_MIX), f32)],
        scratch_shapes=[pltpu.VMEM((bb, tc + SUBLANES, W_MIX), f32), pltpu.VMEM((bb, 1, W_MIX), f32)],
        compiler_params=_params("parallel", "arbitrary"),
        name="rglru",
    )(za, conv0, h0, p["conv_w"], p["conv_b"], p["wa"], p["ba"], p["wx"], p["bx"], p["lam"], p["norm"])


def _swa_body(q_ref, ko_ref, vo_ref, kp_ref, vp_ref, bias_ref, sink_ref, gn_ref, y_ref, *, bb, tq, mask_first):
    n = pl.program_id(1)
    nk = WINDOW + tq
    kw = SWA_KV_HEADS * SWA_HD
    lo = lax.broadcasted_iota(jnp.int32, (1, kw), 1) < SWA_HD
    row_head = lax.broadcasted_iota(jnp.int32, (SWA_HEADS * tq, 1), 0) // tq
    sink = jnp.zeros((SWA_HEADS * tq, 1), f32)
    for h in range(SWA_HEADS):
        sink = jnp.where(row_head == h, sink_ref[:, h:h + 1], sink)
    bias = bias_ref[...]
    if mask_first:
        col = lax.broadcasted_iota(jnp.int32, (1, nk), 1)
        bias = jnp.where((n > 0) | (col >= WINDOW), bias, -jnp.inf)
    swap = lambda x: pltpu.roll(x, SWA_HD, 1)
    qs = []
    for i in range(bb):
        q = q_ref[i] * (SWA_HD ** -0.5)
        q01, q23 = q[:, 0:kw], q[:, kw:2 * kw]
        qs.append(jnp.concatenate([jnp.where(lo, q01, 0.0), jnp.where(lo, swap(q01), 0.0),
                                   jnp.where(lo, 0.0, swap(q23)), jnp.where(lo, 0.0, q23)], axis=0))
    s = [_mm(qs[i], jnp.concatenate([kp_ref[i], ko_ref[i]], axis=0), NT) + bias for i in range(bb)]
    m = [jnp.maximum(jnp.max(x, axis=-1, keepdims=True), sink) for x in s]
    e = [jnp.exp(x - mx) for x, mx in zip(s, m)]
    den = [jnp.sum(x, axis=-1, keepdims=True) + jnp.exp(sink - mx) for x, mx in zip(e, m)]
    pv = [_mm(e[i], jnp.concatenate([vp_ref[i], vo_ref[i]], axis=0)) / den[i] for i in range(bb)]
    for i in range(bb):
        p = pv[i]
        o01 = jnp.where(lo, p[0:tq], swap(p[tq:2 * tq]))
        o23 = jnp.where(lo, swap(p[2 * tq:3 * tq]), p[3 * tq:4 * tq])
        y_ref[i] = _rms(jnp.concatenate([o01, o23], axis=1), gn_ref[...]).astype(y_ref.dtype)


def _swa(zb, k_prev, v_prev, layer, bias, sinks, gn, *, bb, tq, own_prev, y_dtype):
    b, t, _ = zb.shape
    kw = SWA_KV_HEADS * SWA_HD
    if own_prev:
        kp_spec = pl.BlockSpec((None, bb, WINDOW, kw), lambda i, n: (0, i, jnp.maximum(n - 1, 0), 2))
        vp_spec = pl.BlockSpec((None, bb, WINDOW, kw), lambda i, n: (0, i, jnp.maximum(n - 1, 0), 3))
    else:
        kp_spec = pl.BlockSpec((None, bb, WINDOW, kw), lambda i, n: (layer, i, 0, 0))
        vp_spec = pl.BlockSpec((None, bb, WINDOW, kw), lambda i, n: (layer, i, 0, 0))
    return pl.pallas_call(
        functools.partial(_swa_body, bb=bb, tq=tq, mask_first=own_prev),
        grid=(b // bb, t // tq),
        in_specs=[
            pl.BlockSpec((bb, tq, W_MIX), lambda i, n: (i, n, 0)),
            pl.BlockSpec((bb, tq, kw), lambda i, n: (i, n, 2)),
            pl.BlockSpec((bb, tq, kw), lambda i, n: (i, n, 3)),
            kp_spec, vp_spec,
            pl.BlockSpec(bias.shape, lambda i, n: (0, 0)),
            pl.BlockSpec((1, SWA_HEADS), lambda i, n: (0, 0)),
            pl.BlockSpec((1, W_MIX), lambda i, n: (0, 0)),
        ],
        out_specs=pl.BlockSpec((bb, tq, W_MIX), lambda i, n: (i, n, 0)),
        out_shape=jax.ShapeDtypeStruct((b, t, W_MIX), y_dtype),
        compiler_params=_params("parallel", "arbitrary"),
        name="swa",
    )(zb, zb, zb, k_prev, v_prev, bias, sinks, gn)


def _col_bcast(row):
    n = row.shape[-1]
    return jnp.broadcast_to(row, (n, n)).T


def _gla_body(q_ref, k_ref, v_ref, t_ref, s0_ref, wup_ref, gb_ref, gn_ref, acc_ref, y_ref, sn_ref, st_ref,
              *, bb, tc, chunk, t_valid, masked):
    del acc_ref
    j = pl.program_id(1)
    n = bb * tc
    qk = GLA_HEADS * GLA_DK
    sr = lax.broadcasted_iota(jnp.int32, (qk, W_MIX), 0) // GLA_DK
    sc = lax.broadcasted_iota(jnp.int32, (qk, W_MIX), 1) // GLA_DV
    diag = sr == sc

    @pl.when(j == 0)
    def _():
        for i in range(bb):
            x = s0_ref[i]
            st_ref[i] = jnp.where(diag, jnp.concatenate([x] * GLA_HEADS, axis=1), 0.0)

    la = jax.nn.log_sigmoid(_mm(t_ref[...].reshape(n, LANES), wup_ref[...], passes=3) + gb_ref[...]) / GLA_TAU
    k = k_ref[...].reshape(n, qk)
    v = v_ref[...].reshape(n, W_MIX)
    if masked:
        live = lax.broadcasted_iota(jnp.int32, (n, 1), 0) % tc + j * tc < t_valid
        la, k, v = (jnp.where(live, x, 0.0) for x in (la, k, v))
    bc = _seg_cumsum(la, chunk)
    q_dec = q_ref[...].reshape(n, qk) * (GLA_DK ** -0.5) * jnp.exp(bc)
    k_inv = k * jnp.exp(-bc)
    khead = lax.broadcasted_iota(jnp.int32, (1, qk), 1) // GLA_DK
    vhead = lax.broadcasted_iota(jnp.int32, (1, W_MIX), 1) // GLA_DV
    r2 = lax.broadcasted_iota(jnp.int32, (GLA_HEADS * tc, tc), 0) % tc
    c2 = lax.broadcasted_iota(jnp.int32, (GLA_HEADS * tc, tc), 1)
    causal = (r2 // chunk == c2 // chunk) & (r2 >= c2)
    nch = tc // chunk
    ts = [slice(i * tc, (i + 1) * tc) for i in range(bb)]
    qs = [jnp.concatenate([jnp.where(khead == h, q_dec[t], 0.0) for h in range(GLA_HEADS)], axis=0) for t in ts]
    att = [jnp.where(causal, _mm(qs[i], k_inv[ts[i]], NT), 0.0) for i in range(bb)]
    full = [_mm(att[i], v[ts[i]]) for i in range(bb)]
    kv = []
    for c in range(nch):
        kv_c = []
        for i in range(bb):
            rs = slice(i * tc + c * chunk, i * tc + (c + 1) * chunk)
            b_end = bc[i * tc + (c + 1) * chunk - 1:i * tc + (c + 1) * chunk, :]
            k_end = k[rs] * jnp.exp(b_end - bc[rs])
            dec = _col_bcast(jnp.exp(b_end))
            kv_c.append((jnp.concatenate([dec] * (W_MIX // qk), axis=1),
                         jnp.where(diag, _mm(k_end, v[rs], TN), 0.0)))
        kv.append(kv_c)
    st = [st_ref[i] for i in range(bb)]
    inter = [[] for _ in range(bb)]
    for c in range(nch):
        for i in range(bb):
            inter[i].append(_mm(q_dec[i * tc + c * chunk:i * tc + (c + 1) * chunk], st[i]))
            st[i] = st[i] * kv[c][i][0] + kv[c][i][1]
    outs = []
    for i in range(bb):
        st_ref[i] = st[i]
        o = jnp.where(vhead == 0, full[i][0:tc], 0.0)
        for h in range(1, GLA_HEADS):
            o = o + jnp.where(vhead == h, full[i][h * tc:(h + 1) * tc], 0.0)
        outs.append(o + _cat(inter[i], 0))
    o = _cat(outs, 0)
    ms = _mm_exact_rhs(o * o, _head_ones(W_MIX, GLA_DV)) / GLA_DV
    y_ref[...] = (o * lax.rsqrt(ms + RMS_EPS) * gn_ref[...]).reshape(bb, tc, W_MIX).astype(y_ref.dtype)

    @pl.when(j == pl.num_programs(1) - 1)
    def _():
        for i in range(bb):
            x = st_ref[i]
            sn_ref[i] = sum(x[:, h * GLA_DV:(h + 1) * GLA_DV] for h in range(1, GLA_HEADS)) + x[:, 0:GLA_DV]


def _gla(zc, zd, s0, layer, acc, layer_out, p, *, bb, tc, chunk, t_valid, y_dtype):
    b, t, _ = zc.shape
    qk = GLA_HEADS * GLA_DK
    state_in = pl.BlockSpec((None, bb, qk, GLA_DV), lambda i, j: (layer, i, 0, 0))
    state_out = pl.BlockSpec((None, bb, qk, GLA_DV), lambda i, j: (layer_out, i, 0, 0))
    return pl.pallas_call(
        functools.partial(_gla_body, bb=bb, tc=tc, chunk=chunk, t_valid=t_valid, masked=t_valid < t),
        grid=(b // bb, t // tc),
        in_specs=[
            pl.BlockSpec((bb, tc, qk), lambda i, j: (i, j, 0)),
            pl.BlockSpec((bb, tc, qk), lambda i, j: (i, j, 1)),
            pl.BlockSpec((bb, tc, W_MIX), lambda i, j: (i, j, 1)),
            pl.BlockSpec((bb, tc, LANES), lambda i, j: (i, j, ZD_TAIL // LANES)),
            state_in,
            pl.BlockSpec((LANES, qk), lambda i, j: (0, 0)),
            pl.BlockSpec((1, qk), lambda i, j: (0, 0)),
            pl.BlockSpec((1, W_MIX), lambda i, j: (0, 0)),
            pl.BlockSpec(memory_space=pl.ANY),
        ],
        out_specs=[pl.BlockSpec((bb, tc, W_MIX), lambda i, j: (i, j, 0)), state_out],
        out_shape=[jax.ShapeDtypeStruct((b, t, W_MIX), y_dtype), jax.ShapeDtypeStruct(acc.shape, f32)],
        input_output_aliases={8: 1},
        scratch_shapes=[pltpu.VMEM((bb, qk, W_MIX), f32)],
        compiler_params=_params("parallel", "arbitrary"),
        name="gla",
    )(zc, zc, zc, zd, s0, p["w_up"], p["b"], p["norm"], acc)


def _rwkv_body(u_ref, sh0_ref, s0_ref, mu_ref, w0_ref, w2_ref, a0_ref, a2_ref, kk_ref, ka_ref, rk_ref,
               lw_ref, lb_ref, acc_ref, y_ref, sn_ref, st_ref, up_ref, *, bb, tc, chunk, t_valid, masked):
    del acc_ref
    j = pl.program_id(1)

    @pl.when(j == 0)
    def _():
        for i in range(bb):
            for h in range(RWKV_HEADS):
                st_ref[i, h] = s0_ref[i, h].T
        up_ref[...] = sh0_ref[...]

    n = bb * tc
    u = u_ref[...].reshape(n, ZD_W)
    rolled = pltpu.roll(u, 1, 0)
    first = lax.broadcasted_iota(jnp.int32, (SUBLANES, 1), 0) == 0
    parts = []
    for i in range(bb):
        parts.append(jnp.where(first, up_ref[i], rolled[i * tc:i * tc + SUBLANES]))
        if tc > SUBLANES:
            parts.append(rolled[i * tc + SUBLANES:(i + 1) * tc])
        up_ref[i] = u[(i + 1) * tc - 1:(i + 1) * tc, :]
    u_prev = _cat(parts, 0)
    us = u + (u_prev - u) * mu_ref[...]
    r = us[:, 0:W_MIX]
    k = us[:, W_MIX:2 * W_MIX]
    v = us[:, 2 * W_MIX:3 * W_MIX]
    tail = us[:, ZD_TAIL:ZD_TAIL + LANES]
    z = w0_ref[...] + _mm(jnp.tanh(tail), w2_ref[...], passes=3)
    lw = -math.exp(-0.5) * jax.nn.sigmoid(z)
    a = jax.nn.sigmoid(a0_ref[...] + _mm(tail, a2_ref[...], passes=3))
    ones = _head_ones(W_MIX, RWKV_N)
    kk = k * kk_ref[...]
    kk = kk / jnp.maximum(jnp.sqrt(_mm_exact_rhs(kk * kk, ones)), 1e-12)
    k = k * (1.0 + (a - 1.0) * ka_ref[...])
    aa, bl, kl, vl = -kk, kk * a, k, v
    if masked:
        live = lax.broadcasted_iota(jnp.int32, (n, 1), 0) % tc + j * tc < t_valid
        lw, aa, bl, kl, vl = (jnp.where(live, x, 0.0) for x in (lw, aa, bl, kl, vl))

    cum = _seg_cumsum(lw, chunk)
    gi = jnp.exp(-cum)
    a_t = aa * jnp.exp(cum - lw)
    b_t = bl * gi
    k_t = kl * gi
    r_t = r * jnp.exp(cum)
    nch = tc // chunk
    r2 = lax.broadcasted_iota(jnp.int32, (tc, tc), 0)
    c2 = lax.broadcasted_iota(jnp.int32, (tc, tc), 1)
    same = r2 // chunk == c2 // chunk
    strict = same & (r2 > c2)
    incl = same & (r2 >= c2)

    chains = [(i, h) for i in range(bb) for h in range(RWKV_HEADS)]
    nq = len(chains)
    sl = lambda x, i, h: x[i * tc:(i + 1) * tc, h * RWKV_N:(h + 1) * RWKV_N]
    ah = [sl(a_t, i, h) for i, h in chains]
    rh = [sl(r_t, i, h) for i, h in chains]
    vh = [sl(vl, i, h) for i, h in chains]
    pair = [_mm(jnp.concatenate([ah[q], rh[q]], axis=0),
                jnp.concatenate([sl(b_t, i, h), sl(k_t, i, h)], axis=0), NT) for q, (i, h) in enumerate(chains)]
    pw = [jnp.where(strict, m[0:tc, 0:tc], 0.0) for m in pair]
    m_rb = [jnp.where(incl, m[tc:2 * tc, 0:tc], 0.0) for m in pair]
    akrk = [jnp.concatenate([jnp.where(strict, m[0:tc, tc:2 * tc], 0.0),
                             jnp.where(incl, m[tc:2 * tc, tc:2 * tc], 0.0)], axis=0) for m in pair]
    both_v = [_mm(akrk[q], vh[q]) for q in range(nq)]
    sol = [jnp.concatenate([ah[q], both_v[q][0:tc]], axis=1) for q in range(nq)]
    y_v = [x[tc:2 * tc] for x in both_v]
    wide = sol[0].shape[1]
    d = 1
    while True:
        last = 2 * d >= chunk
        res = [_mm(pw[q], sol[q] if last else jnp.concatenate([sol[q], pw[q]], axis=1)) for q in range(nq)]
        sol = [sol[q] + res[q][:, 0:wide] for q in range(nq)]
        if last:
            break
        pw = [x[:, wide:wide + tc] for x in res]
        d *= 2
    lhs_c, p2_c, mix_c, yv_c, kv_c, gcol_c = [], [], [], [], [], []
    for c in range(nch):
        rs = slice(c * chunk, (c + 1) * chunk)
        ce = [sl(cum, i, h)[(c + 1) * chunk - 1:(c + 1) * chunk, :] for i, h in chains]
        dec = [jnp.exp(ce[q] - sl(cum, i, h)[rs]) for q, (i, h) in enumerate(chains)]
        lhs_c.append([jnp.concatenate([sol[q][rs, 0:RWKV_N], rh[q][rs]], axis=0) for q in range(nq)])
        p2_c.append([sol[q][rs, RWKV_N:2 * RWKV_N] for q in range(nq)])
        mix_c.append([jnp.concatenate([m_rb[q][rs, rs], (sl(bl, i, h)[rs] * dec[q]).T], axis=0)
                      for q, (i, h) in enumerate(chains)])
        yv_c.append([y_v[q][rs] for q in range(nq)])
        kv_c.append([_mm(sl(kl, i, h)[rs] * dec[q], vh[q][rs], TN) for q, (i, h) in enumerate(chains)])
        gcol_c.append([jnp.broadcast_to(jnp.exp(ce[q]), (RWKV_N, RWKV_N)).T for q in range(nq)])
    st = [st_ref[i, h] for i, h in chains]
    y_parts = [[] for _ in chains]
    for c in range(nch):
        both = [_mm(lhs_c[c][q], st[q]) for q in range(nq)]
        u_c = [both[q][0:chunk] + p2_c[c][q] for q in range(nq)]
        mix = [_mm(mix_c[c][q], u_c[q]) for q in range(nq)]
        st = [gcol_c[c][q] * st[q] + kv_c[c][q] + mix[q][chunk:chunk + RWKV_N] for q in range(nq)]
        for q in range(nq):
            y_parts[q].append(both[q][chunk:2 * chunk] + mix[q][0:chunk] + yv_c[c][q])
    for q, (i, h) in enumerate(chains):
        st_ref[i, h] = st[q]
    y = _cat([_cat([_cat(y_parts[i * RWKV_HEADS + h], 0) for h in range(RWKV_HEADS)], 1) for i in range(bb)], 0)
    mean = _mm_exact_rhs(y, ones) / RWKV_N
    yc = y - mean
    var = _mm_exact_rhs(yc * yc, ones) / RWKV_N
    out = yc * lax.rsqrt(var + RWKV_LN_EPS) * lw_ref[...] + lb_ref[...]
    bonus = _mm_exact_rhs(r * k * rk_ref[...], ones)
    y_ref[...] = (out + bonus * v).reshape(bb, tc, W_MIX).astype(y_ref.dtype)

    @pl.when(j == pl.num_programs(1) - 1)
    def _():
        for i in range(bb):
            for h in range(RWKV_HEADS):
                sn_ref[i, h] = st_ref[i, h].T


def _rwkv(zd, shift0, s0, layer, acc, layer_out, p, *, bb, tc, chunk, t_valid, y_dtype):
    b, t, _ = zd.shape
    vec = lambda n: pl.BlockSpec((1, n), lambda i, j: (0, 0))
    lora = pl.BlockSpec((LANES, W_MIX), lambda i, j: (0, 0))
    state_in = pl.BlockSpec((None, bb, RWKV_HEADS, RWKV_N, RWKV_N), lambda i, j: (layer, i, 0, 0, 0))
    state_out = pl.BlockSpec((None, bb, RWKV_HEADS, RWKV_N, RWKV_N), lambda i, j: (layer_out, i, 0, 0, 0))
    return pl.pallas_call(
        functools.partial(_rwkv_body, bb=bb, tc=tc, chunk=chunk, t_valid=t_valid, masked=t_valid < t),
        grid=(b // bb, t // tc),
        in_specs=[
            pl.BlockSpec((bb, tc, ZD_W), lambda i, j: (i, j, 0)),
            pl.BlockSpec((bb, 1, ZD_W), lambda i, j: (i, 0, 0)),
            state_in,
            vec(ZD_W), vec(W_MIX), lora, vec(W_MIX), lora, vec(W_MIX), vec(W_MIX), vec(W_MIX),
            vec(W_MIX), vec(W_MIX),
            pl.BlockSpec(memory_space=pl.ANY),
        ],
        out_specs=[pl.BlockSpec((bb, tc, W_MIX), lambda i, j: (i, j, 0)), state_out],
        out_shape=[jax.ShapeDtypeStruct((b, t, W_MIX), y_dtype), jax.ShapeDtypeStruct(acc.shape, f32)],
        input_output_aliases={13: 1},
        scratch_shapes=[pltpu.VMEM((bb, RWKV_HEADS, RWKV_N, RWKV_N), f32), pltpu.VMEM((bb, 1, ZD_W), f32)],
        compiler_params=_params("parallel", "arbitrary"),
        name="rwkv7",
    )(zd, shift0, s0, p["mu"], p["w0"], p["w2"], p["a0"], p["a2"], p["k_k"], p["k_a"], p["r_k"],
      p["ln_w"], p["ln_b"], acc)


def _block_diag(w):
    n, c, _ = w.shape
    out = jnp.zeros((n * c, n * c), w.dtype)
    for i in range(n):
        out = out.at[i * c:(i + 1) * c, i * c:(i + 1) * c].set(w[i])
    return out


def _rel_bucket_ids(tq):
    dist = WINDOW + np.arange(tq)[:, None] - np.arange(WINDOW + tq)[None, :]
    max_exact = REL_BUCKETS // 2
    n = np.maximum(dist, 0)
    log_ratio = (np.log(np.maximum(n, 1).astype(np.float32) / np.float32(max_exact))
                 / np.float32(math.log(REL_MAX_DIST / max_exact)))
    large = np.minimum(max_exact + (log_ratio * np.float32(REL_BUCKETS - max_exact)).astype(np.int32),
                       REL_BUCKETS - 1)
    bucket = np.where(n < max_exact, n, large)
    return np.where((dist >= 0) & (dist <= WINDOW), bucket, -1).astype(np.int32)


def _swa_bias(table, tq):
    ids = jnp.asarray(_rel_bucket_ids(tq))[None]
    out = jnp.full((SWA_HEADS, tq, WINDOW + tq), -jnp.inf, f32)
    for k in range(REL_BUCKETS):
        out = jnp.where(ids == k, table[k].astype(f32)[:, None, None], out)
    return out.reshape(SWA_HEADS * tq, WINDOW + tq)


def _pack_layer(i, w_in, lru_conv_w, lru_conv_b, lru_wa, lru_ba, lru_wx, lru_bx, lru_lambda, lru_norm,
                swa_sinks, swa_norm, gla_w_up, gla_b, gla_norm, rwkv_mu, rwkv_w0, rwkv_w2, rwkv_a0, rwkv_a2,
                rwkv_k_k, rwkv_k_a, rwkv_r_k, rwkv_ln_w, rwkv_ln_b, w_out, ple_proj, ple_gate,
                norm_pre, norm_post):
    w = w_in[i]
    c_ca = 5 * W_MIX
    c_ud = c_ca + GLA_RANK
    c_gate = c_ud + RWKV_SHIFT
    pad = jnp.zeros((D_MODEL, ZD_W - RWKV_SHIFT - GLA_RANK), w.dtype)
    w_packed = jnp.concatenate([w[:, :c_ca], w[:, c_ud:c_gate], w[:, c_ca:c_ud], pad, w[:, c_gate:]], axis=1)
    row = lambda a: a.reshape(1, -1)
    lora_lo = RWKV_SHIFT - 2 * RWKV_LORA - ZD_TAIL
    ca_lo = RWKV_SHIFT - ZD_TAIL
    zeros_tail = jnp.zeros((LANES, W_MIX), f32)
    return dict(
        w_in=w_packed.astype(bf16),
        norm_pre=row(norm_pre[i]), norm_post=row(norm_post[i]),
        w_out=w_out[i].astype(bf16), ple_proj=ple_proj[i].astype(bf16), ple_gate=ple_gate[i].astype(bf16),
        lru=dict(conv_w=lru_conv_w[i], conv_b=row(lru_conv_b[i]), wa=_block_diag(lru_wa[i]), ba=row(lru_ba[i]),
                 wx=_block_diag(lru_wx[i]), bx=row(lru_bx[i]), lam=row(lru_lambda[i]), norm=row(lru_norm[i])),
        swa_sinks=row(swa_sinks[i]), swa_norm=row(swa_norm[i]),
        gla=dict(w_up=jnp.zeros((LANES, GLA_HEADS * GLA_DK), f32).at[ca_lo:ca_lo + GLA_RANK].set(gla_w_up[i]),
                 b=row(gla_b[i]), norm=row(jnp.tile(gla_norm[i], GLA_HEADS))),
        rwkv=dict(mu=row(jnp.pad(rwkv_mu[i], (0, ZD_W - RWKV_SHIFT))), w0=row(rwkv_w0[i]),
                  w2=zeros_tail.at[lora_lo:lora_lo + RWKV_LORA].set(rwkv_w2[i]), a0=row(rwkv_a0[i]),
                  a2=zeros_tail.at[lora_lo + RWKV_LORA:lora_lo + 2 * RWKV_LORA].set(rwkv_a2[i]),
                  k_k=row(rwkv_k_k[i]), k_a=row(rwkv_k_a[i]), r_k=row(rwkv_r_k[i]),
                  ln_w=row(rwkv_ln_w[i]), ln_b=row(rwkv_ln_b[i])),
    )


PROMPT_TILES = dict(lru=(4, 256), swa=(8, WINDOW), gla=(8, 128, 64), rwkv=(4, 128, 64))


def _sample_tiles(t_pad):
    return dict(lru=(32, t_pad), swa=(16, t_pad), gla=(16, t_pad, t_pad), rwkv=(16, t_pad, t_pad))


def _layer(x, pe, layer, lp, bias, state, acc, *, tiles, t_valid):
    b, t, _ = x.shape
    prompt = state is None
    rows = b * t
    act = bf16 if prompt else f32
    za, zb, zc, zd, zg = _inproj(x.reshape(rows, D_MODEL), lp["norm_pre"], lp["w_in"], act)
    za = za.reshape(b, t, ZA_W)
    zb = zb.reshape(b, t, ZB_W)
    zc = zc.reshape(b, t, ZC_W)
    zd = zd.reshape(b, t, ZD_W)
    if prompt:
        h0 = jnp.zeros((b, 1, W_MIX), f32)
        conv0 = jnp.zeros((b, SUBLANES, W_MIX), f32)
        kbuf = vbuf = zb[None]
        sg0 = jnp.zeros((1, b, GLA_HEADS * GLA_DK, GLA_DV), f32)
        sr0 = jnp.zeros((1, b, RWKV_HEADS, RWKV_N, RWKV_N), f32)
        sh0 = jnp.zeros((b, 1, ZD_W), f32)
        st_layer = 0
    else:
        h0, conv0, kbuf, vbuf, sg0, sr0, sh0 = state
        h0 = h0[layer].reshape(b, 1, W_MIX)
        conv0 = jnp.pad(conv0[layer], ((0, 0), (SUBLANES - (LRU_CONV - 1), 0), (0, 0)))
        sh0 = jnp.pad(sh0[layer], ((0, 0), (0, ZD_W - RWKV_SHIFT))).reshape(b, 1, ZD_W)
        st_layer = layer
    bb, tc = tiles["lru"]
    ya, h_new = _lru(za, conv0, h0, lp["lru"], bb=bb, tc=tc, t_valid=t_valid, reset_first=prompt, y_dtype=act)
    bb, tq = tiles["swa"]
    yb = _swa(zb, kbuf, vbuf, st_layer, bias, lp["swa_sinks"], lp["swa_norm"], bb=bb, tq=tq, own_prev=prompt,
              y_dtype=act)
    bb, tc, chunk = tiles["gla"]
    yc, acc_g = _gla(zc, zd, sg0, st_layer, acc[0], layer, lp["gla"], bb=bb, tc=tc, chunk=chunk, t_valid=t_valid,
                     y_dtype=act)
    bb, tc, chunk = tiles["rwkv"]
    yd, acc_r = _rwkv(zd, sh0, sr0, st_layer, acc[1], layer, lp["rwkv"], bb=bb, tc=tc, chunk=chunk, t_valid=t_valid,
                      y_dtype=act)
    flat = lambda a: a.reshape(rows, a.shape[-1])
    x_new = _outproj(flat(ya), flat(yb), flat(yc), flat(yd), zg, flat(x), pe.reshape(pe.shape[0], rows, D_PLE), layer,
                     lp["w_out"], lp["norm_post"], lp["ple_proj"], lp["ple_gate"]).reshape(b, t, D_MODEL)
    kv_rows = slice(t_valid - WINDOW, t_valid) if prompt else slice(0, t_valid)
    kw = SWA_KV_HEADS * SWA_HD
    k_new = zb[:, kv_rows, W_MIX:W_MIX + kw].reshape(b, -1, SWA_KV_HEADS, SWA_HD)
    v_new = zb[:, kv_rows, W_MIX + kw:W_MIX + 2 * kw].reshape(b, -1, SWA_KV_HEADS, SWA_HD)
    new_state = (h_new.reshape(b, W_MIX), za[:, t_valid - (LRU_CONV - 1):t_valid], k_new, v_new,
                 zd[:, t_valid - 1, :RWKV_SHIFT])
    return x_new, new_state, (acc_g, acc_r)


def kernel(x_prompt, x_sample, state_lru_h, state_lru_conv, cache_swa_k, cache_swa_v, state_gla, state_rwkv, state_rwkv_shift, p_prompt, p_sample, rel_bias, norm_pre, norm_post, w_in, lru_conv_w, lru_conv_b, lru_wa, lru_ba, lru_wx, lru_bx, lru_lambda, lru_norm, swa_sinks, swa_norm, gla_w_up, gla_b, gla_norm, rwkv_mu, rwkv_w0, rwkv_w2, rwkv_a0, rwkv_a2, rwkv_k_k, rwkv_k_a, rwkv_r_k, rwkv_ln_w, rwkv_ln_b, w_out, ple_proj, ple_gate):
    depth = w_in.shape[0]
    t_p = x_prompt.shape[1]
    t_s = x_sample.shape[1]
    t_s_pad = -(-t_s // SUBLANES) * SUBLANES
    pad_t = lambda a, axis: jnp.pad(a, [(0, t_s_pad - t_s) if d == axis else (0, 0) for d in range(a.ndim)])
    xs = pad_t(x_sample, 1)
    ps = pad_t(p_sample, 2)
    xp = x_prompt
    bias_p = _swa_bias(rel_bias, WINDOW)
    bias_s = _swa_bias(rel_bias, t_s_pad)
    tiles_s = _sample_tiles(t_s_pad)
    n_l, b_s = cache_swa_k.shape[:2]
    kw = SWA_KV_HEADS * SWA_HD
    state_s = (state_lru_h, state_lru_conv, cache_swa_k.reshape(n_l, b_s, WINDOW, kw),
               cache_swa_v.reshape(n_l, b_s, WINDOW, kw),
               state_gla.reshape(n_l, b_s, GLA_HEADS * GLA_DK, GLA_DV), state_rwkv, state_rwkv_shift)
    acc_for = lambda b: (jnp.zeros((depth, b, GLA_HEADS * GLA_DK, GLA_DV), f32),
                         jnp.zeros((depth, b, RWKV_HEADS, RWKV_N, RWKV_N), f32))
    acc_p, acc_s = acc_for(x_prompt.shape[0]), acc_for(b_s)
    sts_p, sts_s = [], []
    for i in range(depth):
        lp = _pack_layer(i, w_in, lru_conv_w, lru_conv_b, lru_wa, lru_ba, lru_wx, lru_bx, lru_lambda, lru_norm,
                         swa_sinks, swa_norm, gla_w_up, gla_b, gla_norm, rwkv_mu, rwkv_w0, rwkv_w2, rwkv_a0,
                         rwkv_a2, rwkv_k_k, rwkv_k_a, rwkv_r_k, rwkv_ln_w, rwkv_ln_b, w_out, ple_proj, ple_gate,
                         norm_pre, norm_post)
        xp, st_p, acc_p = _layer(xp, p_prompt, i, lp, bias_p, None, acc_p, tiles=PROMPT_TILES, t_valid=t_p)
        xs, st_s, acc_s = _layer(xs, ps, i, lp, bias_s, state_s, acc_s, tiles=tiles_s, t_valid=t_s)
        sts_p.append(st_p)
        sts_s.append(st_s)
    hp, cp, kp, vp, shp = [jnp.stack([s[j] for s in sts_p]) for j in range(5)]
    hs, cs, ks, vs, shs = [jnp.stack([s[j] for s in sts_s]) for j in range(5)]
    unpack = lambda a: a.reshape(a.shape[:2] + (GLA_HEADS, GLA_DK, GLA_DV))
    gp, rp = unpack(acc_p[0]), acc_p[1]
    gs, rs = unpack(acc_s[0]), acc_s[1]
    return (xp, xs[:, :t_s], hp, hs, cp, cs, kp, ks, vp, vs, gp, gs, rp, rs, shp, shs)
```

```python
import functools
import math

import jax
import jax.numpy as jnp
import numpy as np
from jax import lax
from jax.experimental import pallas as pl
from jax.experimental.pallas import tpu as pltpu

f32 = jnp.float32
bf16 = jnp.bfloat16

D_MODEL = 1024
D_PLE = 256
RMS_EPS = 1e-6
W_MIX = 256
LRU_BLOCKS = 4
LRU_CONV = 4
LRU_C = 8.0
SWA_HEADS = 4
SWA_KV_HEADS = 2
SWA_HD = 64
WINDOW = 128
REL_BUCKETS = 32
REL_MAX_DIST = 128
GLA_HEADS = 4
GLA_DK = 32
GLA_DV = 64
GLA_RANK = 16
GLA_TAU = 16.0
RWKV_HEADS = 4
RWKV_N = 64
RWKV_LORA = 32
RWKV_SHIFT = 3 * W_MIX + 2 * RWKV_LORA
RWKV_LN_EPS = 64e-5

ZA_W = 256
ZB_W = 512
ZC_W = 512
ZD_W = 896
ZG_W = 1024
ZD_TAIL = 768
N_IN_PACKED = ZA_W + ZB_W + ZC_W + ZD_W + ZG_W

LANES = 128
SUBLANES = 8
VMEM_LIMIT = 48 * 1024 * 1024

ROW_TILE = 512
ROW_SPLIT = 2

NN = (((1,), (0,)), ((), ()))
NT = (((1,), (1,)), ((), ()))
TN = (((0,), (0,)), ((), ()))


def _dot(a, b, dn):
    return lax.dot_general(a, b, dn, preferred_element_type=f32)


def _split2(x):
    hi = x.astype(bf16)
    lo = (x - hi.astype(f32)).astype(bf16)
    return hi, lo


def _mm(a, b, dn=NN, passes=1):
    if passes == 1:
        return _dot(a.astype(bf16), b.astype(bf16), dn)
    ah, al = _split2(a)
    bh, bl = _split2(b)
    return _dot(ah, bh, dn) + (_dot(ah, bl, dn) + _dot(al, bh, dn))


def _mm_exact_rhs(a, b_bf16):
    hi, lo = _split2(a)
    return _dot(hi, b_bf16, NN) + _dot(lo, b_bf16, NN)


def _head_ones(n, width):
    r = lax.broadcasted_iota(jnp.int32, (n, n), 0) // width
    c = lax.broadcasted_iota(jnp.int32, (n, n), 1) // width
    return jnp.where(r == c, 1.0, 0.0).astype(bf16)


def _rms(x, g):
    return x * lax.rsqrt(jnp.mean(x * x, axis=-1, keepdims=True) + RMS_EPS) * g


def _seg_cumsum(x, seg):
    n, w = x.shape
    x3 = x.reshape(n // SUBLANES, SUBLANES, w)
    sub = lax.broadcasted_iota(jnp.int32, (1, SUBLANES, w), 1)
    d = 1
    while d < SUBLANES:
        x3 = x3 + jnp.where(sub >= d, pltpu.roll(x3, d, 1), 0.0)
        d *= 2
    x = x3.reshape(n, w)
    rows = lax.broadcasted_iota(jnp.int32, (n, 1), 0) % seg
    tot = jnp.broadcast_to(x3[:, SUBLANES - 1:SUBLANES, :], x3.shape).reshape(n, w)
    while d < seg:
        add = jnp.where(rows >= d, pltpu.roll(tot, d, 0), 0.0)
        x = x + add
        tot = tot + add
        d *= 2
    return x


def _cat(xs, axis):
    return xs[0] if len(xs) == 1 else jnp.concatenate(xs, axis=axis)


def _params(*sem):
    return pltpu.CompilerParams(dimension_semantics=sem, vmem_limit_bytes=VMEM_LIMIT)


def _inproj_body(x_ref, g_ref, w_ref, za_ref, zb_ref, zc_ref, zd_ref, zg_ref):
    h = _rms(x_ref[...], g_ref[...]).astype(bf16)
    off = 0
    for o_ref in (za_ref, zb_ref, zc_ref, zd_ref, zg_ref):
        n = o_ref.shape[-1]
        o_ref[...] = _dot(h, w_ref[:, off:off + n], NN).astype(o_ref.dtype)
        off += n


def _inproj(x, g, w, gate_dtype):
    rows = x.shape[0]
    tm = min(ROW_TILE, rows)
    widths = (ZA_W, ZB_W, ZC_W, ZD_W, ZG_W)
    return pl.pallas_call(
        _inproj_body,
        grid=(rows // tm,),
        in_specs=[
            pl.BlockSpec((tm, D_MODEL), lambda i: (i, 0)),
            pl.BlockSpec((1, D_MODEL), lambda i: (0, 0)),
            pl.BlockSpec((D_MODEL, N_IN_PACKED), lambda i: (0, 0)),
        ],
        out_specs=[pl.BlockSpec((tm, n), lambda i: (i, 0)) for n in widths],
        out_shape=[jax.ShapeDtypeStruct((rows, n), gate_dtype if n == ZG_W else f32) for n in widths],
        compiler_params=_params("parallel"),
        name="in_proj",
    )(x, g, w)


def _outproj_body(ya_ref, yb_ref, yc_ref, yd_ref, zg_ref, x_ref, pe_ref, wo_ref, gp_ref, wp_ref, wg_ref, o_ref):
    tm = x_ref.shape[0]
    parts = [slice(r, r + tm // ROW_SPLIT) for r in range(0, tm, tm // ROW_SPLIT)]
    accs = []
    for rs in parts:
        g = zg_ref[rs, :].astype(f32)
        sg = g * jax.nn.sigmoid(g)
        acc = None
        for i, y_ref in enumerate((ya_ref, yb_ref, yc_ref, yd_ref)):
            lo = i * W_MIX
            y = (y_ref[rs, :].astype(f32) * sg[:, lo:lo + W_MIX]).astype(bf16)
            part = _dot(y, wo_ref[lo:lo + W_MIX, :], NN)
            acc = part if acc is None else acc + part
        accs.append(acc)
    x1 = [x_ref[rs, :] + _rms(acc, gp_ref[...]) for rs, acc in zip(parts, accs)]
    emb = [_dot(pe_ref[rs, :].astype(bf16), wp_ref[...], NN) for rs in parts]
    gate = [jax.nn.sigmoid(_dot(x.astype(bf16), wg_ref[...], NN)) for x in x1]
    for rs, x, e, gt in zip(parts, x1, emb, gate):
        o_ref[rs, :] = x + e * gt


def _outproj(ya, yb, yc, yd, zg, x, pe, layer, wo, gp, wp, wg):
    rows = x.shape[0]
    tm = min(ROW_TILE, rows)
    row = lambda n: pl.BlockSpec((tm, n), lambda i: (i, 0))
    full = lambda a: pl.BlockSpec(a.shape, lambda i: (0, 0))
    return pl.pallas_call(
        _outproj_body,
        grid=(rows // tm,),
        in_specs=[row(W_MIX), row(W_MIX), row(W_MIX), row(W_MIX), row(ZG_W), row(D_MODEL),
                  pl.BlockSpec((None, tm, D_PLE), lambda i: (layer, i, 0)),
                  full(wo), full(gp), full(wp), full(wg)],
        out_specs=row(D_MODEL),
        out_shape=jax.ShapeDtypeStruct((rows, D_MODEL), f32),
        compiler_params=_params("parallel"),
        name="out_proj",
    )(ya, yb, yc, yd, zg, x, pe, wo, gp, wp, wg)


def _lru_body(x_ref, c0_ref, h0_ref, cw_ref, cb_ref, wa_ref, ba_ref, wx_ref, bx_ref, lam_ref, gn_ref,
              y_ref, hl_ref, ext_ref, hc_ref, *, bb, tc, t_valid, reset_first):
    j = pl.program_id(1)

    @pl.when(j == 0)
    def _():
        ext_ref[:, 0:SUBLANES, :] = c0_ref[...]
        hc_ref[...] = h0_ref[...]

    @pl.when(j > 0)
    def _():
        ext_ref[:, 0:SUBLANES, :] = ext_ref[:, tc:tc + SUBLANES, :]

    n = bb * tc
    ext_ref[:, SUBLANES:SUBLANES + tc, :] = x_ref[...]
    xc = cw_ref[3:4, :] * x_ref[...].reshape(n, W_MIX) + cb_ref[...]
    for k in range(LRU_CONV - 1):
        s = SUBLANES - (LRU_CONV - 1) + k
        xc = xc + cw_ref[k:k + 1, :] * _cat([ext_ref[i, s:s + tc, :] for i in range(bb)], 0)
    gate_r = _mm(xc, wa_ref[...]) + ba_ref[...]
    gate_i = _mm(xc, wx_ref[...]) + bx_ref[...]
    log_a = -LRU_C * jax.nn.sigmoid(gate_r) * jax.nn.softplus(-lam_ref[...])
    a = jnp.exp(log_a)
    mult = jnp.sqrt(1.0 - a * a)
    if reset_first:
        rows = lax.broadcasted_iota(jnp.int32, (n, W_MIX), 0)
        mult = jnp.where(rows % tc + j * tc == 0, 1.0, mult)
    b = mult * jax.nn.sigmoid(gate_i) * xc
    ng = tc // SUBLANES
    a = a.reshape(bb * ng, SUBLANES, W_MIX)
    b = b.reshape(bb * ng, SUBLANES, W_MIX)
    sub = lax.broadcasted_iota(jnp.int32, (1, SUBLANES, W_MIX), 1)
    d = 1
    while d < SUBLANES:
        m = sub >= d
        b = a * jnp.where(m, pltpu.roll(b, d, 1), 0.0) + b
        a = a * jnp.where(m, pltpu.roll(a, d, 1), 1.0)
        d *= 2
    hs = []
    for i in range(bb):
        carry = jnp.broadcast_to(hc_ref[i], (SUBLANES, W_MIX))
        for g in range(ng):
            hg = a[i * ng + g] * carry + b[i * ng + g]
            hs.append(hg)
            carry = jnp.broadcast_to(hg[SUBLANES - 1:SUBLANES], (SUBLANES, W_MIX))
        hc_ref[i] = carry[0:1]
    h = _cat(hs, 0)
    y_ref[...] = _rms(h, gn_ref[...]).reshape(bb, tc, W_MIX).astype(y_ref.dtype)
    r_last = (t_valid - 1) % tc

    @pl.when(j == (t_valid - 1) // tc)
    def _():
        for i in range(bb):
            hl_ref[i] = h[i * tc + r_last:i * tc + r_last + 1, :]


def _lru(za, conv0, h0, p, *, bb, tc, t_valid, reset_first, y_dtype):
    b, t, _ = za.shape
    vec = pl.BlockSpec((1, W_MIX), lambda i, j: (0, 0))
    mat = pl.BlockSpec((W_MIX, W_MIX), lambda i, j: (0, 0))
    return pl.pallas_call(
        functools.partial(_lru_body, bb=bb, tc=tc, t_valid=t_valid, reset_first=reset_first),
        grid=(b // bb, t // tc),
        in_specs=[
            pl.BlockSpec((bb, tc, W_MIX), lambda i, j: (i, j, 0)),
            pl.BlockSpec((bb, SUBLANES, W_MIX), lambda i, j: (i, 0, 0)),
            pl.BlockSpec((bb, 1, W_MIX), lambda i, j: (i, 0, 0)),
            pl.BlockSpec((LRU_CONV, W_MIX), lambda i, j: (0, 0)),
            vec, mat, vec, mat, vec, vec, vec,
        ],
        out_specs=[
            pl.BlockSpec((bb, tc, W_MIX), lambda i, j: (i, j, 0)),
            pl.BlockSpec((bb, 1, W_MIX), lambda i, j: (i, 0, 0)),
        ],
        out_shape=[jax.ShapeDtypeStruct((b, t, W_MIX), y_dtype), jax.ShapeDtypeStruct((b, 1, W_MIX), f32)],
        scratch_shapes=[pltpu.VMEM((bb, tc + SUBLANES, W_MIX), f32), pltpu.VMEM((bb, 1, W_MIX), f32)],
        compiler_params=_params("parallel", "arbitrary"),
        name="rglru",
    )(za, conv0, h0, p["conv_w"], p["conv_b"], p["wa"], p["ba"], p["wx"], p["bx"], p["lam"], p["norm"])


def _swa_body(q_ref, ko_ref, vo_ref, kp_ref, vp_ref, bias_ref, sink_ref, gn_ref, y_ref, *, bb, tq, mask_first):
    n = pl.program_id(1)
    nk = WINDOW + tq
    kw = SWA_KV_HEADS * SWA_HD
    lo = lax.broadcasted_iota(jnp.int32, (1, kw), 1) < SWA_HD
    row_head = lax.broadcasted_iota(jnp.int32, (SWA_HEADS * tq, 1), 0) // tq
    sink = jnp.zeros((SWA_HEADS * tq, 1), f32)
    for h in range(SWA_HEADS):
        sink = jnp.where(row_head == h, sink_ref[:, h:h + 1], sink)
    bias = bias_ref[...]
    if mask_first:
        col = lax.broadcasted_iota(jnp.int32, (1, nk), 1)
        bias = jnp.where((n > 0) | (col >= WINDOW), bias, -jnp.inf)
    swap = lambda x: pltpu.roll(x, SWA_HD, 1)
    qs = []
    for i in range(bb):
        q = q_ref[i] * (SWA_HD ** -0.5)
        q01, q23 = q[:, 0:kw], q[:, kw:2 * kw]
        qs.append(jnp.concatenate([jnp.where(lo, q01, 0.0), jnp.where(lo, swap(q01), 0.0),
                                   jnp.where(lo, 0.0, swap(q23)), jnp.where(lo, 0.0, q23)], axis=0))
    s = [_mm(qs[i], jnp.concatenate([kp_ref[i], ko_ref[i]], axis=0), NT) + bias for i in range(bb)]
    m = [jnp.maximum(jnp.max(x, axis=-1, keepdims=True), sink) for x in s]
    e = [jnp.exp(x - mx) for x, mx in zip(s, m)]
    den = [jnp.sum(x, axis=-1, keepdims=True) + jnp.exp(sink - mx) for x, mx in zip(e, m)]
    pv = [_mm(e[i], jnp.concatenate([vp_ref[i], vo_ref[i]], axis=0)) / den[i] for i in range(bb)]
    for i in range(bb):
        p = pv[i]
        o01 = jnp.where(lo, p[0:tq], swap(p[tq:2 * tq]))
        o23 = jnp.where(lo, swap(p[2 * tq:3 * tq]), p[3 * tq:4 * tq])
        y_ref[i] = _rms(jnp.concatenate([o01, o23], axis=1), gn_ref[...]).astype(y_ref.dtype)


def _swa(zb, k_prev, v_prev, layer, bias, sinks, gn, *, bb, tq, own_prev, y_dtype):
    b, t, _ = zb.shape
    kw = SWA_KV_HEADS * SWA_HD
    if own_prev:
        kp_spec = pl.BlockSpec((None, bb, WINDOW, kw), lambda i, n: (0, i, jnp.maximum(n - 1, 0), 2))
        vp_spec = pl.BlockSpec((None, bb, WINDOW, kw), lambda i, n: (0, i, jnp.maximum(n - 1, 0), 3))
    else:
        kp_spec = pl.BlockSpec((None, bb, WINDOW, kw), lambda i, n: (layer, i, 0, 0))
        vp_spec = pl.BlockSpec((None, bb, WINDOW, kw), lambda i, n: (layer, i, 0, 0))
    return pl.pallas_call(
        functools.partial(_swa_body, bb=bb, tq=tq, mask_first=own_prev),
        grid=(b // bb, t // tq),
        in_specs=[
            pl.BlockSpec((bb, tq, W_MIX), lambda i, n: (i, n, 0)),
            pl.BlockSpec((bb, tq, kw), lambda i, n: (i, n, 2)),
            pl.BlockSpec((bb, tq, kw), lambda i, n: (i, n, 3)),
            kp_spec, vp_spec,
            pl.BlockSpec(bias.shape, lambda i, n: (0, 0)),
            pl.BlockSpec((1, SWA_HEADS), lambda i, n: (0, 0)),
            pl.BlockSpec((1, W_MIX), lambda i, n: (0, 0)),
        ],
        out_specs=pl.BlockSpec((bb, tq, W_MIX), lambda i, n: (i, n, 0)),
        out_shape=jax.ShapeDtypeStruct((b, t, W_MIX), y_dtype),
        compiler_params=_params("parallel", "arbitrary"),
        name="swa",
    )(zb, zb, zb, k_prev, v_prev, bias, sinks, gn)


def _col_bcast(row):
    n = row.shape[-1]
    return jnp.broadcast_to(row, (n, n)).T


def _gla_body(q_ref, k_ref, v_ref, t_ref, s0_ref, wup_ref, gb_ref, gn_ref, acc_ref, y_ref, sn_ref, st_ref,
              *, bb, tc, chunk, t_valid, masked):
    del acc_ref
    j = pl.program_id(1)
    n = bb * tc
    qk = GLA_HEADS * GLA_DK
    sr = lax.broadcasted_iota(jnp.int32, (qk, W_MIX), 0) // GLA_DK
    sc = lax.broadcasted_iota(jnp.int32, (qk, W_MIX), 1) // GLA_DV
    diag = sr == sc

    @pl.when(j == 0)
    def _():
        for i in range(bb):
            x = s0_ref[i]
            st_ref[i] = jnp.where(diag, jnp.concatenate([x] * GLA_HEADS, axis=1), 0.0)

    la = jax.nn.log_sigmoid(_mm(t_ref[...].reshape(n, LANES), wup_ref[...], passes=3) + gb_ref[...]) / GLA_TAU
    k = k_ref[...].reshape(n, qk)
    v = v_ref[...].reshape(n, W_MIX)
    if masked:
        live = lax.broadcasted_iota(jnp.int32, (n, 1), 0) % tc + j * tc < t_valid
        la, k, v = (jnp.where(live, x, 0.0) for x in (la, k, v))
    bc = _seg_cumsum(la, chunk)
    q_dec = q_ref[...].reshape(n, qk) * (GLA_DK ** -0.5) * jnp.exp(bc)
    k_inv = k * jnp.exp(-bc)
    khead = lax.broadcasted_iota(jnp.int32, (1, qk), 1) // GLA_DK
    vhead = lax.broadcasted_iota(jnp.int32, (1, W_MIX), 1) // GLA_DV
    r2 = lax.broadcasted_iota(jnp.int32, (GLA_HEADS * tc, tc), 0) % tc
    c2 = lax.broadcasted_iota(jnp.int32, (GLA_HEADS * tc, tc), 1)
    causal = (r2 // chunk == c2 // chunk) & (r2 >= c2)
    nch = tc // chunk
    ts = [slice(i * tc, (i + 1) * tc) for i in range(bb)]
    qs = [jnp.concatenate([jnp.where(khead == h, q_dec[t], 0.0) for h in range(GLA_HEADS)], axis=0) for t in ts]
    att = [jnp.where(causal, _mm(qs[i], k_inv[ts[i]], NT), 0.0) for i in range(bb)]
    full = [_mm(att[i], v[ts[i]]) for i in range(bb)]
    kv = []
    for c in range(nch):
        kv_c = []
        for i in range(bb):
            rs = slice(i * tc + c * chunk, i * tc + (c + 1) * chunk)
            b_end = bc[i * tc + (c + 1) * chunk - 1:i * tc + (c + 1) * chunk, :]
            k_end = k[rs] * jnp.exp(b_end - bc[rs])
            dec = _col_bcast(jnp.exp(b_end))
            kv_c.append((jnp.concatenate([dec] * (W_MIX // qk), axis=1),
                         jnp.where(diag, _mm(k_end, v[rs], TN), 0.0)))
        kv.append(kv_c)
    st = [st_ref[i] for i in range(bb)]
    inter = [[] for _ in range(bb)]
    for c in range(nch):
        for i in range(bb):
            inter[i].append(_mm(q_dec[i * tc + c * chunk:i * tc + (c + 1) * chunk], st[i]))
            st[i] = st[i] * kv[c][i][0] + kv[c][i][1]
    outs = []
    for i in range(bb):
        st_ref[i] = st[i]
        o = jnp.where(vhead == 0, full[i][0:tc], 0.0)
        for h in range(1, GLA_HEADS):
            o = o + jnp.where(vhead == h, full[i][h * tc:(h + 1) * tc], 0.0)
        outs.append(o + _cat(inter[i], 0))
    o = _cat(outs, 0)
    ms = _mm_exact_rhs(o * o, _head_ones(W_MIX, GLA_DV)) / GLA_DV
    y_ref[...] = (o * lax.rsqrt(ms + RMS_EPS) * gn_ref[...]).reshape(bb, tc, W_MIX).astype(y_ref.dtype)

    @pl.when(j == pl.num_programs(1) - 1)
    def _():
        for i in range(bb):
            x = st_ref[i]
            sn_ref[i] = sum(x[:, h * GLA_DV:(h + 1) * GLA_DV] for h in range(1, GLA_HEADS)) + x[:, 0:GLA_DV]


def _gla(zc, zd, s0, layer, acc, layer_out, p, *, bb, tc, chunk, t_valid, y_dtype):
    b, t, _ = zc.shape
    qk = GLA_HEADS * GLA_DK
    state_in = pl.BlockSpec((None, bb, qk, GLA_DV), lambda i, j: (layer, i, 0, 0))
    state_out = pl.BlockSpec((None, bb, qk, GLA_DV), lambda i, j: (layer_out, i, 0, 0))
    return pl.pallas_call(
        functools.partial(_gla_body, bb=bb, tc=tc, chunk=chunk, t_valid=t_valid, masked=t_valid < t),
        grid=(b // bb, t // tc),
        in_specs=[
            pl.BlockSpec((bb, tc, qk), lambda i, j: (i, j, 0)),
            pl.BlockSpec((bb, tc, qk), lambda i, j: (i, j, 1)),
            pl.BlockSpec((bb, tc, W_MIX), lambda i, j: (i, j, 1)),
            pl.BlockSpec((bb, tc, LANES), lambda i, j: (i, j, ZD_TAIL // LANES)),
            state_in,
            pl.BlockSpec((LANES, qk), lambda i, j: (0, 0)),
            pl.BlockSpec((1, qk), lambda i, j: (0, 0)),
            pl.BlockSpec((1, W_MIX), lambda i, j: (0, 0)),
            pl.BlockSpec(memory_space=pl.ANY),
        ],
        out_specs=[pl.BlockSpec((bb, tc, W_MIX), lambda i, j: (i, j, 0)), state_out],
        out_shape=[jax.ShapeDtypeStruct((b, t, W_MIX), y_dtype), jax.ShapeDtypeStruct(acc.shape, f32)],
        input_output_aliases={8: 1},
        scratch_shapes=[pltpu.VMEM((bb, qk, W_MIX), f32)],
        compiler_params=_params("parallel", "arbitrary"),
        name="gla",
    )(zc, zc, zc, zd, s0, p["w_up"], p["b"], p["norm"], acc)


def _rwkv_body(u_ref, sh0_ref, s0_ref, mu_ref, w0_ref, w2_ref, a0_ref, a2_ref, kk_ref, ka_ref, rk_ref,
               lw_ref, lb_ref, acc_ref, y_ref, sn_ref, st_ref, up_ref, *, bb, tc, chunk, t_valid, masked):
    del acc_ref
    j = pl.program_id(1)

    @pl.when(j == 0)
    def _():
        for i in range(bb):
            for h in range(RWKV_HEADS):
                st_ref[i, h] = s0_ref[i, h].T
        up_ref[...] = sh0_ref[...]

    n = bb * tc
    u = u_ref[...].reshape(n, ZD_W)
    rolled = pltpu.roll(u, 1, 0)
    first = lax.broadcasted_iota(jnp.int32, (SUBLANES, 1), 0) == 0
    parts = []
    for i in range(bb):
        parts.append(jnp.where(first, up_ref[i], rolled[i * tc:i * tc + SUBLANES]))
        if tc > SUBLANES:
            parts.append(rolled[i * tc + SUBLANES:(i + 1) * tc])
        up_ref[i] = u[(i + 1) * tc - 1:(i + 1) * tc, :]
    u_prev = _cat(parts, 0)
    us = u + (u_prev - u) * mu_ref[...]
    r = us[:, 0:W_MIX]
    k = us[:, W_MIX:2 * W_MIX]
    v = us[:, 2 * W_MIX:3 * W_MIX]
    tail = us[:, ZD_TAIL:ZD_TAIL + LANES]
    z = w0_ref[...] + _mm(jnp.tanh(tail), w2_ref[...], passes=3)
    lw = -math.exp(-0.5) * jax.nn.sigmoid(z)
    a = jax.nn.sigmoid(a0_ref[...] + _mm(tail, a2_ref[...], passes=3))
    ones = _head_ones(W_MIX, RWKV_N)
    kk = k * kk_ref[...]
    kk = kk / jnp.maximum(jnp.sqrt(_mm_exact_rhs(kk * kk, ones)), 1e-12)
    k = k * (1.0 + (a - 1.0) * ka_ref[...])
    aa, bl, kl, vl = -kk, kk * a, k, v
    if masked:
        live = lax.broadcasted_iota(jnp.int32, (n, 1), 0) % tc + j * tc < t_valid
        lw, aa, bl, kl, vl = (jnp.where(live, x, 0.0) for x in (lw, aa, bl, kl, vl))

    cum = _seg_cumsum(lw, chunk)
    gi = jnp.exp(-cum)
    a_t = aa * jnp.exp(cum - lw)
    b_t = bl * gi
    k_t = kl * gi
    r_t = r * jnp.exp(cum)
    nch = tc // chunk
    r2 = lax.broadcasted_iota(jnp.int32, (tc, tc), 0)
    c2 = lax.broadcasted_iota(jnp.int32, (tc, tc), 1)
    same = r2 // chunk == c2 // chunk
    strict = same & (r2 > c2)
    incl = same & (r2 >= c2)

    chains = [(i, h) for i in range(bb) for h in range(RWKV_HEADS)]
    nq = len(chains)
    sl = lambda x, i, h: x[i * tc:(i + 1) * tc, h * RWKV_N:(h + 1) * RWKV_N]
    ah = [sl(a_t, i, h) for i, h in chains]
    rh = [sl(r_t, i, h) for i, h in chains]
    vh = [sl(vl, i, h) for i, h in chains]
    pair = [_mm(jnp.concatenate([ah[q], rh[q]], axis=0),
                jnp.concatenate([sl(b_t, i, h), sl(k_t, i, h)], axis=0), NT) for q, (i, h) in enumerate(chains)]
    pw = [jnp.where(strict, m[0:tc, 0:tc], 0.0) for m in pair]
    m_rb = [jnp.where(incl, m[tc:2 * tc, 0:tc], 0.0) for m in pair]
    akrk = [jnp.concatenate([jnp.where(strict, m[0:tc, tc:2 * tc], 0.0),
                             jnp.where(incl, m[tc:2 * tc, tc:2 * tc], 0.0)], axis=0) for m in pair]
    both_v = [_mm(akrk[q], vh[q]) for q in range(nq)]
    sol = [jnp.concatenate([ah[q], both_v[q][0:tc]], axis=1) for q in range(nq)]
    y_v = [x[tc:2 * tc] for x in both_v]
    wide = sol[0].shape[1]
    d = 1
    while True:
        last = 2 * d >= chunk
        res = [_mm(pw[q], sol[q] if last else jnp.concatenate([sol[q], pw[q]], axis=1)) for q in range(nq)]
        sol = [sol[q] + res[q][:, 0:wide] for q in range(nq)]
        if last:
            break
        pw = [x[:, wide:wide + tc] for x in res]
        d *= 2
    lhs_c, p2_c, mix_c, yv_c, kv_c, gcol_c = [], [], [], [], [], []
    for c in range(nch):
        rs = slice(c * chunk, (c + 1) * chunk)
        ce = [sl(cum, i, h)[(c + 1) * chunk - 1:(c + 1) * chunk, :] for i, h in chains]
        dec = [jnp.exp(ce[q] - sl(cum, i, h)[rs]) for q, (i, h) in enumerate(chains)]
        lhs_c.append([jnp.concatenate([sol[q][rs, 0:RWKV_N], rh[q][rs]], axis=0) for q in range(nq)])
        p2_c.append([sol[q][rs, RWKV_N:2 * RWKV_N] for q in range(nq)])
        mix_c.append([jnp.concatenate([m_rb[q][rs, rs], (sl(bl, i, h)[rs] * dec[q]).T], axis=0)
                      for q, (i, h) in enumerate(chains)])
        yv_c.append([y_v[q][rs] for q in range(nq)])
        kv_c.append([_mm(sl(kl, i, h)[rs] * dec[q], vh[q][rs], TN) for q, (i, h) in enumerate(chains)])
        gcol_c.append([jnp.broadcast_to(jnp.exp(ce[q]), (RWKV_N, RWKV_N)).T for q in range(nq)])
    st = [st_ref[i, h] for i, h in chains]
    y_parts = [[] for _ in chains]
    for c in range(nch):
        both = [_mm(lhs_c[c][q], st[q]) for q in range(nq)]
        u_c = [both[q][0:chunk] + p2_c[c][q] for q in range(nq)]
        mix = [_mm(mix_c[c][q], u_c[q]) for q in range(nq)]
        st = [gcol_c[c][q] * st[q] + kv_c[c][q] + mix[q][chunk:chunk + RWKV_N] for q in range(nq)]
        for q in range(nq):
            y_parts[q].append(both[q][chunk:2 * chunk] + mix[q][0:chunk] + yv_c[c][q])
    for q, (i, h) in enumerate(chains):
        st_ref[i, h] = st[q]
    y = _cat([_cat([_cat(y_parts[i * RWKV_HEADS + h], 0) for h in range(RWKV_HEADS)], 1) for i in range(bb)], 0)
    mean = _mm_exact_rhs(y, ones) / RWKV_N
    yc = y - mean
    var = _mm_exact_rhs(yc * yc, ones) / RWKV_N
    out = yc * lax.rsqrt(var + RWKV_LN_EPS) * lw_ref[...] + lb_ref[...]
    bonus = _mm_exact_rhs(r * k * rk_ref[...], ones)
    y_ref[...] = (out + bonus * v).reshape(bb, tc, W_MIX).astype(y_ref.dtype)

    @pl.when(j == pl.num_programs(1) - 1)
    def _():
        for i in range(bb):
            for h in range(RWKV_HEADS):
                sn_ref[i, h] = st_ref[i, h].T


def _rwkv(zd, shift0, s0, layer, acc, layer_out, p, *, bb, tc, chunk, t_valid, y_dtype):
    b, t, _ = zd.shape
    vec = lambda n: pl.BlockSpec((1, n), lambda i, j: (0, 0))
    lora = pl.BlockSpec((LANES, W_MIX), lambda i, j: (0, 0))
    state_in = pl.BlockSpec((None, bb, RWKV_HEADS, RWKV_N, RWKV_N), lambda i, j: (layer, i, 0, 0, 0))
    state_out = pl.BlockSpec((None, bb, RWKV_HEADS, RWKV_N, RWKV_N), lambda i, j: (layer_out, i, 0, 0, 0))
    return pl.pallas_call(
        functools.partial(_rwkv_body, bb=bb, tc=tc, chunk=chunk, t_valid=t_valid, masked=t_valid < t),
        grid=(b // bb, t // tc),
        in_specs=[
            pl.BlockSpec((bb, tc, ZD_W), lambda i, j: (i, j, 0)),
            pl.BlockSpec((bb, 1, ZD_W), lambda i, j: (i, 0, 0)),
            state_in,
            vec(ZD_W), vec(W_MIX), lora, vec(W_MIX), lora, vec(W_MIX), vec(W_MIX), vec(W_MIX),
            vec(W_MIX), vec(W_MIX),
            pl.BlockSpec(memory_space=pl.ANY),
        ],
        out_specs=[pl.BlockSpec((bb, tc, W_MIX), lambda i, j: (i, j, 0)), state_out],
        out_shape=[jax.ShapeDtypeStruct((b, t, W_MIX), y_dtype), jax.ShapeDtypeStruct(acc.shape, f32)],
        input_output_aliases={13: 1},
        scratch_shapes=[pltpu.VMEM((bb, RWKV_HEADS, RWKV_N, RWKV_N), f32), pltpu.VMEM((bb, 1, ZD_W), f32)],
        compiler_params=_params("parallel", "arbitrary"),
        name="rwkv7",
    )(zd, shift0, s0, p["mu"], p["w0"], p["w2"], p["a0"], p["a2"], p["k_k"], p["k_a"], p["r_k"],
      p["ln_w"], p["ln_b"], acc)


def _block_diag(w):
    n, c, _ = w.shape
    out = jnp.zeros((n * c, n * c), w.dtype)
    for i in range(n):
        out = out.at[i * c:(i + 1) * c, i * c:(i + 1) * c].set(w[i])
    return out


def _rel_bucket_ids(tq):
    dist = WINDOW + np.arange(tq)[:, None] - np.arange(WINDOW + tq)[None, :]
    max_exact = REL_BUCKETS // 2
    n = np.maximum(dist, 0)
    log_ratio = (np.log(np.maximum(n, 1).astype(np.float32) / np.float32(max_exact))
                 / np.float32(math.log(REL_MAX_DIST / max_exact)))
    large = np.minimum(max_exact + (log_ratio * np.float32(REL_BUCKETS - max_exact)).astype(np.int32),
                       REL_BUCKETS - 1)
    bucket = np.where(n < max_exact, n, large)
    return np.where((dist >= 0) & (dist <= WINDOW), bucket, -1).astype(np.int32)


def _swa_bias(table, tq):
    ids = jnp.asarray(_rel_bucket_ids(tq))[None]
    out = jnp.full((SWA_HEADS, tq, WINDOW + tq), -jnp.inf, f32)
    for k in range(REL_BUCKETS):
        out = jnp.where(ids == k, table[k].astype(f32)[:, None, None], out)
    return out.reshape(SWA_HEADS * tq, WINDOW + tq)


def _pack_layer(i, w_in, lru_conv_w, lru_conv_b, lru_wa, lru_ba, lru_wx, lru_bx, lru_lambda, lru_norm,
                swa_sinks, swa_norm, gla_w_up, gla_b, gla_norm, rwkv_mu, rwkv_w0, rwkv_w2, rwkv_a0, rwkv_a2,
                rwkv_k_k, rwkv_k_a, rwkv_r_k, rwkv_ln_w, rwkv_ln_b, w_out, ple_proj, ple_gate,
                norm_pre, norm_post):
    w = w_in[i]
    c_ca = 5 * W_MIX
    c_ud = c_ca + GLA_RANK
    c_gate = c_ud + RWKV_SHIFT
    pad = jnp.zeros((D_MODEL, ZD_W - RWKV_SHIFT - GLA_RANK), w.dtype)
    w_packed = jnp.concatenate([w[:, :c_ca], w[:, c_ud:c_gate], w[:, c_ca:c_ud], pad, w[:, c_gate:]], axis=1)
    row = lambda a: a.reshape(1, -1)
    lora_lo = RWKV_SHIFT - 2 * RWKV_LORA - ZD_TAIL
    ca_lo = RWKV_SHIFT - ZD_TAIL
    zeros_tail = jnp.zeros((LANES, W_MIX), f32)
    return dict(
        w_in=w_packed.astype(bf16),
        norm_pre=row(norm_pre[i]), norm_post=row(norm_post[i]),
        w_out=w_out[i].astype(bf16), ple_proj=ple_proj[i].astype(bf16), ple_gate=ple_gate[i].astype(bf16),
        lru=dict(conv_w=lru_conv_w[i], conv_b=row(lru_conv_b[i]), wa=_block_diag(lru_wa[i]), ba=row(lru_ba[i]),
                 wx=_block_diag(lru_wx[i]), bx=row(lru_bx[i]), lam=row(lru_lambda[i]), norm=row(lru_norm[i])),
        swa_sinks=row(swa_sinks[i]), swa_norm=row(swa_norm[i]),
        gla=dict(w_up=jnp.zeros((LANES, GLA_HEADS * GLA_DK), f32).at[ca_lo:ca_lo + GLA_RANK].set(gla_w_up[i]),
                 b=row(gla_b[i]), norm=row(jnp.tile(gla_norm[i], GLA_HEADS))),
        rwkv=dict(mu=row(jnp.pad(rwkv_mu[i], (0, ZD_W - RWKV_SHIFT))), w0=row(rwkv_w0[i]),
                  w2=zeros_tail.at[lora_lo:lora_lo + RWKV_LORA].set(rwkv_w2[i]), a0=row(rwkv_a0[i]),
                  a2=zeros_tail.at[lora_lo + RWKV_LORA:lora_lo + 2 * RWKV_LORA].set(rwkv_a2[i]),
                  k_k=row(rwkv_k_k[i]), k_a=row(rwkv_k_a[i]), r_k=row(rwkv_r_k[i]),
                  ln_w=row(rwkv_ln_w[i]), ln_b=row(rwkv_ln_b[i])),
    )


PROMPT_TILES = dict(lru=(8, 256), swa=(16, WINDOW), gla=(8, 128, 32), rwkv=(4, 128, 64))


def _sample_tiles(t_pad):
    return dict(lru=(32, t_pad), swa=(16, t_pad), gla=(16, t_pad, t_pad), rwkv=(16, t_pad, t_pad))


def _layer(x, pe, layer, lp, bias, state, acc, *, tiles, t_valid):
    b, t, _ = x.shape
    prompt = state is None
    rows = b * t
    act = bf16 if prompt else f32
    za, zb, zc, zd, zg = _inproj(x.reshape(rows, D_MODEL), lp["norm_pre"], lp["w_in"], act)
    za = za.reshape(b, t, ZA_W)
    zb = zb.reshape(b, t, ZB_W)
    zc = zc.reshape(b, t, ZC_W)
    zd = zd.reshape(b, t, ZD_W)
    if prompt:
        h0 = jnp.zeros((b, 1, W_MIX), f32)
        conv0 = jnp.zeros((b, SUBLANES, W_MIX), f32)
        kbuf = vbuf = zb[None]
        sg0 = jnp.zeros((1, b, GLA_HEADS * GLA_DK, GLA_DV), f32)
        sr0 = jnp.zeros((1, b, RWKV_HEADS, RWKV_N, RWKV_N), f32)
        sh0 = jnp.zeros((b, 1, ZD_W), f32)
        st_layer = 0
    else:
        h0, conv0, kbuf, vbuf, sg0, sr0, sh0 = state
        h0 = h0[layer].reshape(b, 1, W_MIX)
        conv0 = jnp.pad(conv0[layer], ((0, 0), (SUBLANES - (LRU_CONV - 1), 0), (0, 0)))
        sh0 = jnp.pad(sh0[layer], ((0, 0), (0, ZD_W - RWKV_SHIFT))).reshape(b, 1, ZD_W)
        st_layer = layer
    bb, tc = tiles["lru"]
    ya, h_new = _lru(za, conv0, h0, lp["lru"], bb=bb, tc=tc, t_valid=t_valid, reset_first=prompt, y_dtype=act)
    bb, tq = tiles["swa"]
    yb = _swa(zb, kbuf, vbuf, st_layer, bias, lp["swa_sinks"], lp["swa_norm"], bb=bb, tq=tq, own_prev=prompt,
              y_dtype=act)
    bb, tc, chunk = tiles["gla"]
    yc, acc_g = _gla(zc, zd, sg0, st_layer, acc[0], layer, lp["gla"], bb=bb, tc=tc, chunk=chunk, t_valid=t_valid,
                     y_dtype=act)
    bb, tc, chunk = tiles["rwkv"]
    yd, acc_r = _rwkv(zd, sh0, sr0, st_layer, acc[1], layer, lp["rwkv"], bb=bb, tc=tc, chunk=chunk, t_valid=t_valid,
                      y_dtype=act)
    flat = lambda a: a.reshape(rows, a.shape[-1])
    x_new = _outproj(flat(ya), flat(yb), flat(yc), flat(yd), zg, flat(x), pe.reshape(pe.shape[0], rows, D_PLE), layer,
                     lp["w_out"], lp["norm_post"], lp["ple_proj"], lp["ple_gate"]).reshape(b, t, D_MODEL)
    kv_rows = slice(t_valid - WINDOW, t_valid) if prompt else slice(0, t_valid)
    kw = SWA_KV_HEADS * SWA_HD
    k_new = zb[:, kv_rows, W_MIX:W_MIX + kw].reshape(b, -1, SWA_KV_HEADS, SWA_HD)
    v_new = zb[:, kv_rows, W_MIX + kw:W_MIX + 2 * kw].reshape(b, -1, SWA_KV_HEADS, SWA_HD)
    new_state = (h_new.reshape(b, W_MIX), za[:, t_valid - (LRU_CONV - 1):t_valid], k_new, v_new,
                 zd[:, t_valid - 1, :RWKV_SHIFT])
    return x_new, new_state, (acc_g, acc_r)


def kernel(x_prompt, x_sample, state_lru_h, state_lru_conv, cache_swa_k, cache_swa_v, state_gla, state_rwkv, state_rwkv_shift, p_prompt, p_sample, rel_bias, norm_pre, norm_post, w_in, lru_conv_w, lru_conv_b, lru_wa, lru_ba, lru_wx, lru_bx, lru_lambda, lru_norm, swa_sinks, swa_norm, gla_w_up, gla_b, gla_norm, rwkv_mu, rwkv_w0, rwkv_w2, rwkv_a0, rwkv_a2, rwkv_k_k, rwkv_k_a, rwkv_r_k, rwkv_ln_w, rwkv_ln_b, w_out, ple_proj, ple_gate):
    depth = w_in.shape[0]
    t_p = x_prompt.shape[1]
    t_s = x_sample.shape[1]
    t_s_pad = -(-t_s // SUBLANES) * SUBLANES
    pad_t = lambda a, axis: jnp.pad(a, [(0, t_s_pad - t_s) if d == axis else (0, 0) for d in range(a.ndim)])
    xs = pad_t(x_sample, 1)
    ps = pad_t(p_sample, 2)
    xp = x_prompt
    bias_p = _swa_bias(rel_bias, WINDOW)
    bias_s = _swa_bias(rel_bias, t_s_pad)
    tiles_s = _sample_tiles(t_s_pad)
    n_l, b_s = cache_swa_k.shape[:2]
    kw = SWA_KV_HEADS * SWA_HD
    state_s = (state_lru_h, state_lru_conv, cache_swa_k.reshape(n_l, b_s, WINDOW, kw),
               cache_swa_v.reshape(n_l, b_s, WINDOW, kw),
               state_gla.reshape(n_l, b_s, GLA_HEADS * GLA_DK, GLA_DV), state_rwkv, state_rwkv_shift)
    acc_for = lambda b: (jnp.zeros((depth, b, GLA_HEADS * GLA_DK, GLA_DV), f32),
                         jnp.zeros((depth, b, RWKV_HEADS, RWKV_N, RWKV_N), f32))
    acc_p, acc_s = acc_for(x_prompt.shape[0]), acc_for(b_s)
    sts_p, sts_s = [], []
    for i in range(depth):
        lp = _pack_layer(i, w_in, lru_conv_w, lru_conv_b, lru_wa, lru_ba, lru_wx, lru_bx, lru_lambda, lru_norm,
                         swa_sinks, swa_norm, gla_w_up, gla_b, gla_norm, rwkv_mu, rwkv_w0, rwkv_w2, rwkv_a0,
                         rwkv_a2, rwkv_k_k, rwkv_k_a, rwkv_r_k, rwkv_ln_w, rwkv_ln_b, w_out, ple_proj, ple_gate,
                         norm_pre, norm_post)
        xp, st_p, acc_p = _layer(xp, p_prompt, i, lp, bias_p, None, acc_p, tiles=PROMPT_TILES, t_valid=t_p)
        xs, st_s, acc_s = _layer(xs, ps, i, lp, bias_s, state_s, acc_s, tiles=tiles_s, t_valid=t_s)
        sts_p.append(st_p)
        sts_s.append(st_s)
    hp, cp, kp, vp, shp = [jnp.stack([s[j] for s in sts_p]) for j in range(5)]
    hs, cs, ks, vs, shs = [jnp.stack([s[j] for s in sts_s]) for j in range(5)]
    unpack = lambda a: a.reshape(a.shape[:2] + (GLA_HEADS, GLA_DK, GLA_DV))
    gp, rp = unpack(acc_p[0]), acc_p[1]
    gs, rs = unpack(acc_s[0]), acc_s[1]
    return (xp, xs[:, :t_s], hp, hs, cp, cs, kp, ks, vp, vs, gp, gs, rp, rs, shp, shs)
```

```python
import functools
import math

import jax
import jax.numpy as jnp
import numpy as np
from jax import lax
from jax.experimental import pallas as pl
from jax.experimental.pallas import tpu as pltpu

f32 = jnp.float32
bf16 = jnp.bfloat16

D_MODEL = 1024
D_PLE = 256
RMS_EPS = 1e-6
W_MIX = 256
LRU_BLOCKS = 4
LRU_CONV = 4
LRU_C = 8.0
SWA_HEADS = 4
SWA_KV_HEADS = 2
SWA_HD = 64
WINDOW = 128
REL_BUCKETS = 32
REL_MAX_DIST = 128
GLA_HEADS = 4
GLA_DK = 32
GLA_DV = 64
GLA_RANK = 16
GLA_TAU = 16.0
RWKV_HEADS = 4
RWKV_N = 64
RWKV_LORA = 32
RWKV_SHIFT = 3 * W_MIX + 2 * RWKV_LORA
RWKV_LN_EPS = 64e-5

ZA_W = 256
ZB_W = 512
ZC_W = 512
ZD_W = 896
ZG_W = 1024
ZD_TAIL = 768
N_IN_PACKED = ZA_W + ZB_W + ZC_W + ZD_W + ZG_W

LANES = 128
SUBLANES = 8
VMEM_LIMIT = 48 * 1024 * 1024

ROW_TILE = 1024
ROW_SPLIT = 2

NN = (((1,), (0,)), ((), ()))
NT = (((1,), (1,)), ((), ()))
TN = (((0,), (0,)), ((), ()))


def _dot(a, b, dn):
    return lax.dot_general(a, b, dn, preferred_element_type=f32)


def _split2(x):
    hi = x.astype(bf16)
    lo = (x - hi.astype(f32)).astype(bf16)
    return hi, lo


def _mm(a, b, dn=NN, passes=1):
    if passes == 1:
        return _dot(a.astype(bf16), b.astype(bf16), dn)
    ah, al = _split2(a)
    bh, bl = _split2(b)
    return _dot(ah, bh, dn) + (_dot(ah, bl, dn) + _dot(al, bh, dn))


def _mm_exact_rhs(a, b_bf16):
    hi, lo = _split2(a)
    return _dot(hi, b_bf16, NN) + _dot(lo, b_bf16, NN)


def _head_ones(n, width):
    r = lax.broadcasted_iota(jnp.int32, (n, n), 0) // width
    c = lax.broadcasted_iota(jnp.int32, (n, n), 1) // width
    return jnp.where(r == c, 1.0, 0.0).astype(bf16)


def _rms(x, g):
    return x * lax.rsqrt(jnp.mean(x * x, axis=-1, keepdims=True) + RMS_EPS) * g


def _seg_cumsum(x, seg):
    n, w = x.shape
    x3 = x.reshape(n // SUBLANES, SUBLANES, w)
    sub = lax.broadcasted_iota(jnp.int32, (1, SUBLANES, w), 1)
    d = 1
    while d < SUBLANES:
        x3 = x3 + jnp.where(sub >= d, pltpu.roll(x3, d, 1), 0.0)
        d *= 2
    x = x3.reshape(n, w)
    rows = lax.broadcasted_iota(jnp.int32, (n, 1), 0) % seg
    tot = jnp.broadcast_to(x3[:, SUBLANES - 1:SUBLANES, :], x3.shape).reshape(n, w)
    while d < seg:
        add = jnp.where(rows >= d, pltpu.roll(tot, d, 0), 0.0)
        x = x + add
        tot = tot + add
        d *= 2
    return x


def _cat(xs, axis):
    return xs[0] if len(xs) == 1 else jnp.concatenate(xs, axis=axis)


def _params(*sem):
    return pltpu.CompilerParams(dimension_semantics=sem, vmem_limit_bytes=VMEM_LIMIT)


def _inproj_body(x_ref, g_ref, w_ref, za_ref, zb_ref, zc_ref, zd_ref, zg_ref):
    h = _rms(x_ref[...], g_ref[...]).astype(bf16)
    off = 0
    for o_ref in (za_ref, zb_ref, zc_ref, zd_ref, zg_ref):
        n = o_ref.shape[-1]
        o_ref[...] = _dot(h, w_ref[:, off:off + n], NN).astype(o_ref.dtype)
        off += n


def _inproj(x, g, w, gate_dtype):
    rows = x.shape[0]
    tm = min(ROW_TILE, rows)
    widths = (ZA_W, ZB_W, ZC_W, ZD_W, ZG_W)
    return pl.pallas_call(
        _inproj_body,
        grid=(rows // tm,),
        in_specs=[
            pl.BlockSpec((tm, D_MODEL), lambda i: (i, 0)),
            pl.BlockSpec((1, D_MODEL), lambda i: (0, 0)),
            pl.BlockSpec((D_MODEL, N_IN_PACKED), lambda i: (0, 0)),
        ],
        out_specs=[pl.BlockSpec((tm, n), lambda i: (i, 0)) for n in widths],
        out_shape=[jax.ShapeDtypeStruct((rows, n), gate_dtype if n == ZG_W else f32) for n in widths],
        compiler_params=_params("parallel"),
        name="in_proj",
    )(x, g, w)


def _outproj_body(ya_ref, yb_ref, yc_ref, yd_ref, zg_ref, x_ref, pe_ref, wo_ref, gp_ref, wp_ref, wg_ref, o_ref):
    tm = x_ref.shape[0]
    parts = [slice(r, r + tm // ROW_SPLIT) for r in range(0, tm, tm // ROW_SPLIT)]
    accs = []
    for rs in parts:
        g = zg_ref[rs, :].astype(f32)
        sg = g * jax.nn.sigmoid(g)
        acc = None
        for i, y_ref in enumerate((ya_ref, yb_ref, yc_ref, yd_ref)):
            lo = i * W_MIX
            y = (y_ref[rs, :].astype(f32) * sg[:, lo:lo + W_MIX]).astype(bf16)
            part = _dot(y, wo_ref[lo:lo + W_MIX, :], NN)
            acc = part if acc is None else acc + part
        accs.append(acc)
    x1 = [x_ref[rs, :] + _rms(acc, gp_ref[...]) for rs, acc in zip(parts, accs)]
    emb = [_dot(pe_ref[rs, :].astype(bf16), wp_ref[...], NN) for rs in parts]
    gate = [jax.nn.sigmoid(_dot(x.astype(bf16), wg_ref[...], NN)) for x in x1]
    for rs, x, e, gt in zip(parts, x1, emb, gate):
        o_ref[rs, :] = x + e * gt


def _outproj(ya, yb, yc, yd, zg, x, pe, layer, wo, gp, wp, wg):
    rows = x.shape[0]
    tm = min(ROW_TILE, rows)
    row = lambda n: pl.BlockSpec((tm, n), lambda i: (i, 0))
    full = lambda a: pl.BlockSpec(a.shape, lambda i: (0, 0))
    return pl.pallas_call(
        _outproj_body,
        grid=(rows // tm,),
        in_specs=[row(W_MIX), row(W_MIX), row(W_MIX), row(W_MIX), row(ZG_W), row(D_MODEL),
                  pl.BlockSpec((None, tm, D_PLE), lambda i: (layer, i, 0)),
                  full(wo), full(gp), full(wp), full(wg)],
        out_specs=row(D_MODEL),
        out_shape=jax.ShapeDtypeStruct((rows, D_MODEL), f32),
        compiler_params=_params("parallel"),
        name="out_proj",
    )(ya, yb, yc, yd, zg, x, pe, wo, gp, wp, wg)


def _lru_body(x_ref, c0_ref, h0_ref, cw_ref, cb_ref, wa_ref, ba_ref, wx_ref, bx_ref, lam_ref, gn_ref,
              y_ref, hl_ref, ext_ref, hc_ref, *, bb, tc, t_valid, reset_first):
    j = pl.program_id(1)

    @pl.when(j == 0)
    def _():
        ext_ref[...] = c0_ref[...]
        hc_ref[...] = h0_ref[...]

    n = bb * tc
    ng = tc // SUBLANES
    sub = lax.broadcasted_iota(jnp.int32, (1, SUBLANES, W_MIX), 1)
    x3 = x_ref[...].reshape(bb * ng, SUBLANES, W_MIX)
    xc = cw_ref[3:4, :] * x3 + cb_ref[...]
    for k in range(1, LRU_CONV):
        rot = pltpu.roll(x3, k, 1)
        before = _cat([_cat([pltpu.roll(ext_ref[i], k, 0)[None]] + ([rot[i * ng:(i + 1) * ng - 1]] if ng > 1 else []), 0)
                       for i in range(bb)], 0)
        xc = xc + cw_ref[LRU_CONV - 1 - k:LRU_CONV - k, :] * jnp.where(sub >= k, rot, before)
    for i in range(bb):
        ext_ref[i] = x3[(i + 1) * ng - 1]
    xc = xc.reshape(n, W_MIX)
    gate_r = _mm(xc, wa_ref[...]) + ba_ref[...]
    gate_i = _mm(xc, wx_ref[...]) + bx_ref[...]
    log_a = -LRU_C * jax.nn.sigmoid(gate_r) * jax.nn.softplus(-lam_ref[...])
    a = jnp.exp(log_a)
    gap = 1.0 - a * a
    mult = jnp.where(gap > 0.0, gap * lax.rsqrt(gap), 0.0)
    if reset_first:
        rows = lax.broadcasted_iota(jnp.int32, (n, W_MIX), 0)
        mult = jnp.where(rows % tc + j * tc == 0, 1.0, mult)
    b = mult * jax.nn.sigmoid(gate_i) * xc
    a = a.reshape(bb * ng, SUBLANES, W_MIX)
    b = b.reshape(bb * ng, SUBLANES, W_MIX)
    d = 1
    while d < SUBLANES:
        m = sub >= d
        b = a * jnp.where(m, pltpu.roll(b, d, 1), 0.0) + b
        a = a * jnp.where(m, pltpu.roll(a, d, 1), 1.0)
        d *= 2
    hs = []
    for i in range(bb):
        carry = jnp.broadcast_to(hc_ref[i], (SUBLANES, W_MIX))
        for g in range(ng):
            hg = a[i * ng + g] * carry + b[i * ng + g]
            hs.append(hg)
            carry = jnp.broadcast_to(hg[SUBLANES - 1:SUBLANES], (SUBLANES, W_MIX))
        hc_ref[i] = carry[0:1]
    h = _cat(hs, 0)
    y_ref[...] = _rms(h, gn_ref[...]).reshape(bb, tc, W_MIX).astype(y_ref.dtype)
    r_last = (t_valid - 1) % tc

    @pl.when(j == (t_valid - 1) // tc)
    def _():
        for i in range(bb):
            hl_ref[i] = h[i * tc + r_last:i * tc + r_last + 1, :]


def _lru(za, conv0, h0, p, *, bb, tc, t_valid, reset_first, y_dtype):
    b, t, _ = za.shape
    vec = pl.BlockSpec((1, W_MIX), lambda i, j: (0, 0))
    mat = pl.BlockSpec((W_MIX, W_MIX), lambda i, j: (0, 0))
    return pl.pallas_call(
        functools.partial(_lru_body, bb=bb, tc=tc, t_valid=t_valid, reset_first=reset_first),
        grid=(b // bb, t // tc),
        in_specs=[
            pl.BlockSpec((bb, tc, W_MIX), lambda i, j: (i, j, 0)),
            pl.BlockSpec((bb, SUBLANES, W_MIX), lambda i, j: (i, 0, 0)),
            pl.BlockSpec((bb, 1, W_MIX), lambda i, j: (i, 0, 0)),
            pl.BlockSpec((LRU_CONV, W_MIX), lambda i, j: (0, 0)),
            vec, mat, vec, mat, vec, vec, vec,
        ],
        out_specs=[
            pl.BlockSpec((bb, tc, W_MIX), lambda i, j: (i, j, 0)),
            pl.BlockSpec((bb, 1, W_MIX), lambda i, j: (i, 0, 0)),
        ],
        out_shape=[jax.ShapeDtypeStruct((b, t, W_MIX), y_dtype), jax.ShapeDtypeStruct((b, 1, W_MIX), f32)],
        scratch_shapes=[pltpu.VMEM((bb, SUBLANES, W_MIX), f32), pltpu.VMEM((bb, 1, W_MIX), f32)],
        compiler_params=_params("parallel", "arbitrary"),
        name="rglru",
    )(za, conv0, h0, p["conv_w"], p["conv_b"], p["wa"], p["ba"], p["wx"], p["bx"], p["lam"], p["norm"])


def _swa_body(q_ref, ko_ref, vo_ref, kp_ref, vp_ref, bias_ref, sink_ref, gn_ref, y_ref, *, bb, tq, mask_first):
    n = pl.program_id(1)
    nk = WINDOW + tq
    kw = SWA_KV_HEADS * SWA_HD
    lo = lax.broadcasted_iota(jnp.int32, (1, kw), 1) < SWA_HD
    row_head = lax.broadcasted_iota(jnp.int32, (SWA_HEADS * tq, 1), 0) // tq
    sink = jnp.zeros((SWA_HEADS * tq, 1), f32)
    for h in range(SWA_HEADS):
        sink = jnp.where(row_head == h, sink_ref[:, h:h + 1], sink)
    bias = bias_ref[...]
    if mask_first:
        col = lax.broadcasted_iota(jnp.int32, (1, nk), 1)
        bias = jnp.where((n > 0) | (col >= WINDOW), bias, -jnp.inf)
    swap = lambda x: pltpu.roll(x, SWA_HD, 1)
    qs = []
    for i in range(bb):
        q = q_ref[i] * (SWA_HD ** -0.5)
        q01, q23 = q[:, 0:kw], q[:, kw:2 * kw]
        qs.append(jnp.concatenate([jnp.where(lo, q01, 0.0), jnp.where(lo, swap(q01), 0.0),
                                   jnp.where(lo, 0.0, swap(q23)), jnp.where(lo, 0.0, q23)], axis=0))
    s = [_mm(qs[i], jnp.concatenate([kp_ref[i], ko_ref[i]], axis=0), NT) + bias for i in range(bb)]
    m = [jnp.maximum(jnp.max(x, axis=-1, keepdims=True), sink) for x in s]
    e = [jnp.exp(x - mx) for x, mx in zip(s, m)]
    den = [jnp.sum(x, axis=-1, keepdims=True) + jnp.exp(sink - mx) for x, mx in zip(e, m)]
    pv = [_mm(e[i], jnp.concatenate([vp_ref[i], vo_ref[i]], axis=0)) / den[i] for i in range(bb)]
    for i in range(bb):
        p = pv[i]
        o01 = jnp.where(lo, p[0:tq], swap(p[tq:2 * tq]))
        o23 = jnp.where(lo, swap(p[2 * tq:3 * tq]), p[3 * tq:4 * tq])
        y_ref[i] = _rms(jnp.concatenate([o01, o23], axis=1), gn_ref[...]).astype(y_ref.dtype)


def _swa(zb, k_prev, v_prev, layer, bias, sinks, gn, *, bb, tq, own_prev, y_dtype):
    b, t, _ = zb.shape
    kw = SWA_KV_HEADS * SWA_HD
    if own_prev:
        kp_spec = pl.BlockSpec((None, bb, WINDOW, kw), lambda i, n: (0, i, jnp.maximum(n - 1, 0), 2))
        vp_spec = pl.BlockSpec((None, bb, WINDOW, kw), lambda i, n: (0, i, jnp.maximum(n - 1, 0), 3))
    else:
        kp_spec = pl.BlockSpec((None, bb, WINDOW, kw), lambda i, n: (layer, i, 0, 0))
        vp_spec = pl.BlockSpec((None, bb, WINDOW, kw), lambda i, n: (layer, i, 0, 0))
    return pl.pallas_call(
        functools.partial(_swa_body, bb=bb, tq=tq, mask_first=own_prev),
        grid=(b // bb, t // tq),
        in_specs=[
            pl.BlockSpec((bb, tq, W_MIX), lambda i, n: (i, n, 0)),
            pl.BlockSpec((bb, tq, kw), lambda i, n: (i, n, 2)),
            pl.BlockSpec((bb, tq, kw), lambda i, n: (i, n, 3)),
            kp_spec, vp_spec,
            pl.BlockSpec(bias.shape, lambda i, n: (0, 0)),
            pl.BlockSpec((1, SWA_HEADS), lambda i, n: (0, 0)),
            pl.BlockSpec((1, W_MIX), lambda i, n: (0, 0)),
        ],
        out_specs=pl.BlockSpec((bb, tq, W_MIX), lambda i, n: (i, n, 0)),
        out_shape=jax.ShapeDtypeStruct((b, t, W_MIX), y_dtype),
        compiler_params=_params("parallel", "arbitrary"),
        name="swa",
    )(zb, zb, zb, k_prev, v_prev, bias, sinks, gn)


def _col_bcast(row):
    n = row.shape[-1]
    return jnp.broadcast_to(row, (n, n)).T


def _gla_body(q_ref, k_ref, v_ref, t_ref, s0_ref, wup_ref, gb_ref, gn_ref, acc_ref, y_ref, sn_ref, st_ref,
              *, bb, tc, chunk, t_valid, masked):
    del acc_ref
    j = pl.program_id(1)
    n = bb * tc
    qk = GLA_HEADS * GLA_DK
    sr = lax.broadcasted_iota(jnp.int32, (qk, W_MIX), 0) // GLA_DK
    sc = lax.broadcasted_iota(jnp.int32, (qk, W_MIX), 1) // GLA_DV
    diag = sr == sc

    @pl.when(j == 0)
    def _():
        for i in range(bb):
            x = s0_ref[i]
            st_ref[i] = jnp.where(diag, jnp.concatenate([x] * GLA_HEADS, axis=1), 0.0)

    la = jax.nn.log_sigmoid(_mm(t_ref[...].reshape(n, LANES), wup_ref[...], passes=3) + gb_ref[...]) / GLA_TAU
    k = k_ref[...].reshape(n, qk)
    v = v_ref[...].reshape(n, W_MIX)
    if masked:
        live = lax.broadcasted_iota(jnp.int32, (n, 1), 0) % tc + j * tc < t_valid
        la, k, v = (jnp.where(live, x, 0.0) for x in (la, k, v))
    bc = _seg_cumsum(la, chunk)
    q_dec = q_ref[...].reshape(n, qk) * (GLA_DK ** -0.5) * jnp.exp(bc)
    k_inv = k * jnp.exp(-bc)
    khead = lax.broadcasted_iota(jnp.int32, (1, qk), 1) // GLA_DK
    vhead = lax.broadcasted_iota(jnp.int32, (1, W_MIX), 1) // GLA_DV
    r2 = lax.broadcasted_iota(jnp.int32, (GLA_HEADS * tc, tc), 0) % tc
    c2 = lax.broadcasted_iota(jnp.int32, (GLA_HEADS * tc, tc), 1)
    causal = (r2 // chunk == c2 // chunk) & (r2 >= c2)
    nch = tc // chunk
    ts = [slice(i * tc, (i + 1) * tc) for i in range(bb)]
    qs = [jnp.concatenate([jnp.where(khead == h, q_dec[t], 0.0) for h in range(GLA_HEADS)], axis=0) for t in ts]
    att = [jnp.where(causal, _mm(qs[i], k_inv[ts[i]], NT), 0.0) for i in range(bb)]
    full = [_mm(att[i], v[ts[i]]) for i in range(bb)]
    kv = []
    for c in range(nch):
        kv_c = []
        for i in range(bb):
            rs = slice(i * tc + c * chunk, i * tc + (c + 1) * chunk)
            b_end = bc[i * tc + (c + 1) * chunk - 1:i * tc + (c + 1) * chunk, :]
            k_end = k[rs] * jnp.exp(b_end - bc[rs])
            dec = _col_bcast(jnp.exp(b_end))
            kv_c.append((jnp.concatenate([dec] * (W_MIX // qk), axis=1),
                         jnp.where(diag, _mm(k_end, v[rs], TN), 0.0)))
        kv.append(kv_c)
    st = [st_ref[i] for i in range(bb)]
    inter = [[] for _ in range(bb)]
    for c in range(nch):
        for i in range(bb):
            inter[i].append(_mm(q_dec[i * tc + c * chunk:i * tc + (c + 1) * chunk], st[i]))
            st[i] = st[i] * kv[c][i][0] + kv[c][i][1]
    outs = []
    for i in range(bb):
        st_ref[i] = st[i]
        o = jnp.where(vhead == 0, full[i][0:tc], 0.0)
        for h in range(1, GLA_HEADS):
            o = o + jnp.where(vhead == h, full[i][h * tc:(h + 1) * tc], 0.0)
        outs.append(o + _cat(inter[i], 0))
    o = _cat(outs, 0)
    ms = _mm_exact_rhs(o * o, _head_ones(W_MIX, GLA_DV)) / GLA_DV
    y_ref[...] = (o * lax.rsqrt(ms + RMS_EPS) * gn_ref[...]).reshape(bb, tc, W_MIX).astype(y_ref.dtype)

    @pl.when(j == pl.num_programs(1) - 1)
    def _():
        for i in range(bb):
            x = st_ref[i]
            sn_ref[i] = sum(x[:, h * GLA_DV:(h + 1) * GLA_DV] for h in range(1, GLA_HEADS)) + x[:, 0:GLA_DV]


def _gla(zc, zd, s0, layer, acc, layer_out, p, *, bb, tc, chunk, t_valid, y_dtype):
    b, t, _ = zc.shape
    qk = GLA_HEADS * GLA_DK
    state_in = pl.BlockSpec((None, bb, qk, GLA_DV), lambda i, j: (layer, i, 0, 0))
    state_out = pl.BlockSpec((None, bb, qk, GLA_DV), lambda i, j: (layer_out, i, 0, 0))
    return pl.pallas_call(
        functools.partial(_gla_body, bb=bb, tc=tc, chunk=chunk, t_valid=t_valid, masked=t_valid < t),
        grid=(b // bb, t // tc),
        in_specs=[
            pl.BlockSpec((bb, tc, qk), lambda i, j: (i, j, 0)),
            pl.BlockSpec((bb, tc, qk), lambda i, j: (i, j, 1)),
            pl.BlockSpec((bb, tc, W_MIX), lambda i, j: (i, j, 1)),
            pl.BlockSpec((bb, tc, LANES), lambda i, j: (i, j, ZD_TAIL // LANES)),
            state_in,
            pl.BlockSpec((LANES, qk), lambda i, j: (0, 0)),
            pl.BlockSpec((1, qk), lambda i, j: (0, 0)),
            pl.BlockSpec((1, W_MIX), lambda i, j: (0, 0)),
            pl.BlockSpec(memory_space=pl.ANY),
        ],
        out_specs=[pl.BlockSpec((bb, tc, W_MIX), lambda i, j: (i, j, 0)), state_out],
        out_shape=[jax.ShapeDtypeStruct((b, t, W_MIX), y_dtype), jax.ShapeDtypeStruct(acc.shape, f32)],
        input_output_aliases={8: 1},
        scratch_shapes=[pltpu.VMEM((bb, qk, W_MIX), f32)],
        compiler_params=_params("parallel", "arbitrary"),
        name="gla",
    )(zc, zc, zc, zd, s0, p["w_up"], p["b"], p["norm"], acc)


def _rwkv_body(u_ref, sh0_ref, s0_ref, mu_ref, w0_ref, w2_ref, a0_ref, a2_ref, kk_ref, ka_ref, rk_ref,
               lw_ref, lb_ref, acc_ref, y_ref, sn_ref, st_ref, up_ref, *, bb, tc, chunk, t_valid, masked):
    del acc_ref
    j = pl.program_id(1)

    @pl.when(j == 0)
    def _():
        for i in range(bb):
            for h in range(RWKV_HEADS):
                st_ref[i, h] = s0_ref[i, h].T
        up_ref[...] = sh0_ref[...]

    n = bb * tc
    u = u_ref[...].reshape(n, ZD_W)
    rolled = pltpu.roll(u, 1, 0)
    first = lax.broadcasted_iota(jnp.int32, (SUBLANES, 1), 0) == 0
    parts = []
    for i in range(bb):
        parts.append(jnp.where(first, up_ref[i], rolled[i * tc:i * tc + SUBLANES]))
        if tc > SUBLANES:
            parts.append(rolled[i * tc + SUBLANES:(i + 1) * tc])
        up_ref[i] = u[(i + 1) * tc - 1:(i + 1) * tc, :]
    u_prev = _cat(parts, 0)
    us = u + (u_prev - u) * mu_ref[...]
    r = us[:, 0:W_MIX]
    k = us[:, W_MIX:2 * W_MIX]
    v = us[:, 2 * W_MIX:3 * W_MIX]
    tail = us[:, ZD_TAIL:ZD_TAIL + LANES]
    z = w0_ref[...] + _mm(jnp.tanh(tail), w2_ref[...], passes=3)
    lw = -math.exp(-0.5) * jax.nn.sigmoid(z)
    a = jax.nn.sigmoid(a0_ref[...] + _mm(tail, a2_ref[...], passes=3))
    ones = _head_ones(W_MIX, RWKV_N)
    kk = k * kk_ref[...]
    kk = kk * lax.rsqrt(jnp.maximum(_mm_exact_rhs(kk * kk, ones), 1e-24))
    k = k * (1.0 + (a - 1.0) * ka_ref[...])
    aa, bl, kl, vl = -kk, kk * a, k, v
    if masked:
        live = lax.broadcasted_iota(jnp.int32, (n, 1), 0) % tc + j * tc < t_valid
        lw, aa, bl, kl, vl = (jnp.where(live, x, 0.0) for x in (lw, aa, bl, kl, vl))

    cum = _seg_cumsum(lw, chunk)
    gi = jnp.exp(-cum)
    a_t = aa * jnp.exp(cum - lw)
    b_t = bl * gi
    k_t = kl * gi
    r_t = r * jnp.exp(cum)
    nch = tc // chunk
    r2 = lax.broadcasted_iota(jnp.int32, (tc, tc), 0)
    c2 = lax.broadcasted_iota(jnp.int32, (tc, tc), 1)
    same = r2 // chunk == c2 // chunk
    strict = same & (r2 > c2)
    incl = same & (r2 >= c2)

    chains = [(i, h) for i in range(bb) for h in range(RWKV_HEADS)]
    nq = len(chains)
    sl = lambda x, i, h: x[i * tc:(i + 1) * tc, h * RWKV_N:(h + 1) * RWKV_N]
    ah = [sl(a_t, i, h) for i, h in chains]
    rh = [sl(r_t, i, h) for i, h in chains]
    vh = [sl(vl, i, h) for i, h in chains]
    pair = [_mm(jnp.concatenate([ah[q], rh[q]], axis=0),
                jnp.concatenate([sl(b_t, i, h), sl(k_t, i, h)], axis=0), NT) for q, (i, h) in enumerate(chains)]
    pw = [jnp.where(strict, m[0:tc, 0:tc], 0.0) for m in pair]
    m_rb = [jnp.where(incl, m[tc:2 * tc, 0:tc], 0.0) for m in pair]
    akrk = [jnp.concatenate([jnp.where(strict, m[0:tc, tc:2 * tc], 0.0),
                             jnp.where(incl, m[tc:2 * tc, tc:2 * tc], 0.0)], axis=0) for m in pair]
    both_v = [_mm(akrk[q], vh[q]) for q in range(nq)]
    sol = [jnp.concatenate([ah[q], both_v[q][0:tc]], axis=1) for q in range(nq)]
    y_v = [x[tc:2 * tc] for x in both_v]
    wide = sol[0].shape[1]
    d = 1
    while True:
        last = 2 * d >= chunk
        res = [_mm(pw[q], sol[q] if last else jnp.concatenate([sol[q], pw[q]], axis=1)) for q in range(nq)]
        sol = [sol[q] + res[q][:, 0:wide] for q in range(nq)]
        if last:
            break
        pw = [x[:, wide:wide + tc] for x in res]
        d *= 2
    lhs_c, p2_c, mix_c, yv_c, kv_c, gcol_c = [], [], [], [], [], []
    for c in range(nch):
        rs = slice(c * chunk, (c + 1) * chunk)
        ce = [sl(cum, i, h)[(c + 1) * chunk - 1:(c + 1) * chunk, :] for i, h in chains]
        dec = [jnp.exp(ce[q] - sl(cum, i, h)[rs]) for q, (i, h) in enumerate(chains)]
        lhs_c.append([jnp.concatenate([sol[q][rs, 0:RWKV_N], rh[q][rs]], axis=0) for q in range(nq)])
        p2_c.append([sol[q][rs, RWKV_N:2 * RWKV_N] for q in range(nq)])
        mix_c.append([jnp.concatenate([m_rb[q][rs, rs], (sl(bl, i, h)[rs] * dec[q]).T], axis=0)
                      for q, (i, h) in enumerate(chains)])
        yv_c.append([y_v[q][rs] for q in range(nq)])
        kv_c.append([_mm(sl(kl, i, h)[rs] * dec[q], vh[q][rs], TN) for q, (i, h) in enumerate(chains)])
        gcol_c.append([jnp.broadcast_to(jnp.exp(ce[q]), (RWKV_N, RWKV_N)).T for q in range(nq)])
    st = [st_ref[i, h] for i, h in chains]
    y_parts = [[] for _ in chains]
    for c in range(nch):
        both = [_mm(lhs_c[c][q], st[q]) for q in range(nq)]
        u_c = [both[q][0:chunk] + p2_c[c][q] for q in range(nq)]
        mix = [_mm(mix_c[c][q], u_c[q]) for q in range(nq)]
        st = [gcol_c[c][q] * st[q] + kv_c[c][q] + mix[q][chunk:chunk + RWKV_N] for q in range(nq)]
        for q in range(nq):
            y_parts[q].append(both[q][chunk:2 * chunk] + mix[q][0:chunk] + yv_c[c][q])
    for q, (i, h) in enumerate(chains):
        st_ref[i, h] = st[q]
    y = _cat([_cat([_cat(y_parts[i * RWKV_HEADS + h], 0) for h in range(RWKV_HEADS)], 1) for i in range(bb)], 0)
    mean = _mm_exact_rhs(y, ones) / RWKV_N
    yc = y - mean
    var = _mm_exact_rhs(yc * yc, ones) / RWKV_N
    out = yc * lax.rsqrt(var + RWKV_LN_EPS) * lw_ref[...] + lb_ref[...]
    bonus = _mm_exact_rhs(r * k * rk_ref[...], ones)
    y_ref[...] = (out + bonus * v).reshape(bb, tc, W_MIX).astype(y_ref.dtype)

    @pl.when(j == pl.num_programs(1) - 1)
    def _():
        for i in range(bb):
            for h in range(RWKV_HEADS):
                sn_ref[i, h] = st_ref[i, h].T


def _rwkv(zd, shift0, s0, layer, acc, layer_out, p, *, bb, tc, chunk, t_valid, y_dtype):
    b, t, _ = zd.shape
    vec = lambda n: pl.BlockSpec((1, n), lambda i, j: (0, 0))
    lora = pl.BlockSpec((LANES, W_MIX), lambda i, j: (0, 0))
    state_in = pl.BlockSpec((None, bb, RWKV_HEADS, RWKV_N, RWKV_N), lambda i, j: (layer, i, 0, 0, 0))
    state_out = pl.BlockSpec((None, bb, RWKV_HEADS, RWKV_N, RWKV_N), lambda i, j: (layer_out, i, 0, 0, 0))
    return pl.pallas_call(
        functools.partial(_rwkv_body, bb=bb, tc=tc, chunk=chunk, t_valid=t_valid, masked=t_valid < t),
        grid=(b // bb, t // tc),
        in_specs=[
            pl.BlockSpec((bb, tc, ZD_W), lambda i, j: (i, j, 0)),
            pl.BlockSpec((bb, 1, ZD_W), lambda i, j: (i, 0, 0)),
            state_in,
            vec(ZD_W), vec(W_MIX), lora, vec(W_MIX), lora, vec(W_MIX), vec(W_MIX), vec(W_MIX),
            vec(W_MIX), vec(W_MIX),
            pl.BlockSpec(memory_space=pl.ANY),
        ],
        out_specs=[pl.BlockSpec((bb, tc, W_MIX), lambda i, j: (i, j, 0)), state_out],
        out_shape=[jax.ShapeDtypeStruct((b, t, W_MIX), y_dtype), jax.ShapeDtypeStruct(acc.shape, f32)],
        input_output_aliases={13: 1},
        scratch_shapes=[pltpu.VMEM((bb, RWKV_HEADS, RWKV_N, RWKV_N), f32), pltpu.VMEM((bb, 1, ZD_W), f32)],
        compiler_params=_params("parallel", "arbitrary"),
        name="rwkv7",
    )(zd, shift0, s0, p["mu"], p["w0"], p["w2"], p["a0"], p["a2"], p["k_k"], p["k_a"], p["r_k"],
      p["ln_w"], p["ln_b"], acc)


def _block_diag(w):
    n, c, _ = w.shape
    out = jnp.zeros((n * c, n * c), w.dtype)
    for i in range(n):
        out = out.at[i * c:(i + 1) * c, i * c:(i + 1) * c].set(w[i])
    return out


def _rel_bucket_ids(tq):
    dist = WINDOW + np.arange(tq)[:, None] - np.arange(WINDOW + tq)[None, :]
    max_exact = REL_BUCKETS // 2
    n = np.maximum(dist, 0)
    log_ratio = (np.log(np.maximum(n, 1).astype(np.float32) / np.float32(max_exact))
                 / np.float32(math.log(REL_MAX_DIST / max_exact)))
    large = np.minimum(max_exact + (log_ratio * np.float32(REL_BUCKETS - max_exact)).astype(np.int32),
                       REL_BUCKETS - 1)
    bucket = np.where(n < max_exact, n, large)
    return np.where((dist >= 0) & (dist <= WINDOW), bucket, -1).astype(np.int32)


def _swa_bias(table, tq):
    ids = jnp.asarray(_rel_bucket_ids(tq))[None]
    out = jnp.full((SWA_HEADS, tq, WINDOW + tq), -jnp.inf, f32)
    for k in range(REL_BUCKETS):
        out = jnp.where(ids == k, table[k].astype(f32)[:, None, None], out)
    return out.reshape(SWA_HEADS * tq, WINDOW + tq)


def _pack_layer(i, w_in, lru_conv_w, lru_conv_b, lru_wa, lru_ba, lru_wx, lru_bx, lru_lambda, lru_norm,
                swa_sinks, swa_norm, gla_w_up, gla_b, gla_norm, rwkv_mu, rwkv_w0, rwkv_w2, rwkv_a0, rwkv_a2,
                rwkv_k_k, rwkv_k_a, rwkv_r_k, rwkv_ln_w, rwkv_ln_b, w_out, ple_proj, ple_gate,
                norm_pre, norm_post):
    w = w_in[i]
    c_ca = 5 * W_MIX
    c_ud = c_ca + GLA_RANK
    c_gate = c_ud + RWKV_SHIFT
    pad = jnp.zeros((D_MODEL, ZD_W - RWKV_SHIFT - GLA_RANK), w.dtype)
    w_packed = jnp.concatenate([w[:, :c_ca], w[:, c_ud:c_gate], w[:, c_ca:c_ud], pad, w[:, c_gate:]], axis=1)
    row = lambda a: a.reshape(1, -1)
    lora_lo = RWKV_SHIFT - 2 * RWKV_LORA - ZD_TAIL
    ca_lo = RWKV_SHIFT - ZD_TAIL
    zeros_tail = jnp.zeros((LANES, W_MIX), f32)
    return dict(
        w_in=w_packed.astype(bf16),
        norm_pre=row(norm_pre[i]), norm_post=row(norm_post[i]),
        w_out=w_out[i].astype(bf16), ple_proj=ple_proj[i].astype(bf16), ple_gate=ple_gate[i].astype(bf16),
        lru=dict(conv_w=lru_conv_w[i], conv_b=row(lru_conv_b[i]), wa=_block_diag(lru_wa[i]), ba=row(lru_ba[i]),
                 wx=_block_diag(lru_wx[i]), bx=row(lru_bx[i]), lam=row(lru_lambda[i]), norm=row(lru_norm[i])),
        swa_sinks=row(swa_sinks[i]), swa_norm=row(swa_norm[i]),
        gla=dict(w_up=jnp.zeros((LANES, GLA_HEADS * GLA_DK), f32).at[ca_lo:ca_lo + GLA_RANK].set(gla_w_up[i]),
                 b=row(gla_b[i]), norm=row(jnp.tile(gla_norm[i], GLA_HEADS))),
        rwkv=dict(mu=row(jnp.pad(rwkv_mu[i], (0, ZD_W - RWKV_SHIFT))), w0=row(rwkv_w0[i]),
                  w2=zeros_tail.at[lora_lo:lora_lo + RWKV_LORA].set(rwkv_w2[i]), a0=row(rwkv_a0[i]),
                  a2=zeros_tail.at[lora_lo + RWKV_LORA:lora_lo + 2 * RWKV_LORA].set(rwkv_a2[i]),
                  k_k=row(rwkv_k_k[i]), k_a=row(rwkv_k_a[i]), r_k=row(rwkv_r_k[i]),
                  ln_w=row(rwkv_ln_w[i]), ln_b=row(rwkv_ln_b[i])),
    )


PROMPT_TILES = dict(lru=(8, 256), swa=(16, WINDOW), gla=(8, 128, 32), rwkv=(4, 128, 64))


def _sample_tiles(t_pad):
    return dict(lru=(32, t_pad), swa=(16, t_pad), gla=(16, t_pad, t_pad), rwkv=(16, t_pad, t_pad))


def _layer(x, pe, layer, lp, bias, state, acc, *, tiles, t_valid):
    b, t, _ = x.shape
    prompt = state is None
    rows = b * t
    act = bf16 if prompt else f32
    za, zb, zc, zd, zg = _inproj(x.reshape(rows, D_MODEL), lp["norm_pre"], lp["w_in"], act)
    za = za.reshape(b, t, ZA_W)
    zb = zb.reshape(b, t, ZB_W)
    zc = zc.reshape(b, t, ZC_W)
    zd = zd.reshape(b, t, ZD_W)
    if prompt:
        h0 = jnp.zeros((b, 1, W_MIX), f32)
        conv0 = jnp.zeros((b, SUBLANES, W_MIX), f32)
        kbuf = vbuf = zb[None]
        sg0 = jnp.zeros((1, b, GLA_HEADS * GLA_DK, GLA_DV), f32)
        sr0 = jnp.zeros((1, b, RWKV_HEADS, RWKV_N, RWKV_N), f32)
        sh0 = jnp.zeros((b, 1, ZD_W), f32)
        st_layer = 0
    else:
        h0, conv0, kbuf, vbuf, sg0, sr0, sh0 = state
        h0 = h0[layer].reshape(b, 1, W_MIX)
        conv0 = jnp.pad(conv0[layer], ((0, 0), (SUBLANES - (LRU_CONV - 1), 0), (0, 0)))
        sh0 = jnp.pad(sh0[layer], ((0, 0), (0, ZD_W - RWKV_SHIFT))).reshape(b, 1, ZD_W)
        st_layer = layer
    bb, tc = tiles["lru"]
    ya, h_new = _lru(za, conv0, h0, lp["lru"], bb=bb, tc=tc, t_valid=t_valid, reset_first=prompt, y_dtype=act)
    bb, tq = tiles["swa"]
    yb = _swa(zb, kbuf, vbuf, st_layer, bias, lp["swa_sinks"], lp["swa_norm"], bb=bb, tq=tq, own_prev=prompt,
              y_dtype=act)
    bb, tc, chunk = tiles["gla"]
    yc, acc_g = _gla(zc, zd, sg0, st_layer, acc[0], layer, lp["gla"], bb=bb, tc=tc, chunk=chunk, t_valid=t_valid,
                     y_dtype=act)
    bb, tc, chunk = tiles["rwkv"]
    yd, acc_r = _rwkv(zd, sh0, sr0, st_layer, acc[1], layer, lp["rwkv"], bb=bb, tc=tc, chunk=chunk, t_valid=t_valid,
                      y_dtype=act)
    flat = lambda a: a.reshape(rows, a.shape[-1])
    x_new = _outproj(flat(ya), flat(yb), flat(yc), flat(yd), zg, flat(x), pe.reshape(pe.shape[0], rows, D_PLE), layer,
                     lp["w_out"], lp["norm_post"], lp["ple_proj"], lp["ple_gate"]).reshape(b, t, D_MODEL)
    kv_rows = slice(t_valid - WINDOW, t_valid) if prompt else slice(0, t_valid)
    kw = SWA_KV_HEADS * SWA_HD
    k_new = zb[:, kv_rows, W_MIX:W_MIX + kw].reshape(b, -1, SWA_KV_HEADS, SWA_HD)
    v_new = zb[:, kv_rows, W_MIX + kw:W_MIX + 2 * kw].reshape(b, -1, SWA_KV_HEADS, SWA_HD)
    new_state = (h_new.reshape(b, W_MIX), za[:, t_valid - (LRU_CONV - 1):t_valid], k_new, v_new,
                 zd[:, t_valid - 1, :RWKV_SHIFT])
    return x_new, new_state, (acc_g, acc_r)


def kernel(x_prompt, x_sample, state_lru_h, state_lru_conv, cache_swa_k, cache_swa_v, state_gla, state_rwkv, state_rwkv_shift, p_prompt, p_sample, rel_bias, norm_pre, norm_post, w_in, lru_conv_w, lru_conv_b, lru_wa, lru_ba, lru_wx, lru_bx, lru_lambda, lru_norm, swa_sinks, swa_norm, gla_w_up, gla_b, gla_norm, rwkv_mu, rwkv_w0, rwkv_w2, rwkv_a0, rwkv_a2, rwkv_k_k, rwkv_k_a, rwkv_r_k, rwkv_ln_w, rwkv_ln_b, w_out, ple_proj, ple_gate):
    depth = w_in.shape[0]
    t_p = x_prompt.shape[1]
    t_s = x_sample.shape[1]
    t_s_pad = -(-t_s // SUBLANES) * SUBLANES
    pad_t = lambda a, axis: jnp.pad(a, [(0, t_s_pad - t_s) if d == axis else (0, 0) for d in range(a.ndim)])
    xs = pad_t(x_sample, 1)
    ps = pad_t(p_sample, 2)
    xp = x_prompt
    bias_p = _swa_bias(rel_bias, WINDOW)
    bias_s = _swa_bias(rel_bias, t_s_pad)
    tiles_s = _sample_tiles(t_s_pad)
    n_l, b_s = cache_swa_k.shape[:2]
    kw = SWA_KV_HEADS * SWA_HD
    state_s = (state_lru_h, state_lru_conv, cache_swa_k.reshape(n_l, b_s, WINDOW, kw),
               cache_swa_v.reshape(n_l, b_s, WINDOW, kw),
               state_gla.reshape(n_l, b_s, GLA_HEADS * GLA_DK, GLA_DV), state_rwkv, state_rwkv_shift)
    acc_for = lambda b: (jnp.zeros((depth, b, GLA_HEADS * GLA_DK, GLA_DV), f32),
                         jnp.zeros((depth, b, RWKV_HEADS, RWKV_N, RWKV_N), f32))
    acc_p, acc_s = acc_for(x_prompt.shape[0]), acc_for(b_s)
    sts_p, sts_s = [], []
    for i in range(depth):
        lp = _pack_layer(i, w_in, lru_conv_w, lru_conv_b, lru_wa, lru_ba, lru_wx, lru_bx, lru_lambda, lru_norm,
                         swa_sinks, swa_norm, gla_w_up, gla_b, gla_norm, rwkv_mu, rwkv_w0, rwkv_w2, rwkv_a0,
                         rwkv_a2, rwkv_k_k, rwkv_k_a, rwkv_r_k, rwkv_ln_w, rwkv_ln_b, w_out, ple_proj, ple_gate,
                         norm_pre, norm_post)
        xp, st_p, acc_p = _layer(xp, p_prompt, i, lp, bias_p, None, acc_p, tiles=PROMPT_TILES, t_valid=t_p)
        xs, st_s, acc_s = _layer(xs, ps, i, lp, bias_s, state_s, acc_s, tiles=tiles_s, t_valid=t_s)
        sts_p.append(st_p)
        sts_s.append(st_s)
    hp, cp, kp, vp, shp = [jnp.stack([s[j] for s in sts_p]) for j in range(5)]
    hs, cs, ks, vs, shs = [jnp.stack([s[j] for s in sts_s]) for j in range(5)]
    unpack = lambda a: a.reshape(a.shape[:2] + (GLA_HEADS, GLA_DK, GLA_DV))
    gp, rp = unpack(acc_p[0]), acc_p[1]
    gs, rs = unpack(acc_s[0]), acc_s[1]
    return (xp, xs[:, :t_s], hp, hs, cp, cs, kp, ks, vp, vs, gp, gs, rp, rs, shp, shs)
```

```python
import functools
import math

import jax
import jax.numpy as jnp
import numpy as np
from jax import lax
from jax.experimental import pallas as pl
from jax.experimental.pallas import tpu as pltpu

f32 = jnp.float32
bf16 = jnp.bfloat16

D_MODEL = 1024
D_PLE = 256
RMS_EPS = 1e-6
W_MIX = 256
LRU_BLOCKS = 4
LRU_CONV = 4
LRU_C = 8.0
SWA_HEADS = 4
SWA_KV_HEADS = 2
SWA_HD = 64
WINDOW = 128
REL_BUCKETS = 32
REL_MAX_DIST = 128
GLA_HEADS = 4
GLA_DK = 32
GLA_DV = 64
GLA_RANK = 16
GLA_TAU = 16.0
RWKV_HEADS = 4
RWKV_N = 64
RWKV_LORA = 32
RWKV_SHIFT = 3 * W_MIX + 2 * RWKV_LORA
RWKV_LN_EPS = 64e-5
RWKV_KK_EPS = 1e-12

ZA_W = 256
ZB_W = 512
ZC_W = 512
ZD_W = 896
ZG_W = 1024
ZD_TAIL = 768
N_IN_PACKED = ZA_W + ZB_W + ZC_W + ZD_W + ZG_W

LANES = 128
SUBLANES = 8
VMEM_LIMIT = 48 * 1024 * 1024

ROW_TILE = 1024
ROW_SPLIT = 2

NN = (((1,), (0,)), ((), ()))
NT = (((1,), (1,)), ((), ()))
TN = (((0,), (0,)), ((), ()))


def _dot(a, b, dn):
    return lax.dot_general(a, b, dn, preferred_element_type=f32)


def _split2(x):
    hi = x.astype(bf16)
    lo = (x - hi.astype(f32)).astype(bf16)
    return hi, lo


def _mm(a, b, dn=NN, passes=1):
    if passes == 1:
        return _dot(a.astype(bf16), b.astype(bf16), dn)
    ah, al = _split2(a)
    bh, bl = _split2(b)
    return _dot(ah, bh, dn) + (_dot(ah, bl, dn) + _dot(al, bh, dn))


def _mm_exact_rhs(a, b_bf16):
    hi, lo = _split2(a)
    return _dot(hi, b_bf16, NN) + _dot(lo, b_bf16, NN)


def _head_ones(n, width):
    r = lax.broadcasted_iota(jnp.int32, (n, n), 0) // width
    c = lax.broadcasted_iota(jnp.int32, (n, n), 1) // width
    return jnp.where(r == c, 1.0, 0.0).astype(bf16)


def _rms(x, g):
    return x * lax.rsqrt(jnp.mean(x * x, axis=-1, keepdims=True) + RMS_EPS) * g


def _seg_cumsum(x, seg):
    n, w = x.shape
    x3 = x.reshape(n // SUBLANES, SUBLANES, w)
    sub = lax.broadcasted_iota(jnp.int32, (1, SUBLANES, w), 1)
    d = 1
    while d < SUBLANES:
        x3 = x3 + jnp.where(sub >= d, pltpu.roll(x3, d, 1), 0.0)
        d *= 2
    x = x3.reshape(n, w)
    rows = lax.broadcasted_iota(jnp.int32, (n, 1), 0) % seg
    tot = jnp.broadcast_to(x3[:, SUBLANES - 1:SUBLANES, :], x3.shape).reshape(n, w)
    while d < seg:
        add = jnp.where(rows >= d, pltpu.roll(tot, d, 0), 0.0)
        x = x + add
        tot = tot + add
        d *= 2
    return x


def _cat(xs, axis):
    return xs[0] if len(xs) == 1 else jnp.concatenate(xs, axis=axis)


def _params(*sem):
    return pltpu.CompilerParams(dimension_semantics=sem, vmem_limit_bytes=VMEM_LIMIT)


def _inproj_body(x_ref, g_ref, w_ref, za_ref, zb_ref, zc_ref, zd_ref, zg_ref):
    h = _rms(x_ref[...], g_ref[...]).astype(bf16)
    off = 0
    for o_ref in (za_ref, zb_ref, zc_ref, zd_ref, zg_ref):
        n = o_ref.shape[-1]
        o_ref[...] = _dot(h, w_ref[:, off:off + n], NN).astype(o_ref.dtype)
        off += n


def _inproj(x, g, w, gate_dtype):
    rows = x.shape[0]
    tm = min(ROW_TILE, rows)
    widths = (ZA_W, ZB_W, ZC_W, ZD_W, ZG_W)
    return pl.pallas_call(
        _inproj_body,
        grid=(rows // tm,),
        in_specs=[
            pl.BlockSpec((tm, D_MODEL), lambda i: (i, 0)),
            pl.BlockSpec((1, D_MODEL), lambda i: (0, 0)),
            pl.BlockSpec((D_MODEL, N_IN_PACKED), lambda i: (0, 0)),
        ],
        out_specs=[pl.BlockSpec((tm, n), lambda i: (i, 0)) for n in widths],
        out_shape=[jax.ShapeDtypeStruct((rows, n), gate_dtype if n == ZG_W else f32) for n in widths],
        compiler_params=_params("parallel"),
        name="in_proj",
    )(x, g, w)


def _outproj_body(ya_ref, yb_ref, yc_ref, yd_ref, zg_ref, x_ref, pe_ref, wo_ref, gp_ref, wp_ref, wg_ref, o_ref):
    tm = x_ref.shape[0]
    parts = [slice(r, r + tm // ROW_SPLIT) for r in range(0, tm, tm // ROW_SPLIT)]
    accs = []
    for rs in parts:
        g = zg_ref[rs, :].astype(f32)
        sg = g * jax.nn.sigmoid(g)
        acc = None
        for i, y_ref in enumerate((ya_ref, yb_ref, yc_ref, yd_ref)):
            lo = i * W_MIX
            y = (y_ref[rs, :].astype(f32) * sg[:, lo:lo + W_MIX]).astype(bf16)
            part = _dot(y, wo_ref[lo:lo + W_MIX, :], NN)
            acc = part if acc is None else acc + part
        accs.append(acc)
    x1 = [x_ref[rs, :] + _rms(acc, gp_ref[...]) for rs, acc in zip(parts, accs)]
    emb = [_dot(pe_ref[rs, :].astype(bf16), wp_ref[...], NN) for rs in parts]
    gate = [jax.nn.sigmoid(_dot(x.astype(bf16), wg_ref[...], NN)) for x in x1]
    for rs, x, e, gt in zip(parts, x1, emb, gate):
        o_ref[rs, :] = x + e * gt


def _outproj(ya, yb, yc, yd, zg, x, pe, layer, wo, gp, wp, wg):
    rows = x.shape[0]
    tm = min(ROW_TILE, rows)
    row = lambda n: pl.BlockSpec((tm, n), lambda i: (i, 0))
    full = lambda a: pl.BlockSpec(a.shape, lambda i: (0, 0))
    return pl.pallas_call(
        _outproj_body,
        grid=(rows // tm,),
        in_specs=[row(W_MIX), row(W_MIX), row(W_MIX), row(W_MIX), row(ZG_W), row(D_MODEL),
                  pl.BlockSpec((None, tm, D_PLE), lambda i: (layer, i, 0)),
                  full(wo), full(gp), full(wp), full(wg)],
        out_specs=row(D_MODEL),
        out_shape=jax.ShapeDtypeStruct((rows, D_MODEL), f32),
        compiler_params=_params("parallel"),
        name="out_proj",
    )(ya, yb, yc, yd, zg, x, pe, wo, gp, wp, wg)


def _lru_body(x_ref, c0_ref, h0_ref, cw_ref, cb_ref, wa_ref, ba_ref, wx_ref, bx_ref, lam_ref, gn_ref,
              y_ref, hl_ref, ext_ref, hc_ref, *, bb, tc, t_valid, reset_first):
    j = pl.program_id(1)

    @pl.when(j == 0)
    def _():
        ext_ref[...] = c0_ref[...]
        hc_ref[...] = h0_ref[...]

    n = bb * tc
    ng = tc // SUBLANES
    sub = lax.broadcasted_iota(jnp.int32, (1, SUBLANES, W_MIX), 1)
    x3 = x_ref[...].reshape(bb * ng, SUBLANES, W_MIX)
    xc = cw_ref[3:4, :] * x3 + cb_ref[...]
    for k in range(1, LRU_CONV):
        rot = pltpu.roll(x3, k, 1)
        before = _cat([_cat([pltpu.roll(ext_ref[i], k, 0)[None]] + ([rot[i * ng:(i + 1) * ng - 1]] if ng > 1 else []), 0)
                       for i in range(bb)], 0)
        xc = xc + cw_ref[LRU_CONV - 1 - k:LRU_CONV - k, :] * jnp.where(sub >= k, rot, before)
    for i in range(bb):
        ext_ref[i] = x3[(i + 1) * ng - 1]
    xc = xc.reshape(n, W_MIX)
    gate_r = _mm(xc, wa_ref[...]) + ba_ref[...]
    gate_i = _mm(xc, wx_ref[...]) + bx_ref[...]
    log_a = -LRU_C * jax.nn.sigmoid(gate_r) * jax.nn.softplus(-lam_ref[...])
    a = jnp.exp(log_a)
    gap = 1.0 - a * a
    mult = jnp.where(gap > 0.0, gap * lax.rsqrt(gap), 0.0)
    if reset_first:
        rows = lax.broadcasted_iota(jnp.int32, (n, W_MIX), 0)
        mult = jnp.where(rows % tc + j * tc == 0, 1.0, mult)
    b = mult * jax.nn.sigmoid(gate_i) * xc
    a = a.reshape(bb * ng, SUBLANES, W_MIX)
    b = b.reshape(bb * ng, SUBLANES, W_MIX)
    d = 1
    while d < SUBLANES:
        m = sub >= d
        b = a * jnp.where(m, pltpu.roll(b, d, 1), 0.0) + b
        a = a * jnp.where(m, pltpu.roll(a, d, 1), 1.0)
        d *= 2
    hs = []
    for i in range(bb):
        carry = jnp.broadcast_to(hc_ref[i], (SUBLANES, W_MIX))
        for g in range(ng):
            hg = a[i * ng + g] * carry + b[i * ng + g]
            hs.append(hg)
            carry = jnp.broadcast_to(hg[SUBLANES - 1:SUBLANES], (SUBLANES, W_MIX))
        hc_ref[i] = carry[0:1]
    h = _cat(hs, 0)
    y_ref[...] = _rms(h, gn_ref[...]).reshape(bb, tc, W_MIX).astype(y_ref.dtype)
    r_last = (t_valid - 1) % tc

    @pl.when(j == (t_valid - 1) // tc)
    def _():
        for i in range(bb):
            hl_ref[i] = h[i * tc + r_last:i * tc + r_last + 1, :]


def _lru(za, conv0, h0, p, *, bb, tc, t_valid, reset_first, y_dtype):
    b, t, _ = za.shape
    vec = pl.BlockSpec((1, W_MIX), lambda i, j: (0, 0))
    mat = pl.BlockSpec((W_MIX, W_MIX), lambda i, j: (0, 0))
    return pl.pallas_call(
        functools.partial(_lru_body, bb=bb, tc=tc, t_valid=t_valid, reset_first=reset_first),
        grid=(b // bb, t // tc),
        in_specs=[
            pl.BlockSpec((bb, tc, W_MIX), lambda i, j: (i, j, 0)),
            pl.BlockSpec((bb, SUBLANES, W_MIX), lambda i, j: (i, 0, 0)),
            pl.BlockSpec((bb, 1, W_MIX), lambda i, j: (i, 0, 0)),
            pl.BlockSpec((LRU_CONV, W_MIX), lambda i, j: (0, 0)),
            vec, mat, vec, mat, vec, vec, vec,
        ],
        out_specs=[
            pl.BlockSpec((bb, tc, W_MIX), lambda i, j: (i, j, 0)),
            pl.BlockSpec((bb, 1, W_MIX), lambda i, j: (i, 0, 0)),
        ],
        out_shape=[jax.ShapeDtypeStruct((b, t, W_MIX), y_dtype), jax.ShapeDtypeStruct((b, 1, W_MIX), f32)],
        scratch_shapes=[pltpu.VMEM((bb, SUBLANES, W_MIX), f32), pltpu.VMEM((bb, 1, W_MIX), f32)],
        compiler_params=_params("parallel", "arbitrary"),
        name="rglru",
    )(za, conv0, h0, p["conv_w"], p["conv_b"], p["wa"], p["ba"], p["wx"], p["bx"], p["lam"], p["norm"])


def _swa_body(q_ref, ko_ref, vo_ref, kp_ref, vp_ref, bias_ref, sink_ref, gn_ref, y_ref, *, bb, tq, mask_first):
    n = pl.program_id(1)
    nk = WINDOW + tq
    kw = SWA_KV_HEADS * SWA_HD
    lo = lax.broadcasted_iota(jnp.int32, (1, kw), 1) < SWA_HD
    row_head = lax.broadcasted_iota(jnp.int32, (SWA_HEADS * tq, 1), 0) // tq
    sink = jnp.zeros((SWA_HEADS * tq, 1), f32)
    for h in range(SWA_HEADS):
        sink = jnp.where(row_head == h, sink_ref[:, h:h + 1], sink)
    bias = bias_ref[...]
    if mask_first:
        col = lax.broadcasted_iota(jnp.int32, (1, nk), 1)
        bias = jnp.where((n > 0) | (col >= WINDOW), bias, -jnp.inf)
    swap = lambda x: pltpu.roll(x, SWA_HD, 1)
    qs = []
    for i in range(bb):
        q = q_ref[i] * (SWA_HD ** -0.5)
        q01, q23 = q[:, 0:kw], q[:, kw:2 * kw]
        qs.append(jnp.concatenate([jnp.where(lo, q01, 0.0), jnp.where(lo, swap(q01), 0.0),
                                   jnp.where(lo, 0.0, swap(q23)), jnp.where(lo, 0.0, q23)], axis=0))
    s = [_mm(qs[i], jnp.concatenate([kp_ref[i], ko_ref[i]], axis=0), NT) + bias for i in range(bb)]
    m = [jnp.maximum(jnp.max(x, axis=-1, keepdims=True), sink) for x in s]
    e = [jnp.exp(x - mx) for x, mx in zip(s, m)]
    den = [jnp.sum(x, axis=-1, keepdims=True) + jnp.exp(sink - mx) for x, mx in zip(e, m)]
    pv = [_mm(e[i], jnp.concatenate([vp_ref[i], vo_ref[i]], axis=0)) / den[i] for i in range(bb)]
    for i in range(bb):
        p = pv[i]
        o01 = jnp.where(lo, p[0:tq], swap(p[tq:2 * tq]))
        o23 = jnp.where(lo, swap(p[2 * tq:3 * tq]), p[3 * tq:4 * tq])
        y_ref[i] = _rms(jnp.concatenate([o01, o23], axis=1), gn_ref[...]).astype(y_ref.dtype)


def _swa(zb, k_prev, v_prev, layer, bias, sinks, gn, *, bb, tq, own_prev, y_dtype):
    b, t, _ = zb.shape
    kw = SWA_KV_HEADS * SWA_HD
    if own_prev:
        kp_spec = pl.BlockSpec((None, bb, WINDOW, kw), lambda i, n: (0, i, jnp.maximum(n - 1, 0), 2))
        vp_spec = pl.BlockSpec((None, bb, WINDOW, kw), lambda i, n: (0, i, jnp.maximum(n - 1, 0), 3))
    else:
        kp_spec = pl.BlockSpec((None, bb, WINDOW, kw), lambda i, n: (layer, i, 0, 0))
        vp_spec = pl.BlockSpec((None, bb, WINDOW, kw), lambda i, n: (layer, i, 0, 0))
    return pl.pallas_call(
        functools.partial(_swa_body, bb=bb, tq=tq, mask_first=own_prev),
        grid=(b // bb, t // tq),
        in_specs=[
            pl.BlockSpec((bb, tq, W_MIX), lambda i, n: (i, n, 0)),
            pl.BlockSpec((bb, tq, kw), lambda i, n: (i, n, 2)),
            pl.BlockSpec((bb, tq, kw), lambda i, n: (i, n, 3)),
            kp_spec, vp_spec,
            pl.BlockSpec(bias.shape, lambda i, n: (0, 0)),
            pl.BlockSpec((1, SWA_HEADS), lambda i, n: (0, 0)),
            pl.BlockSpec((1, W_MIX), lambda i, n: (0, 0)),
        ],
        out_specs=pl.BlockSpec((bb, tq, W_MIX), lambda i, n: (i, n, 0)),
        out_shape=jax.ShapeDtypeStruct((b, t, W_MIX), y_dtype),
        compiler_params=_params("parallel", "arbitrary"),
        name="swa",
    )(zb, zb, zb, k_prev, v_prev, bias, sinks, gn)


def _col_bcast(row):
    n = row.shape[-1]
    return jnp.broadcast_to(row, (n, n)).T


def _gla_body(q_ref, k_ref, v_ref, t_ref, s0_ref, wup_ref, gb_ref, gn_ref, acc_ref, y_ref, sn_ref, st_ref,
              *, bb, tc, chunk, t_valid, masked):
    del acc_ref
    j = pl.program_id(1)
    n = bb * tc
    qk = GLA_HEADS * GLA_DK
    sr = lax.broadcasted_iota(jnp.int32, (qk, W_MIX), 0) // GLA_DK
    sc = lax.broadcasted_iota(jnp.int32, (qk, W_MIX), 1) // GLA_DV
    diag = sr == sc

    @pl.when(j == 0)
    def _():
        for i in range(bb):
            x = s0_ref[i]
            st_ref[i] = jnp.where(diag, jnp.concatenate([x] * GLA_HEADS, axis=1), 0.0)

    la = jax.nn.log_sigmoid(_mm(t_ref[...].reshape(n, LANES), wup_ref[...], passes=3) + gb_ref[...]) / GLA_TAU
    k = k_ref[...].reshape(n, qk)
    v = v_ref[...].reshape(n, W_MIX)
    if masked:
        live = lax.broadcasted_iota(jnp.int32, (n, 1), 0) % tc + j * tc < t_valid
        la, k, v = (jnp.where(live, x, 0.0) for x in (la, k, v))
    bc = _seg_cumsum(la, chunk)
    q_dec = q_ref[...].reshape(n, qk) * (GLA_DK ** -0.5) * jnp.exp(bc)
    k_inv = k * jnp.exp(-bc)
    khead = lax.broadcasted_iota(jnp.int32, (1, qk), 1) // GLA_DK
    vhead = lax.broadcasted_iota(jnp.int32, (1, W_MIX), 1) // GLA_DV
    r2 = lax.broadcasted_iota(jnp.int32, (GLA_HEADS * tc, tc), 0) % tc
    c2 = lax.broadcasted_iota(jnp.int32, (GLA_HEADS * tc, tc), 1)
    causal = (r2 // chunk == c2 // chunk) & (r2 >= c2)
    nch = tc // chunk
    ts = [slice(i * tc, (i + 1) * tc) for i in range(bb)]
    qs = [jnp.concatenate([jnp.where(khead == h, q_dec[t], 0.0) for h in range(GLA_HEADS)], axis=0) for t in ts]
    att = [jnp.where(causal, _mm(qs[i], k_inv[ts[i]], NT), 0.0) for i in range(bb)]
    full = [_mm(att[i], v[ts[i]]) for i in range(bb)]
    kv = []
    for c in range(nch):
        kv_c = []
        for i in range(bb):
            rs = slice(i * tc + c * chunk, i * tc + (c + 1) * chunk)
            b_end = bc[i * tc + (c + 1) * chunk - 1:i * tc + (c + 1) * chunk, :]
            k_end = k[rs] * jnp.exp(b_end - bc[rs])
            dec = _col_bcast(jnp.exp(b_end))
            kv_c.append((jnp.concatenate([dec] * (W_MIX // qk), axis=1),
                         jnp.where(diag, _mm(k_end, v[rs], TN), 0.0)))
        kv.append(kv_c)
    st = [st_ref[i] for i in range(bb)]
    inter = [[] for _ in range(bb)]
    for c in range(nch):
        for i in range(bb):
            inter[i].append(_mm(q_dec[i * tc + c * chunk:i * tc + (c + 1) * chunk], st[i]))
            st[i] = st[i] * kv[c][i][0] + kv[c][i][1]
    outs = []
    for i in range(bb):
        st_ref[i] = st[i]
        o = jnp.where(vhead == 0, full[i][0:tc], 0.0)
        for h in range(1, GLA_HEADS):
            o = o + jnp.where(vhead == h, full[i][h * tc:(h + 1) * tc], 0.0)
        outs.append(o + _cat(inter[i], 0))
    o = _cat(outs, 0)
    ms = _mm_exact_rhs(o * o, _head_ones(W_MIX, GLA_DV)) / GLA_DV
    y_ref[...] = (o * lax.rsqrt(ms + RMS_EPS) * gn_ref[...]).reshape(bb, tc, W_MIX).astype(y_ref.dtype)

    @pl.when(j == pl.num_programs(1) - 1)
    def _():
        for i in range(bb):
            x = st_ref[i]
            sn_ref[i] = sum(x[:, h * GLA_DV:(h + 1) * GLA_DV] for h in range(1, GLA_HEADS)) + x[:, 0:GLA_DV]


def _gla(zc, zd, s0, layer, acc, layer_out, p, *, bb, tc, chunk, t_valid, y_dtype):
    b, t, _ = zc.shape
    qk = GLA_HEADS * GLA_DK
    state_in = pl.BlockSpec((None, bb, qk, GLA_DV), lambda i, j: (layer, i, 0, 0))
    state_out = pl.BlockSpec((None, bb, qk, GLA_DV), lambda i, j: (layer_out, i, 0, 0))
    args = (zc, zc, zc, zd, s0, p["w_up"], p["b"], p["norm"], acc)
    return pl.pallas_call(
        functools.partial(_gla_body, bb=bb, tc=tc, chunk=chunk, t_valid=t_valid, masked=t_valid < t),
        grid=(b // bb, t // tc),
        in_specs=[
            pl.BlockSpec((bb, tc, qk), lambda i, j: (i, j, 0)),
            pl.BlockSpec((bb, tc, qk), lambda i, j: (i, j, 1)),
            pl.BlockSpec((bb, tc, W_MIX), lambda i, j: (i, j, 1)),
            pl.BlockSpec((bb, tc, LANES), lambda i, j: (i, j, ZD_TAIL // LANES)),
            state_in,
            pl.BlockSpec((LANES, qk), lambda i, j: (0, 0)),
            pl.BlockSpec((1, qk), lambda i, j: (0, 0)),
            pl.BlockSpec((1, W_MIX), lambda i, j: (0, 0)),
            pl.BlockSpec(memory_space=pl.ANY),
        ],
        out_specs=[pl.BlockSpec((bb, tc, W_MIX), lambda i, j: (i, j, 0)), state_out],
        out_shape=[jax.ShapeDtypeStruct((b, t, W_MIX), y_dtype), jax.ShapeDtypeStruct(acc.shape, f32)],
        input_output_aliases={len(args) - 1: 1},
        scratch_shapes=[pltpu.VMEM((bb, qk, W_MIX), f32)],
        compiler_params=_params("parallel", "arbitrary"),
        name="gla",
    )(*args)


def _rwkv_body(u_ref, sh0_ref, s0_ref, mu_ref, w0_ref, w2_ref, a0_ref, a2_ref, kk_ref, ka_ref, rk_ref,
               lw_ref, lb_ref, acc_ref, y_ref, sn_ref, st_ref, up_ref, *, bb, tc, chunk, t_valid, masked):
    del acc_ref
    j = pl.program_id(1)

    @pl.when(j == 0)
    def _():
        for i in range(bb):
            for h in range(RWKV_HEADS):
                st_ref[i, h] = s0_ref[i, h].T
        up_ref[...] = sh0_ref[...]

    n = bb * tc
    u = u_ref[...].reshape(n, ZD_W)
    rolled = pltpu.roll(u, 1, 0)
    first = lax.broadcasted_iota(jnp.int32, (SUBLANES, 1), 0) == 0
    parts = []
    for i in range(bb):
        parts.append(jnp.where(first, up_ref[i], rolled[i * tc:i * tc + SUBLANES]))
        if tc > SUBLANES:
            parts.append(rolled[i * tc + SUBLANES:(i + 1) * tc])
        up_ref[i] = u[(i + 1) * tc - 1:(i + 1) * tc, :]
    u_prev = _cat(parts, 0)
    us = u + (u_prev - u) * mu_ref[...]
    r = us[:, 0:W_MIX]
    k = us[:, W_MIX:2 * W_MIX]
    v = us[:, 2 * W_MIX:3 * W_MIX]
    tail = us[:, ZD_TAIL:ZD_TAIL + LANES]
    z = w0_ref[...] + _mm(jnp.tanh(tail), w2_ref[...], passes=3)
    lw = -math.exp(-0.5) * jax.nn.sigmoid(z)
    a = jax.nn.sigmoid(a0_ref[...] + _mm(tail, a2_ref[...], passes=3))
    ones = _head_ones(W_MIX, RWKV_N)
    kk = k * kk_ref[...]
    kk = kk * lax.rsqrt(jnp.maximum(_mm_exact_rhs(kk * kk, ones), RWKV_KK_EPS ** 2))
    k = k * (1.0 + (a - 1.0) * ka_ref[...])
    aa, bl, kl, vl = -kk, kk * a, k, v
    if masked:
        live = lax.broadcasted_iota(jnp.int32, (n, 1), 0) % tc + j * tc < t_valid
        lw, aa, bl, kl, vl = (jnp.where(live, x, 0.0) for x in (lw, aa, bl, kl, vl))

    cum = _seg_cumsum(lw, chunk)
    gi = jnp.exp(-cum)
    a_t = aa * jnp.exp(cum - lw)
    b_t = bl * gi
    k_t = kl * gi
    r_t = r * jnp.exp(cum)
    nch = tc // chunk
    r2 = lax.broadcasted_iota(jnp.int32, (tc, tc), 0)
    c2 = lax.broadcasted_iota(jnp.int32, (tc, tc), 1)
    same = r2 // chunk == c2 // chunk
    strict = same & (r2 > c2)
    incl = same & (r2 >= c2)

    chains = [(i, h) for i in range(bb) for h in range(RWKV_HEADS)]
    nq = len(chains)
    sl = lambda x, i, h: x[i * tc:(i + 1) * tc, h * RWKV_N:(h + 1) * RWKV_N]
    ah = [sl(a_t, i, h) for i, h in chains]
    rh = [sl(r_t, i, h) for i, h in chains]
    vh = [sl(vl, i, h) for i, h in chains]
    pair = [_mm(jnp.concatenate([ah[q], rh[q]], axis=0),
                jnp.concatenate([sl(b_t, i, h), sl(k_t, i, h)], axis=0), NT) for q, (i, h) in enumerate(chains)]
    pw = [jnp.where(strict, m[0:tc, 0:tc], 0.0) for m in pair]
    m_rb = [jnp.where(incl, m[tc:2 * tc, 0:tc], 0.0) for m in pair]
    akrk = [jnp.concatenate([jnp.where(strict, m[0:tc, tc:2 * tc], 0.0),
                             jnp.where(incl, m[tc:2 * tc, tc:2 * tc], 0.0)], axis=0) for m in pair]
    both_v = [_mm(akrk[q], vh[q]) for q in range(nq)]
    sol = [jnp.concatenate([ah[q], both_v[q][0:tc]], axis=1) for q in range(nq)]
    y_v = [x[tc:2 * tc] for x in both_v]
    wide = sol[0].shape[1]
    d = 1
    while True:
        last = 2 * d >= chunk
        res = [_mm(pw[q], sol[q] if last else jnp.concatenate([sol[q], pw[q]], axis=1)) for q in range(nq)]
        sol = [sol[q] + res[q][:, 0:wide] for q in range(nq)]
        if last:
            break
        pw = [x[:, wide:wide + tc] for x in res]
        d *= 2
    lhs_c, p2_c, mix_c, yv_c, kv_c, gcol_c = [], [], [], [], [], []
    for c in range(nch):
        rs = slice(c * chunk, (c + 1) * chunk)
        ce = [sl(cum, i, h)[(c + 1) * chunk - 1:(c + 1) * chunk, :] for i, h in chains]
        dec = [jnp.exp(ce[q] - sl(cum, i, h)[rs]) for q, (i, h) in enumerate(chains)]
        lhs_c.append([jnp.concatenate([sol[q][rs, 0:RWKV_N], rh[q][rs]], axis=0) for q in range(nq)])
        p2_c.append([sol[q][rs, RWKV_N:2 * RWKV_N] for q in range(nq)])
        mix_c.append([jnp.concatenate([m_rb[q][rs, rs], (sl(bl, i, h)[rs] * dec[q]).T], axis=0)
                      for q, (i, h) in enumerate(chains)])
        yv_c.append([y_v[q][rs] for q in range(nq)])
        kv_c.append([_mm(sl(kl, i, h)[rs] * dec[q], vh[q][rs], TN) for q, (i, h) in enumerate(chains)])
        gcol_c.append([jnp.broadcast_to(jnp.exp(ce[q]), (RWKV_N, RWKV_N)).T for q in range(nq)])
    st = [st_ref[i, h] for i, h in chains]
    y_parts = [[] for _ in chains]
    for c in range(nch):
        both = [_mm(lhs_c[c][q], st[q]) for q in range(nq)]
        u_c = [both[q][0:chunk] + p2_c[c][q] for q in range(nq)]
        mix = [_mm(mix_c[c][q], u_c[q]) for q in range(nq)]
        st = [gcol_c[c][q] * st[q] + kv_c[c][q] + mix[q][chunk:chunk + RWKV_N] for q in range(nq)]
        for q in range(nq):
            y_parts[q].append(both[q][chunk:2 * chunk] + mix[q][0:chunk] + yv_c[c][q])
    for q, (i, h) in enumerate(chains):
        st_ref[i, h] = st[q]
    y = _cat([_cat([_cat(y_parts[i * RWKV_HEADS + h], 0) for h in range(RWKV_HEADS)], 1) for i in range(bb)], 0)
    mean = _mm_exact_rhs(y, ones) / RWKV_N
    yc = y - mean
    var = _mm_exact_rhs(yc * yc, ones) / RWKV_N
    out = yc * lax.rsqrt(var + RWKV_LN_EPS) * lw_ref[...] + lb_ref[...]
    bonus = _mm_exact_rhs(r * k * rk_ref[...], ones)
    y_ref[...] = (out + bonus * v).reshape(bb, tc, W_MIX).astype(y_ref.dtype)

    @pl.when(j == pl.num_programs(1) - 1)
    def _():
        for i in range(bb):
            for h in range(RWKV_HEADS):
                sn_ref[i, h] = st_ref[i, h].T


def _rwkv(zd, shift0, s0, layer, acc, layer_out, p, *, bb, tc, chunk, t_valid, y_dtype):
    b, t, _ = zd.shape
    vec = lambda n: pl.BlockSpec((1, n), lambda i, j: (0, 0))
    lora = pl.BlockSpec((LANES, W_MIX), lambda i, j: (0, 0))
    state_in = pl.BlockSpec((None, bb, RWKV_HEADS, RWKV_N, RWKV_N), lambda i, j: (layer, i, 0, 0, 0))
    state_out = pl.BlockSpec((None, bb, RWKV_HEADS, RWKV_N, RWKV_N), lambda i, j: (layer_out, i, 0, 0, 0))
    args = (zd, shift0, s0, p["mu"], p["w0"], p["w2"], p["a0"], p["a2"], p["k_k"], p["k_a"], p["r_k"],
            p["ln_w"], p["ln_b"], acc)
    return pl.pallas_call(
        functools.partial(_rwkv_body, bb=bb, tc=tc, chunk=chunk, t_valid=t_valid, masked=t_valid < t),
        grid=(b // bb, t // tc),
        in_specs=[
            pl.BlockSpec((bb, tc, ZD_W), lambda i, j: (i, j, 0)),
            pl.BlockSpec((bb, 1, ZD_W), lambda i, j: (i, 0, 0)),
            state_in,
            vec(ZD_W), vec(W_MIX), lora, vec(W_MIX), lora, vec(W_MIX), vec(W_MIX), vec(W_MIX),
            vec(W_MIX), vec(W_MIX),
            pl.BlockSpec(memory_space=pl.ANY),
        ],
        out_specs=[pl.BlockSpec((bb, tc, W_MIX), lambda i, j: (i, j, 0)), state_out],
        out_shape=[jax.ShapeDtypeStruct((b, t, W_MIX), y_dtype), jax.ShapeDtypeStruct(acc.shape, f32)],
        input_output_aliases={len(args) - 1: 1},
        scratch_shapes=[pltpu.VMEM((bb, RWKV_HEADS, RWKV_N, RWKV_N), f32), pltpu.VMEM((bb, 1, ZD_W), f32)],
        compiler_params=_params("parallel", "arbitrary"),
        name="rwkv7",
    )(*args)


def _block_diag(w):
    n, c, _ = w.shape
    out = jnp.zeros((n * c, n * c), w.dtype)
    for i in range(n):
        out = out.at[i * c:(i + 1) * c, i * c:(i + 1) * c].set(w[i])
    return out


def _rel_bucket_ids(tq):
    dist = WINDOW + np.arange(tq)[:, None] - np.arange(WINDOW + tq)[None, :]
    max_exact = REL_BUCKETS // 2
    n = np.maximum(dist, 0)
    log_ratio = (np.log(np.maximum(n, 1).astype(np.float32) / np.float32(max_exact))
                 / np.float32(math.log(REL_MAX_DIST / max_exact)))
    large = np.minimum(max_exact + (log_ratio * np.float32(REL_BUCKETS - max_exact)).astype(np.int32),
                       REL_BUCKETS - 1)
    bucket = np.where(n < max_exact, n, large)
    return np.where((dist >= 0) & (dist <= WINDOW), bucket, -1).astype(np.int32)


def _swa_bias(table, tq):
    ids = jnp.asarray(_rel_bucket_ids(tq))[None]
    out = jnp.full((SWA_HEADS, tq, WINDOW + tq), -jnp.inf, f32)
    for k in range(REL_BUCKETS):
        out = jnp.where(ids == k, table[k].astype(f32)[:, None, None], out)
    return out.reshape(SWA_HEADS * tq, WINDOW + tq)


def _pack_layer(i, w_in, lru_conv_w, lru_conv_b, lru_wa, lru_ba, lru_wx, lru_bx, lru_lambda, lru_norm,
                swa_sinks, swa_norm, gla_w_up, gla_b, gla_norm, rwkv_mu, rwkv_w0, rwkv_w2, rwkv_a0, rwkv_a2,
                rwkv_k_k, rwkv_k_a, rwkv_r_k, rwkv_ln_w, rwkv_ln_b, w_out, ple_proj, ple_gate,
                norm_pre, norm_post):
    w = w_in[i]
    c_ca = 5 * W_MIX
    c_ud = c_ca + GLA_RANK
    c_gate = c_ud + RWKV_SHIFT
    pad = jnp.zeros((D_MODEL, ZD_W - RWKV_SHIFT - GLA_RANK), w.dtype)
    w_packed = jnp.concatenate([w[:, :c_ca], w[:, c_ud:c_gate], w[:, c_ca:c_ud], pad, w[:, c_gate:]], axis=1)
    row = lambda a: a.reshape(1, -1)
    lora_lo = RWKV_SHIFT - 2 * RWKV_LORA - ZD_TAIL
    ca_lo = RWKV_SHIFT - ZD_TAIL
    zeros_tail = jnp.zeros((LANES, W_MIX), f32)
    return dict(
        w_in=w_packed.astype(bf16),
        norm_pre=row(norm_pre[i]), norm_post=row(norm_post[i]),
        w_out=w_out[i].astype(bf16), ple_proj=ple_proj[i].astype(bf16), ple_gate=ple_gate[i].astype(bf16),
        lru=dict(conv_w=lru_conv_w[i], conv_b=row(lru_conv_b[i]), wa=_block_diag(lru_wa[i]), ba=row(lru_ba[i]),
                 wx=_block_diag(lru_wx[i]), bx=row(lru_bx[i]), lam=row(lru_lambda[i]), norm=row(lru_norm[i])),
        swa_sinks=row(swa_sinks[i]), swa_norm=row(swa_norm[i]),
        gla=dict(w_up=jnp.zeros((LANES, GLA_HEADS * GLA_DK), f32).at[ca_lo:ca_lo + GLA_RANK].set(gla_w_up[i]),
                 b=row(gla_b[i]), norm=row(jnp.tile(gla_norm[i], GLA_HEADS))),
        rwkv=dict(mu=row(jnp.pad(rwkv_mu[i], (0, ZD_W - RWKV_SHIFT))), w0=row(rwkv_w0[i]),
                  w2=zeros_tail.at[lora_lo:lora_lo + RWKV_LORA].set(rwkv_w2[i]), a0=row(rwkv_a0[i]),
                  a2=zeros_tail.at[lora_lo + RWKV_LORA:lora_lo + 2 * RWKV_LORA].set(rwkv_a2[i]),
                  k_k=row(rwkv_k_k[i]), k_a=row(rwkv_k_a[i]), r_k=row(rwkv_r_k[i]),
                  ln_w=row(rwkv_ln_w[i]), ln_b=row(rwkv_ln_b[i])),
    )


PROMPT_TILES = dict(lru=(8, 256), swa=(16, WINDOW), gla=(8, 128, 32), rwkv=(4, 128, 64))


def _sample_tiles(t_pad):
    return dict(lru=(32, t_pad), swa=(16, t_pad), gla=(16, t_pad, t_pad), rwkv=(16, t_pad, t_pad))


def _layer(x, pe, layer, lp, bias, state, acc, *, tiles, t_valid):
    b, t, _ = x.shape
    prompt = state is None
    rows = b * t
    act = bf16 if prompt else f32
    za, zb, zc, zd, zg = _inproj(x.reshape(rows, D_MODEL), lp["norm_pre"], lp["w_in"], act)
    za = za.reshape(b, t, ZA_W)
    zb = zb.reshape(b, t, ZB_W)
    zc = zc.reshape(b, t, ZC_W)
    zd = zd.reshape(b, t, ZD_W)
    if prompt:
        h0 = jnp.zeros((b, 1, W_MIX), f32)
        conv0 = jnp.zeros((b, SUBLANES, W_MIX), f32)
        kbuf = vbuf = zb[None]
        sg0 = jnp.zeros((1, b, GLA_HEADS * GLA_DK, GLA_DV), f32)
        sr0 = jnp.zeros((1, b, RWKV_HEADS, RWKV_N, RWKV_N), f32)
        sh0 = jnp.zeros((b, 1, ZD_W), f32)
        st_layer = 0
    else:
        h0, conv0, kbuf, vbuf, sg0, sr0, sh0 = state
        h0 = h0[layer].reshape(b, 1, W_MIX)
        conv0 = jnp.pad(conv0[layer], ((0, 0), (SUBLANES - (LRU_CONV - 1), 0), (0, 0)))
        sh0 = jnp.pad(sh0[layer], ((0, 0), (0, ZD_W - RWKV_SHIFT))).reshape(b, 1, ZD_W)
        st_layer = layer
    bb, tc = tiles["lru"]
    ya, h_new = _lru(za, conv0, h0, lp["lru"], bb=bb, tc=tc, t_valid=t_valid, reset_first=prompt, y_dtype=act)
    bb, tq = tiles["swa"]
    yb = _swa(zb, kbuf, vbuf, st_layer, bias, lp["swa_sinks"], lp["swa_norm"], bb=bb, tq=tq, own_prev=prompt,
              y_dtype=act)
    bb, tc, chunk = tiles["gla"]
    yc, acc_g = _gla(zc, zd, sg0, st_layer, acc[0], layer, lp["gla"], bb=bb, tc=tc, chunk=chunk, t_valid=t_valid,
                     y_dtype=act)
    bb, tc, chunk = tiles["rwkv"]
    yd, acc_r = _rwkv(zd, sh0, sr0, st_layer, acc[1], layer, lp["rwkv"], bb=bb, tc=tc, chunk=chunk, t_valid=t_valid,
                      y_dtype=act)
    flat = lambda a: a.reshape(rows, a.shape[-1])
    x_new = _outproj(flat(ya), flat(yb), flat(yc), flat(yd), zg, flat(x), pe.reshape(pe.shape[0], rows, D_PLE), layer,
                     lp["w_out"], lp["norm_post"], lp["ple_proj"], lp["ple_gate"]).reshape(b, t, D_MODEL)
    kv_rows = slice(t_valid - WINDOW, t_valid) if prompt else slice(0, t_valid)
    kw = SWA_KV_HEADS * SWA_HD
    k_new = zb[:, kv_rows, W_MIX:W_MIX + kw].reshape(b, -1, SWA_KV_HEADS, SWA_HD)
    v_new = zb[:, kv_rows, W_MIX + kw:W_MIX + 2 * kw].reshape(b, -1, SWA_KV_HEADS, SWA_HD)
    new_state = (h_new.reshape(b, W_MIX), za[:, t_valid - (LRU_CONV - 1):t_valid], k_new, v_new,
                 zd[:, t_valid - 1, :RWKV_SHIFT])
    return x_new, new_state, (acc_g, acc_r)


def kernel(x_prompt, x_sample, state_lru_h, state_lru_conv, cache_swa_k, cache_swa_v, state_gla, state_rwkv, state_rwkv_shift, p_prompt, p_sample, rel_bias, norm_pre, norm_post, w_in, lru_conv_w, lru_conv_b, lru_wa, lru_ba, lru_wx, lru_bx, lru_lambda, lru_norm, swa_sinks, swa_norm, gla_w_up, gla_b, gla_norm, rwkv_mu, rwkv_w0, rwkv_w2, rwkv_a0, rwkv_a2, rwkv_k_k, rwkv_k_a, rwkv_r_k, rwkv_ln_w, rwkv_ln_b, w_out, ple_proj, ple_gate):
    depth = w_in.shape[0]
    t_p = x_prompt.shape[1]
    t_s = x_sample.shape[1]
    t_s_pad = -(-t_s // SUBLANES) * SUBLANES
    pad_t = lambda a, axis: jnp.pad(a, [(0, t_s_pad - t_s) if d == axis else (0, 0) for d in range(a.ndim)])
    xs = pad_t(x_sample, 1)
    ps = pad_t(p_sample, 2)
    xp = x_prompt
    bias_p = _swa_bias(rel_bias, WINDOW)
    bias_s = _swa_bias(rel_bias, t_s_pad)
    tiles_s = _sample_tiles(t_s_pad)
    n_l, b_s = cache_swa_k.shape[:2]
    kw = SWA_KV_HEADS * SWA_HD
    state_s = (state_lru_h, state_lru_conv, cache_swa_k.reshape(n_l, b_s, WINDOW, kw),
               cache_swa_v.reshape(n_l, b_s, WINDOW, kw),
               state_gla.reshape(n_l, b_s, GLA_HEADS * GLA_DK, GLA_DV), state_rwkv, state_rwkv_shift)
    acc_for = lambda b: (jnp.zeros((depth, b, GLA_HEADS * GLA_DK, GLA_DV), f32),
                         jnp.zeros((depth, b, RWKV_HEADS, RWKV_N, RWKV_N), f32))
    acc_p, acc_s = acc_for(x_prompt.shape[0]), acc_for(b_s)
    sts_p, sts_s = [], []
    for i in range(depth):
        lp = _pack_layer(i, w_in, lru_conv_w, lru_conv_b, lru_wa, lru_ba, lru_wx, lru_bx, lru_lambda, lru_norm,
                         swa_sinks, swa_norm, gla_w_up, gla_b, gla_norm, rwkv_mu, rwkv_w0, rwkv_w2, rwkv_a0,
                         rwkv_a2, rwkv_k_k, rwkv_k_a, rwkv_r_k, rwkv_ln_w, rwkv_ln_b, w_out, ple_proj, ple_gate,
                         norm_pre, norm_post)
        xp, st_p, acc_p = _layer(xp, p_prompt, i, lp, bias_p, None, acc_p, tiles=PROMPT_TILES, t_valid=t_p)
        xs, st_s, acc_s = _layer(xs, ps, i, lp, bias_s, state_s, acc_s, tiles=tiles_s, t_valid=t_s)
        sts_p.append(st_p)
        sts_s.append(st_s)
    hp, cp, kp, vp, shp = [jnp.stack([s[j] for s in sts_p]) for j in range(5)]
    hs, cs, ks, vs, shs = [jnp.stack([s[j] for s in sts_s]) for j in range(5)]
    unpack = lambda a: a.reshape(a.shape[:2] + (GLA_HEADS, GLA_DK, GLA_DV))
    gp, rp = unpack(acc_p[0]), acc_p[1]
    gs, rs = unpack(acc_s[0]), acc_s[1]
    return (xp, xs[:, :t_s], hp, hs, cp, cs, kp, ks, vp, vs, gp, gs, rp, rs, shp, shs)
```

```python
import functools
import math

import jax
import jax.numpy as jnp
import numpy as np
from jax import lax
from jax.experimental import pallas as pl
from jax.experimental.pallas import tpu as pltpu

f32 = jnp.float32
bf16 = jnp.bfloat16

D_MODEL = 1024
D_PLE = 256
RMS_EPS = 1e-6
W_MIX = 256
LRU_BLOCKS = 4
LRU_CONV = 4
LRU_C = 8.0
SWA_HEADS = 4
SWA_KV_HEADS = 2
SWA_HD = 64
WINDOW = 128
REL_BUCKETS = 32
REL_MAX_DIST = 128
GLA_HEADS = 4
GLA_DK = 32
GLA_DV = 64
GLA_RANK = 16
GLA_TAU = 16.0
RWKV_HEADS = 4
RWKV_N = 64
RWKV_LORA = 32
RWKV_SHIFT = 3 * W_MIX + 2 * RWKV_LORA
RWKV_LN_EPS = 64e-5
RWKV_KK_EPS = 1e-12

ZA_W = 256
ZB_W = 512
ZC_W = 512
ZD_W = 896
ZG_W = 1024
ZD_TAIL = 768
N_IN_PACKED = ZA_W + ZB_W + ZC_W + ZD_W + ZG_W

LANES = 128
SUBLANES = 8
VMEM_LIMIT = 48 * 1024 * 1024

ROW_TILE = 1024
ROW_SPLIT = 4

NN = (((1,), (0,)), ((), ()))
NT = (((1,), (1,)), ((), ()))
TN = (((0,), (0,)), ((), ()))


def _dot(a, b, dn):
    return lax.dot_general(a, b, dn, preferred_element_type=f32)


def _split2(x):
    hi = x.astype(bf16)
    lo = (x - hi.astype(f32)).astype(bf16)
    return hi, lo


def _mm(a, b, dn=NN, passes=1):
    if passes == 1:
        return _dot(a.astype(bf16), b.astype(bf16), dn)
    ah, al = _split2(a)
    bh, bl = _split2(b)
    return _dot(ah, bh, dn) + (_dot(ah, bl, dn) + _dot(al, bh, dn))


def _mm_exact_rhs(a, b_bf16):
    hi, lo = _split2(a)
    return _dot(hi, b_bf16, NN) + _dot(lo, b_bf16, NN)


def _head_ones(n, width):
    r = lax.broadcasted_iota(jnp.int32, (n, n), 0) // width
    c = lax.broadcasted_iota(jnp.int32, (n, n), 1) // width
    return jnp.where(r == c, 1.0, 0.0).astype(bf16)


def _rms(x, g):
    return x * lax.rsqrt(jnp.mean(x * x, axis=-1, keepdims=True) + RMS_EPS) * g


def _seg_cumsum(x, seg):
    n, w = x.shape
    x3 = x.reshape(n // SUBLANES, SUBLANES, w)
    sub = lax.broadcasted_iota(jnp.int32, (1, SUBLANES, w), 1)
    d = 1
    while d < SUBLANES:
        x3 = x3 + jnp.where(sub >= d, pltpu.roll(x3, d, 1), 0.0)
        d *= 2
    x = x3.reshape(n, w)
    rows = lax.broadcasted_iota(jnp.int32, (n, 1), 0) % seg
    tot = jnp.broadcast_to(x3[:, SUBLANES - 1:SUBLANES, :], x3.shape).reshape(n, w)
    while d < seg:
        add = jnp.where(rows >= d, pltpu.roll(tot, d, 0), 0.0)
        x = x + add
        tot = tot + add
        d *= 2
    return x


def _cat(xs, axis):
    return xs[0] if len(xs) == 1 else jnp.concatenate(xs, axis=axis)


def _params(*sem):
    return pltpu.CompilerParams(dimension_semantics=sem, vmem_limit_bytes=VMEM_LIMIT)


def _inproj_body(x_ref, g_ref, w_ref, za_ref, zb_ref, zc_ref, zd_ref, zg_ref):
    h = _rms(x_ref[...], g_ref[...]).astype(bf16)
    off = 0
    for o_ref in (za_ref, zb_ref, zc_ref, zd_ref, zg_ref):
        n = o_ref.shape[-1]
        o_ref[...] = _dot(h, w_ref[:, off:off + n], NN).astype(o_ref.dtype)
        off += n


def _inproj(x, g, w, gate_dtype):
    rows = x.shape[0]
    tm = min(ROW_TILE, rows)
    widths = (ZA_W, ZB_W, ZC_W, ZD_W, ZG_W)
    return pl.pallas_call(
        _inproj_body,
        grid=(rows // tm,),
        in_specs=[
            pl.BlockSpec((tm, D_MODEL), lambda i: (i, 0)),
            pl.BlockSpec((1, D_MODEL), lambda i: (0, 0)),
            pl.BlockSpec((D_MODEL, N_IN_PACKED), lambda i: (0, 0)),
        ],
        out_specs=[pl.BlockSpec((tm, n), lambda i: (i, 0)) for n in widths],
        out_shape=[jax.ShapeDtypeStruct((rows, n), gate_dtype if n == ZG_W else f32) for n in widths],
        compiler_params=_params("parallel"),
        name="in_proj",
    )(x, g, w)


def _outproj_body(ya_ref, yb_ref, yc_ref, yd_ref, zg_ref, x_ref, pe_ref, wo_ref, gp_ref, wp_ref, wg_ref, o_ref):
    tm = x_ref.shape[0]
    parts = [slice(r, r + tm // ROW_SPLIT) for r in range(0, tm, tm // ROW_SPLIT)]
    accs = []
    for rs in parts:
        g = zg_ref[rs, :].astype(f32)
        sg = g * jax.nn.sigmoid(g)
        acc = None
        for i, y_ref in enumerate((ya_ref, yb_ref, yc_ref, yd_ref)):
            lo = i * W_MIX
            y = (y_ref[rs, :].astype(f32) * sg[:, lo:lo + W_MIX]).astype(bf16)
            part = _dot(y, wo_ref[lo:lo + W_MIX, :], NN)
            acc = part if acc is None else acc + part
        accs.append(acc)
    x1 = [x_ref[rs, :] + _rms(acc, gp_ref[...]) for rs, acc in zip(parts, accs)]
    emb = [_dot(pe_ref[rs, :].astype(bf16), wp_ref[...], NN) for rs in parts]
    gate = [jax.nn.sigmoid(_dot(x.astype(bf16), wg_ref[...], NN)) for x in x1]
    for rs, x, e, gt in zip(parts, x1, emb, gate):
        o_ref[rs, :] = x + e * gt


def _outproj(ya, yb, yc, yd, zg, x, pe, layer, wo, gp, wp, wg):
    rows = x.shape[0]
    tm = min(ROW_TILE, rows)
    row = lambda n: pl.BlockSpec((tm, n), lambda i: (i, 0))
    full = lambda a: pl.BlockSpec(a.shape, lambda i: (0, 0))
    return pl.pallas_call(
        _outproj_body,
        grid=(rows // tm,),
        in_specs=[row(W_MIX), row(W_MIX), row(W_MIX), row(W_MIX), row(ZG_W), row(D_MODEL),
                  pl.BlockSpec((None, tm, D_PLE), lambda i: (layer, i, 0)),
                  full(wo), full(gp), full(wp), full(wg)],
        out_specs=row(D_MODEL),
        out_shape=jax.ShapeDtypeStruct((rows, D_MODEL), f32),
        compiler_params=_params("parallel"),
        name="out_proj",
    )(ya, yb, yc, yd, zg, x, pe, wo, gp, wp, wg)


def _lru_body(x_ref, c0_ref, h0_ref, cw_ref, cb_ref, wa_ref, ba_ref, wx_ref, bx_ref, lam_ref, gn_ref,
              y_ref, hl_ref, ext_ref, hc_ref, *, bb, tc, t_valid, reset_first):
    j = pl.program_id(1)

    @pl.when(j == 0)
    def _():
        ext_ref[...] = c0_ref[...]
        hc_ref[...] = h0_ref[...]

    n = bb * tc
    ng = tc // SUBLANES
    sub = lax.broadcasted_iota(jnp.int32, (1, SUBLANES, W_MIX), 1)
    x3 = x_ref[...].reshape(bb * ng, SUBLANES, W_MIX)
    xc = cw_ref[3:4, :] * x3 + cb_ref[...]
    for k in range(1, LRU_CONV):
        rot = pltpu.roll(x3, k, 1)
        before = _cat([_cat([pltpu.roll(ext_ref[i], k, 0)[None]] + ([rot[i * ng:(i + 1) * ng - 1]] if ng > 1 else []), 0)
                       for i in range(bb)], 0)
        xc = xc + cw_ref[LRU_CONV - 1 - k:LRU_CONV - k, :] * jnp.where(sub >= k, rot, before)
    for i in range(bb):
        ext_ref[i] = x3[(i + 1) * ng - 1]
    xc = xc.reshape(n, W_MIX)
    gate_r = _mm(xc, wa_ref[...]) + ba_ref[...]
    gate_i = _mm(xc, wx_ref[...]) + bx_ref[...]
    log_a = -LRU_C * jax.nn.sigmoid(gate_r) * jax.nn.softplus(-lam_ref[...])
    a = jnp.exp(log_a)
    gap = 1.0 - a * a
    mult = jnp.where(gap > 0.0, gap * lax.rsqrt(gap), 0.0)
    if reset_first:
        rows = lax.broadcasted_iota(jnp.int32, (n, W_MIX), 0)
        mult = jnp.where(rows % tc + j * tc == 0, 1.0, mult)
    b = mult * jax.nn.sigmoid(gate_i) * xc
    a = a.reshape(bb * ng, SUBLANES, W_MIX)
    b = b.reshape(bb * ng, SUBLANES, W_MIX)
    d = 1
    while d < SUBLANES:
        m = sub >= d
        b = a * jnp.where(m, pltpu.roll(b, d, 1), 0.0) + b
        a = a * jnp.where(m, pltpu.roll(a, d, 1), 1.0)
        d *= 2
    hs = []
    for i in range(bb):
        carry = jnp.broadcast_to(hc_ref[i], (SUBLANES, W_MIX))
        for g in range(ng):
            hg = a[i * ng + g] * carry + b[i * ng + g]
            hs.append(hg)
            carry = jnp.broadcast_to(hg[SUBLANES - 1:SUBLANES], (SUBLANES, W_MIX))
        hc_ref[i] = carry[0:1]
    h = _cat(hs, 0)
    y_ref[...] = _rms(h, gn_ref[...]).reshape(bb, tc, W_MIX).astype(y_ref.dtype)
    r_last = (t_valid - 1) % tc

    @pl.when(j == (t_valid - 1) // tc)
    def _():
        for i in range(bb):
            hl_ref[i] = h[i * tc + r_last:i * tc + r_last + 1, :]


def _lru(za, conv0, h0, p, *, bb, tc, t_valid, reset_first, y_dtype):
    b, t, _ = za.shape
    vec = pl.BlockSpec((1, W_MIX), lambda i, j: (0, 0))
    mat = pl.BlockSpec((W_MIX, W_MIX), lambda i, j: (0, 0))
    return pl.pallas_call(
        functools.partial(_lru_body, bb=bb, tc=tc, t_valid=t_valid, reset_first=reset_first),
        grid=(b // bb, t // tc),
        in_specs=[
            pl.BlockSpec((bb, tc, W_MIX), lambda i, j: (i, j, 0)),
            pl.BlockSpec((bb, SUBLANES, W_MIX), lambda i, j: (i, 0, 0)),
            pl.BlockSpec((bb, 1, W_MIX), lambda i, j: (i, 0, 0)),
            pl.BlockSpec((LRU_CONV, W_MIX), lambda i, j: (0, 0)),
            vec, mat, vec, mat, vec, vec, vec,
        ],
        out_specs=[
            pl.BlockSpec((bb, tc, W_MIX), lambda i, j: (i, j, 0)),
            pl.BlockSpec((bb, 1, W_MIX), lambda i, j: (i, 0, 0)),
        ],
        out_shape=[jax.ShapeDtypeStruct((b, t, W_MIX), y_dtype), jax.ShapeDtypeStruct((b, 1, W_MIX), f32)],
        scratch_shapes=[pltpu.VMEM((bb, SUBLANES, W_MIX), f32), pltpu.VMEM((bb, 1, W_MIX), f32)],
        compiler_params=_params("parallel", "arbitrary"),
        name="rglru",
    )(za, conv0, h0, p["conv_w"], p["conv_b"], p["wa"], p["ba"], p["wx"], p["bx"], p["lam"], p["norm"])


def _swa_body(q_ref, ko_ref, vo_ref, kp_ref, vp_ref, bias_ref, sink_ref, gn_ref, y_ref, *, bb, tq, mask_first):
    n = pl.program_id(1)
    nk = WINDOW + tq
    kw = SWA_KV_HEADS * SWA_HD
    lo = lax.broadcasted_iota(jnp.int32, (1, kw), 1) < SWA_HD
    row_head = lax.broadcasted_iota(jnp.int32, (SWA_HEADS * tq, 1), 0) // tq
    sink = jnp.zeros((SWA_HEADS * tq, 1), f32)
    for h in range(SWA_HEADS):
        sink = jnp.where(row_head == h, sink_ref[:, h:h + 1], sink)
    bias = bias_ref[...]
    if mask_first:
        col = lax.broadcasted_iota(jnp.int32, (1, nk), 1)
        bias = jnp.where((n > 0) | (col >= WINDOW), bias, -jnp.inf)
    swap = lambda x: pltpu.roll(x, SWA_HD, 1)
    qs = []
    for i in range(bb):
        q = q_ref[i] * (SWA_HD ** -0.5)
        q01, q23 = q[:, 0:kw], q[:, kw:2 * kw]
        qs.append(jnp.concatenate([jnp.where(lo, q01, 0.0), jnp.where(lo, swap(q01), 0.0),
                                   jnp.where(lo, 0.0, swap(q23)), jnp.where(lo, 0.0, q23)], axis=0))
    s = [_mm(qs[i], jnp.concatenate([kp_ref[i], ko_ref[i]], axis=0), NT) + bias for i in range(bb)]
    m = [jnp.maximum(jnp.max(x, axis=-1, keepdims=True), sink) for x in s]
    e = [jnp.exp(x - mx) for x, mx in zip(s, m)]
    den = [jnp.sum(x, axis=-1, keepdims=True) + jnp.exp(sink - mx) for x, mx in zip(e, m)]
    pv = [_mm(e[i], jnp.concatenate([vp_ref[i], vo_ref[i]], axis=0)) / den[i] for i in range(bb)]
    for i in range(bb):
        p = pv[i]
        o01 = jnp.where(lo, p[0:tq], swap(p[tq:2 * tq]))
        o23 = jnp.where(lo, swap(p[2 * tq:3 * tq]), p[3 * tq:4 * tq])
        y_ref[i] = _rms(jnp.concatenate([o01, o23], axis=1), gn_ref[...]).astype(y_ref.dtype)


def _swa(zb, k_prev, v_prev, layer, bias, sinks, gn, *, bb, tq, own_prev, y_dtype):
    b, t, _ = zb.shape
    kw = SWA_KV_HEADS * SWA_HD
    if own_prev:
        kp_spec = pl.BlockSpec((None, bb, WINDOW, kw), lambda i, n: (0, i, jnp.maximum(n - 1, 0), 2))
        vp_spec = pl.BlockSpec((None, bb, WINDOW, kw), lambda i, n: (0, i, jnp.maximum(n - 1, 0), 3))
    else:
        kp_spec = pl.BlockSpec((None, bb, WINDOW, kw), lambda i, n: (layer, i, 0, 0))
        vp_spec = pl.BlockSpec((None, bb, WINDOW, kw), lambda i, n: (layer, i, 0, 0))
    return pl.pallas_call(
        functools.partial(_swa_body, bb=bb, tq=tq, mask_first=own_prev),
        grid=(b // bb, t // tq),
        in_specs=[
            pl.BlockSpec((bb, tq, W_MIX), lambda i, n: (i, n, 0)),
            pl.BlockSpec((bb, tq, kw), lambda i, n: (i, n, 2)),
            pl.BlockSpec((bb, tq, kw), lambda i, n: (i, n, 3)),
            kp_spec, vp_spec,
            pl.BlockSpec(bias.shape, lambda i, n: (0, 0)),
            pl.BlockSpec((1, SWA_HEADS), lambda i, n: (0, 0)),
            pl.BlockSpec((1, W_MIX), lambda i, n: (0, 0)),
        ],
        out_specs=pl.BlockSpec((bb, tq, W_MIX), lambda i, n: (i, n, 0)),
        out_shape=jax.ShapeDtypeStruct((b, t, W_MIX), y_dtype),
        compiler_params=_params("parallel", "arbitrary"),
        name="swa",
    )(zb, zb, zb, k_prev, v_prev, bias, sinks, gn)


def _col_bcast(row):
    n = row.shape[-1]
    return jnp.broadcast_to(row, (n, n)).T


def _gla_body(q_ref, k_ref, v_ref, t_ref, s0_ref, wup_ref, gb_ref, gn_ref, acc_ref, y_ref, sn_ref, st_ref,
              *, bb, tc, chunk, t_valid, masked):
    del acc_ref
    j = pl.program_id(1)
    n = bb * tc
    qk = GLA_HEADS * GLA_DK
    sr = lax.broadcasted_iota(jnp.int32, (qk, W_MIX), 0) // GLA_DK
    sc = lax.broadcasted_iota(jnp.int32, (qk, W_MIX), 1) // GLA_DV
    diag = sr == sc

    @pl.when(j == 0)
    def _():
        for i in range(bb):
            x = s0_ref[i]
            st_ref[i] = jnp.where(diag, jnp.concatenate([x] * GLA_HEADS, axis=1), 0.0)

    la = jax.nn.log_sigmoid(_mm(t_ref[...].reshape(n, LANES), wup_ref[...], passes=3) + gb_ref[...]) / GLA_TAU
    k = k_ref[...].reshape(n, qk)
    v = v_ref[...].reshape(n, W_MIX)
    if masked:
        live = lax.broadcasted_iota(jnp.int32, (n, 1), 0) % tc + j * tc < t_valid
        la, k, v = (jnp.where(live, x, 0.0) for x in (la, k, v))
    bc = _seg_cumsum(la, chunk)
    q = q_ref[...].reshape(n, qk) * (GLA_DK ** -0.5)
    q_dec = q * jnp.exp(bc)
    mid = _cat([jnp.broadcast_to(bc[r + chunk // 2 - 1:r + chunk // 2, :], (chunk, qk)) for r in range(0, n, chunk)], 0)
    q_mid = q * jnp.exp(bc - mid)
    k_mid = k * jnp.exp(mid - bc)
    khead = lax.broadcasted_iota(jnp.int32, (1, qk), 1) // GLA_DK
    vhead = lax.broadcasted_iota(jnp.int32, (1, W_MIX), 1) // GLA_DV
    r2 = lax.broadcasted_iota(jnp.int32, (GLA_HEADS * tc, tc), 0) % tc
    c2 = lax.broadcasted_iota(jnp.int32, (GLA_HEADS * tc, tc), 1)
    causal = (r2 // chunk == c2 // chunk) & (r2 >= c2)
    nch = tc // chunk
    ts = [slice(i * tc, (i + 1) * tc) for i in range(bb)]
    qs = [jnp.concatenate([jnp.where(khead == h, q_mid[t], 0.0) for h in range(GLA_HEADS)], axis=0) for t in ts]
    att = [jnp.where(causal, _mm(qs[i], k_mid[ts[i]], NT), 0.0) for i in range(bb)]
    full = [_mm(att[i], v[ts[i]]) for i in range(bb)]
    kv = []
    for c in range(nch):
        kv_c = []
        for i in range(bb):
            rs = slice(i * tc + c * chunk, i * tc + (c + 1) * chunk)
            b_end = bc[i * tc + (c + 1) * chunk - 1:i * tc + (c + 1) * chunk, :]
            k_end = k[rs] * jnp.exp(b_end - bc[rs])
            dec = _col_bcast(jnp.exp(b_end))
            kv_c.append((jnp.concatenate([dec] * (W_MIX // qk), axis=1),
                         jnp.where(diag, _mm(k_end, v[rs], TN), 0.0)))
        kv.append(kv_c)
    st = [st_ref[i] for i in range(bb)]
    inter = [[] for _ in range(bb)]
    for c in range(nch):
        for i in range(bb):
            inter[i].append(_mm(q_dec[i * tc + c * chunk:i * tc + (c + 1) * chunk], st[i]))
            st[i] = st[i] * kv[c][i][0] + kv[c][i][1]
    outs = []
    for i in range(bb):
        st_ref[i] = st[i]
        o = jnp.where(vhead == 0, full[i][0:tc], 0.0)
        for h in range(1, GLA_HEADS):
            o = o + jnp.where(vhead == h, full[i][h * tc:(h + 1) * tc], 0.0)
        outs.append(o + _cat(inter[i], 0))
    o = _cat(outs, 0)
    ms = _mm_exact_rhs(o * o, _head_ones(W_MIX, GLA_DV)) / GLA_DV
    y_ref[...] = (o * lax.rsqrt(ms + RMS_EPS) * gn_ref[...]).reshape(bb, tc, W_MIX).astype(y_ref.dtype)

    @pl.when(j == pl.num_programs(1) - 1)
    def _():
        for i in range(bb):
            x = st_ref[i]
            sn_ref[i] = sum(x[:, h * GLA_DV:(h + 1) * GLA_DV] for h in range(1, GLA_HEADS)) + x[:, 0:GLA_DV]


def _gla(zc, zd, s0, layer, acc, layer_out, p, *, bb, tc, chunk, t_valid, y_dtype):
    b, t, _ = zc.shape
    qk = GLA_HEADS * GLA_DK
    state_in = pl.BlockSpec((None, bb, qk, GLA_DV), lambda i, j: (layer, i, 0, 0))
    state_out = pl.BlockSpec((None, bb, qk, GLA_DV), lambda i, j: (layer_out, i, 0, 0))
    args = (zc, zc, zc, zd, s0, p["w_up"], p["b"], p["norm"], acc)
    return pl.pallas_call(
        functools.partial(_gla_body, bb=bb, tc=tc, chunk=chunk, t_valid=t_valid, masked=t_valid < t),
        grid=(b // bb, t // tc),
        in_specs=[
            pl.BlockSpec((bb, tc, qk), lambda i, j: (i, j, 0)),
            pl.BlockSpec((bb, tc, qk), lambda i, j: (i, j, 1)),
            pl.BlockSpec((bb, tc, W_MIX), lambda i, j: (i, j, 1)),
            pl.BlockSpec((bb, tc, LANES), lambda i, j: (i, j, ZD_TAIL // LANES)),
            state_in,
            pl.BlockSpec((LANES, qk), lambda i, j: (0, 0)),
            pl.BlockSpec((1, qk), lambda i, j: (0, 0)),
            pl.BlockSpec((1, W_MIX), lambda i, j: (0, 0)),
            pl.BlockSpec(memory_space=pl.ANY),
        ],
        out_specs=[pl.BlockSpec((bb, tc, W_MIX), lambda i, j: (i, j, 0)), state_out],
        out_shape=[jax.ShapeDtypeStruct((b, t, W_MIX), y_dtype), jax.ShapeDtypeStruct(acc.shape, f32)],
        input_output_aliases={len(args) - 1: 1},
        scratch_shapes=[pltpu.VMEM((bb, qk, W_MIX), f32)],
        compiler_params=_params("parallel", "arbitrary"),
        name="gla",
    )(*args)


def _rwkv_body(u_ref, sh0_ref, s0_ref, mu_ref, w0_ref, w2_ref, a0_ref, a2_ref, kk_ref, ka_ref, rk_ref,
               lw_ref, lb_ref, acc_ref, y_ref, sn_ref, st_ref, up_ref, *, bb, tc, chunk, t_valid, masked):
    del acc_ref
    j = pl.program_id(1)

    @pl.when(j == 0)
    def _():
        for i in range(bb):
            for h in range(RWKV_HEADS):
                st_ref[i, h] = s0_ref[i, h].T
        up_ref[...] = sh0_ref[...]

    n = bb * tc
    u = u_ref[...].reshape(n, ZD_W)
    rolled = pltpu.roll(u, 1, 0)
    first = lax.broadcasted_iota(jnp.int32, (SUBLANES, 1), 0) == 0
    parts = []
    for i in range(bb):
        parts.append(jnp.where(first, up_ref[i], rolled[i * tc:i * tc + SUBLANES]))
        if tc > SUBLANES:
            parts.append(rolled[i * tc + SUBLANES:(i + 1) * tc])
        up_ref[i] = u[(i + 1) * tc - 1:(i + 1) * tc, :]
    u_prev = _cat(parts, 0)
    us = u + (u_prev - u) * mu_ref[...]
    r = us[:, 0:W_MIX]
    k = us[:, W_MIX:2 * W_MIX]
    v = us[:, 2 * W_MIX:3 * W_MIX]
    tail = us[:, ZD_TAIL:ZD_TAIL + LANES]
    z = w0_ref[...] + _mm(jnp.tanh(tail), w2_ref[...], passes=3)
    lw = -math.exp(-0.5) * jax.nn.sigmoid(z)
    a = jax.nn.sigmoid(a0_ref[...] + _mm(tail, a2_ref[...], passes=3))
    ones = _head_ones(W_MIX, RWKV_N)
    kk = k * kk_ref[...]
    kk = kk * lax.rsqrt(jnp.maximum(_mm_exact_rhs(kk * kk, ones), RWKV_KK_EPS ** 2))
    k = k * (1.0 + (a - 1.0) * ka_ref[...])
    aa, bl, kl, vl = -kk, kk * a, k, v
    if masked:
        live = lax.broadcasted_iota(jnp.int32, (n, 1), 0) % tc + j * tc < t_valid
        lw, aa, bl, kl, vl = (jnp.where(live, x, 0.0) for x in (lw, aa, bl, kl, vl))

    cum = _seg_cumsum(lw, chunk)
    gi = jnp.exp(-cum)
    a_t = aa * jnp.exp(cum - lw)
    b_t = bl * gi
    k_t = kl * gi
    r_t = r * jnp.exp(cum)
    nch = tc // chunk
    r2 = lax.broadcasted_iota(jnp.int32, (tc, tc), 0)
    c2 = lax.broadcasted_iota(jnp.int32, (tc, tc), 1)
    same = r2 // chunk == c2 // chunk
    strict = same & (r2 > c2)
    incl = same & (r2 >= c2)

    chains = [(i, h) for i in range(bb) for h in range(RWKV_HEADS)]
    nq = len(chains)
    sl = lambda x, i, h: x[i * tc:(i + 1) * tc, h * RWKV_N:(h + 1) * RWKV_N]
    ah = [sl(a_t, i, h) for i, h in chains]
    rh = [sl(r_t, i, h) for i, h in chains]
    vh = [sl(vl, i, h) for i, h in chains]
    pair = [_mm(jnp.concatenate([ah[q], rh[q]], axis=0),
                jnp.concatenate([sl(b_t, i, h), sl(k_t, i, h)], axis=0), NT) for q, (i, h) in enumerate(chains)]
    pw = [jnp.where(strict, m[0:tc, 0:tc], 0.0) for m in pair]
    m_rb = [jnp.where(incl, m[tc:2 * tc, 0:tc], 0.0) for m in pair]
    akrk = [jnp.concatenate([jnp.where(strict, m[0:tc, tc:2 * tc], 0.0),
                             jnp.where(incl, m[tc:2 * tc, tc:2 * tc], 0.0)], axis=0) for m in pair]
    both_v = [_mm(akrk[q], vh[q]) for q in range(nq)]
    sol = [jnp.concatenate([ah[q], both_v[q][0:tc]], axis=1) for q in range(nq)]
    y_v = [x[tc:2 * tc] for x in both_v]
    wide = sol[0].shape[1]
    d = 1
    while True:
        last = 2 * d >= chunk
        res = [_mm(pw[q], sol[q] if last else jnp.concatenate([sol[q], pw[q]], axis=1)) for q in range(nq)]
        sol = [sol[q] + res[q][:, 0:wide] for q in range(nq)]
        if last:
            break
        pw = [x[:, wide:wide + tc] for x in res]
        d *= 2
    lhs_c, p2_c, mix_c, yv_c, kv_c, gcol_c = [], [], [], [], [], []
    for c in range(nch):
        rs = slice(c * chunk, (c + 1) * chunk)
        ce = [sl(cum, i, h)[(c + 1) * chunk - 1:(c + 1) * chunk, :] for i, h in chains]
        dec = [jnp.exp(ce[q] - sl(cum, i, h)[rs]) for q, (i, h) in enumerate(chains)]
        lhs_c.append([jnp.concatenate([sol[q][rs, 0:RWKV_N], rh[q][rs]], axis=0) for q in range(nq)])
        p2_c.append([sol[q][rs, RWKV_N:2 * RWKV_N] for q in range(nq)])
        mix_c.append([jnp.concatenate([m_rb[q][rs, rs], (sl(bl, i, h)[rs] * dec[q]).T], axis=0)
                      for q, (i, h) in enumerate(chains)])
        yv_c.append([y_v[q][rs] for q in range(nq)])
        kv_c.append([_mm(sl(kl, i, h)[rs] * dec[q], vh[q][rs], TN) for q, (i, h) in enumerate(chains)])
        gcol_c.append([jnp.broadcast_to(jnp.exp(ce[q]), (RWKV_N, RWKV_N)).T for q in range(nq)])
    st = [st_ref[i, h] for i, h in chains]
    y_parts = [[] for _ in chains]
    for c in range(nch):
        both = [_mm(lhs_c[c][q], st[q]) for q in range(nq)]
        u_c = [both[q][0:chunk] + p2_c[c][q] for q in range(nq)]
        mix = [_mm(mix_c[c][q], u_c[q]) for q in range(nq)]
        st = [gcol_c[c][q] * st[q] + kv_c[c][q] + mix[q][chunk:chunk + RWKV_N] for q in range(nq)]
        for q in range(nq):
            y_parts[q].append(both[q][chunk:2 * chunk] + mix[q][0:chunk] + yv_c[c][q])
    for q, (i, h) in enumerate(chains):
        st_ref[i, h] = st[q]
    y = _cat([_cat([_cat(y_parts[i * RWKV_HEADS + h], 0) for h in range(RWKV_HEADS)], 1) for i in range(bb)], 0)
    mean = _mm_exact_rhs(y, ones) / RWKV_N
    yc = y - mean
    var = _mm_exact_rhs(yc * yc, ones) / RWKV_N
    out = yc * lax.rsqrt(var + RWKV_LN_EPS) * lw_ref[...] + lb_ref[...]
    bonus = _mm_exact_rhs(r * k * rk_ref[...], ones)
    y_ref[...] = (out + bonus * v).reshape(bb, tc, W_MIX).astype(y_ref.dtype)

    @pl.when(j == pl.num_programs(1) - 1)
    def _():
        for i in range(bb):
            for h in range(RWKV_HEADS):
                sn_ref[i, h] = st_ref[i, h].T


def _rwkv(zd, shift0, s0, layer, acc, layer_out, p, *, bb, tc, chunk, t_valid, y_dtype):
    b, t, _ = zd.shape
    vec = lambda n: pl.BlockSpec((1, n), lambda i, j: (0, 0))
    lora = pl.BlockSpec((LANES, W_MIX), lambda i, j: (0, 0))
    state_in = pl.BlockSpec((None, bb, RWKV_HEADS, RWKV_N, RWKV_N), lambda i, j: (layer, i, 0, 0, 0))
    state_out = pl.BlockSpec((None, bb, RWKV_HEADS, RWKV_N, RWKV_N), lambda i, j: (layer_out, i, 0, 0, 0))
    args = (zd, shift0, s0, p["mu"], p["w0"], p["w2"], p["a0"], p["a2"], p["k_k"], p["k_a"], p["r_k"],
            p["ln_w"], p["ln_b"], acc)
    return pl.pallas_call(
        functools.partial(_rwkv_body, bb=bb, tc=tc, chunk=chunk, t_valid=t_valid, masked=t_valid < t),
        grid=(b // bb, t // tc),
        in_specs=[
            pl.BlockSpec((bb, tc, ZD_W), lambda i, j: (i, j, 0)),
            pl.BlockSpec((bb, 1, ZD_W), lambda i, j: (i, 0, 0)),
            state_in,
            vec(ZD_W), vec(W_MIX), lora, vec(W_MIX), lora, vec(W_MIX), vec(W_MIX), vec(W_MIX),
            vec(W_MIX), vec(W_MIX),
            pl.BlockSpec(memory_space=pl.ANY),
        ],
        out_specs=[pl.BlockSpec((bb, tc, W_MIX), lambda i, j: (i, j, 0)), state_out],
        out_shape=[jax.ShapeDtypeStruct((b, t, W_MIX), y_dtype), jax.ShapeDtypeStruct(acc.shape, f32)],
        input_output_aliases={len(args) - 1: 1},
        scratch_shapes=[pltpu.VMEM((bb, RWKV_HEADS, RWKV_N, RWKV_N), f32), pltpu.VMEM((bb, 1, ZD_W), f32)],
        compiler_params=_params("parallel", "arbitrary"),
        name="rwkv7",
    )(*args)


def _block_diag(w):
    n, c, _ = w.shape
    out = jnp.zeros((n * c, n * c), w.dtype)
    for i in range(n):
        out = out.at[i * c:(i + 1) * c, i * c:(i + 1) * c].set(w[i])
    return out


def _rel_bucket_ids(tq):
    dist = WINDOW + np.arange(tq)[:, None] - np.arange(WINDOW + tq)[None, :]
    max_exact = REL_BUCKETS // 2
    n = np.maximum(dist, 0)
    log_ratio = (np.log(np.maximum(n, 1).astype(np.float32) / np.float32(max_exact))
                 / np.float32(math.log(REL_MAX_DIST / max_exact)))
    large = np.minimum(max_exact + (log_ratio * np.float32(REL_BUCKETS - max_exact)).astype(np.int32),
                       REL_BUCKETS - 1)
    bucket = np.where(n < max_exact, n, large)
    return np.where((dist >= 0) & (dist <= WINDOW), bucket, -1).astype(np.int32)


def _swa_bias(table, tq):
    ids = jnp.asarray(_rel_bucket_ids(tq))[None]
    out = jnp.full((SWA_HEADS, tq, WINDOW + tq), -jnp.inf, f32)
    for k in range(REL_BUCKETS):
        out = jnp.where(ids == k, table[k].astype(f32)[:, None, None], out)
    return out.reshape(SWA_HEADS * tq, WINDOW + tq)


def _pack_layer(i, w_in, lru_conv_w, lru_conv_b, lru_wa, lru_ba, lru_wx, lru_bx, lru_lambda, lru_norm,
                swa_sinks, swa_norm, gla_w_up, gla_b, gla_norm, rwkv_mu, rwkv_w0, rwkv_w2, rwkv_a0, rwkv_a2,
                rwkv_k_k, rwkv_k_a, rwkv_r_k, rwkv_ln_w, rwkv_ln_b, w_out, ple_proj, ple_gate,
                norm_pre, norm_post):
    w = w_in[i]
    c_ca = 5 * W_MIX
    c_ud = c_ca + GLA_RANK
    c_gate = c_ud + RWKV_SHIFT
    pad = jnp.zeros((D_MODEL, ZD_W - RWKV_SHIFT - GLA_RANK), w.dtype)
    w_packed = jnp.concatenate([w[:, :c_ca], w[:, c_ud:c_gate], w[:, c_ca:c_ud], pad, w[:, c_gate:]], axis=1)
    row = lambda a: a.reshape(1, -1)
    lora_lo = RWKV_SHIFT - 2 * RWKV_LORA - ZD_TAIL
    ca_lo = RWKV_SHIFT - ZD_TAIL
    zeros_tail = jnp.zeros((LANES, W_MIX), f32)
    return dict(
        w_in=w_packed.astype(bf16),
        norm_pre=row(norm_pre[i]), norm_post=row(norm_post[i]),
        w_out=w_out[i].astype(bf16), ple_proj=ple_proj[i].astype(bf16), ple_gate=ple_gate[i].astype(bf16),
        lru=dict(conv_w=lru_conv_w[i], conv_b=row(lru_conv_b[i]), wa=_block_diag(lru_wa[i]), ba=row(lru_ba[i]),
                 wx=_block_diag(lru_wx[i]), bx=row(lru_bx[i]), lam=row(lru_lambda[i]), norm=row(lru_norm[i])),
        swa_sinks=row(swa_sinks[i]), swa_norm=row(swa_norm[i]),
        gla=dict(w_up=jnp.zeros((LANES, GLA_HEADS * GLA_DK), f32).at[ca_lo:ca_lo + GLA_RANK].set(gla_w_up[i]),
                 b=row(gla_b[i]), norm=row(jnp.tile(gla_norm[i], GLA_HEADS))),
        rwkv=dict(mu=row(jnp.pad(rwkv_mu[i], (0, ZD_W - RWKV_SHIFT))), w0=row(rwkv_w0[i]),
                  w2=zeros_tail.at[lora_lo:lora_lo + RWKV_LORA].set(rwkv_w2[i]), a0=row(rwkv_a0[i]),
                  a2=zeros_tail.at[lora_lo + RWKV_LORA:lora_lo + 2 * RWKV_LORA].set(rwkv_a2[i]),
                  k_k=row(rwkv_k_k[i]), k_a=row(rwkv_k_a[i]), r_k=row(rwkv_r_k[i]),
                  ln_w=row(rwkv_ln_w[i]), ln_b=row(rwkv_ln_b[i])),
    )


PROMPT_TILES = dict(lru=(8, 256), swa=(16, WINDOW), gla=(8, 128, 64), rwkv=(4, 128, 64))


def _sample_tiles(t_pad):
    return dict(lru=(32, t_pad), swa=(16, t_pad), gla=(16, t_pad, t_pad), rwkv=(16, t_pad, t_pad))


def _layer(x, pe, layer, lp, bias, state, acc, *, tiles, t_valid):
    b, t, _ = x.shape
    prompt = state is None
    rows = b * t
    act = bf16 if prompt else f32
    za, zb, zc, zd, zg = _inproj(x.reshape(rows, D_MODEL), lp["norm_pre"], lp["w_in"], act)
    za = za.reshape(b, t, ZA_W)
    zb = zb.reshape(b, t, ZB_W)
    zc = zc.reshape(b, t, ZC_W)
    zd = zd.reshape(b, t, ZD_W)
    if prompt:
        h0 = jnp.zeros((b, 1, W_MIX), f32)
        conv0 = jnp.zeros((b, SUBLANES, W_MIX), f32)
        kbuf = vbuf = zb[None]
        sg0 = jnp.zeros((1, b, GLA_HEADS * GLA_DK, GLA_DV), f32)
        sr0 = jnp.zeros((1, b, RWKV_HEADS, RWKV_N, RWKV_N), f32)
        sh0 = jnp.zeros((b, 1, ZD_W), f32)
        st_layer = 0
    else:
        h0, conv0, kbuf, vbuf, sg0, sr0, sh0 = state
        h0 = h0[layer].reshape(b, 1, W_MIX)
        conv0 = jnp.pad(conv0[layer], ((0, 0), (SUBLANES - (LRU_CONV - 1), 0), (0, 0)))
        sh0 = jnp.pad(sh0[layer], ((0, 0), (0, ZD_W - RWKV_SHIFT))).reshape(b, 1, ZD_W)
        st_layer = layer
    bb, tc = tiles["lru"]
    ya, h_new = _lru(za, conv0, h0, lp["lru"], bb=bb, tc=tc, t_valid=t_valid, reset_first=prompt, y_dtype=act)
    bb, tq = tiles["swa"]
    yb = _swa(zb, kbuf, vbuf, st_layer, bias, lp["swa_sinks"], lp["swa_norm"], bb=bb, tq=tq, own_prev=prompt,
              y_dtype=act)
    bb, tc, chunk = tiles["gla"]
    yc, acc_g = _gla(zc, zd, sg0, st_layer, acc[0], layer, lp["gla"], bb=bb, tc=tc, chunk=chunk, t_valid=t_valid,
                     y_dtype=act)
    bb, tc, chunk = tiles["rwkv"]
    yd, acc_r = _rwkv(zd, sh0, sr0, st_layer, acc[1], layer, lp["rwkv"], bb=bb, tc=tc, chunk=chunk, t_valid=t_valid,
                      y_dtype=act)
    flat = lambda a: a.reshape(rows, a.shape[-1])
    x_new = _outproj(flat(ya), flat(yb), flat(yc), flat(yd), zg, flat(x), pe.reshape(pe.shape[0], rows, D_PLE), layer,
                     lp["w_out"], lp["norm_post"], lp["ple_proj"], lp["ple_gate"]).reshape(b, t, D_MODEL)
    kv_rows = slice(t_valid - WINDOW, t_valid) if prompt else slice(0, t_valid)
    kw = SWA_KV_HEADS * SWA_HD
    k_new = zb[:, kv_rows, W_MIX:W_MIX + kw].reshape(b, -1, SWA_KV_HEADS, SWA_HD)
    v_new = zb[:, kv_rows, W_MIX + kw:W_MIX + 2 * kw].reshape(b, -1, SWA_KV_HEADS, SWA_HD)
    new_state = (h_new.reshape(b, W_MIX), za[:, t_valid - (LRU_CONV - 1):t_valid], k_new, v_new,
                 zd[:, t_valid - 1, :RWKV_SHIFT])
    return x_new, new_state, (acc_g, acc_r)


def kernel(x_prompt, x_sample, state_lru_h, state_lru_conv, cache_swa_k, cache_swa_v, state_gla, state_rwkv, state_rwkv_shift, p_prompt, p_sample, rel_bias, norm_pre, norm_post, w_in, lru_conv_w, lru_conv_b, lru_wa, lru_ba, lru_wx, lru_bx, lru_lambda, lru_norm, swa_sinks, swa_norm, gla_w_up, gla_b, gla_norm, rwkv_mu, rwkv_w0, rwkv_w2, rwkv_a0, rwkv_a2, rwkv_k_k, rwkv_k_a, rwkv_r_k, rwkv_ln_w, rwkv_ln_b, w_out, ple_proj, ple_gate):
    depth = w_in.shape[0]
    t_p = x_prompt.shape[1]
    t_s = x_sample.shape[1]
    t_s_pad = -(-t_s // SUBLANES) * SUBLANES
    pad_t = lambda a, axis: jnp.pad(a, [(0, t_s_pad - t_s) if d == axis else (0, 0) for d in range(a.ndim)])
    xs = pad_t(x_sample, 1)
    ps = pad_t(p_sample, 2)
    xp = x_prompt
    bias_p = _swa_bias(rel_bias, WINDOW)
    bias_s = _swa_bias(rel_bias, t_s_pad)
    tiles_s = _sample_tiles(t_s_pad)
    n_l, b_s = cache_swa_k.shape[:2]
    kw = SWA_KV_HEADS * SWA_HD
    state_s = (state_lru_h, state_lru_conv, cache_swa_k.reshape(n_l, b_s, WINDOW, kw),
               cache_swa_v.reshape(n_l, b_s, WINDOW, kw),
               state_gla.reshape(n_l, b_s, GLA_HEADS * GLA_DK, GLA_DV), state_rwkv, state_rwkv_shift)
    acc_for = lambda b: (jnp.zeros((depth, b, GLA_HEADS * GLA_DK, GLA_DV), f32),
                         jnp.zeros((depth, b, RWKV_HEADS, RWKV_N, RWKV_N), f32))
    acc_p, acc_s = acc_for(x_prompt.shape[0]), acc_for(b_s)
    sts_p, sts_s = [], []
    for i in range(depth):
        lp = _pack_layer(i, w_in, lru_conv_w, lru_conv_b, lru_wa, lru_ba, lru_wx, lru_bx, lru_lambda, lru_norm,
                         swa_sinks, swa_norm, gla_w_up, gla_b, gla_norm, rwkv_mu, rwkv_w0, rwkv_w2, rwkv_a0,
                         rwkv_a2, rwkv_k_k, rwkv_k_a, rwkv_r_k, rwkv_ln_w, rwkv_ln_b, w_out, ple_proj, ple_gate,
                         norm_pre, norm_post)
        xp, st_p, acc_p = _layer(xp, p_prompt, i, lp, bias_p, None, acc_p, tiles=PROMPT_TILES, t_valid=t_p)
        xs, st_s, acc_s = _layer(xs, ps, i, lp, bias_s, state_s, acc_s, tiles=tiles_s, t_valid=t_s)
        sts_p.append(st_p)
        sts_s.append(st_s)
    hp, cp, kp, vp, shp = [jnp.stack([s[j] for s in sts_p]) for j in range(5)]
    hs, cs, ks, vs, shs = [jnp.stack([s[j] for s in sts_s]) for j in range(5)]
    unpack = lambda a: a.reshape(a.shape[:2] + (GLA_HEADS, GLA_DK, GLA_DV))
    gp, rp = unpack(acc_p[0]), acc_p[1]
    gs, rs = unpack(acc_s[0]), acc_s[1]
    return (xp, xs[:, :t_s], hp, hs, cp, cs, kp, ks, vp, vs, gp, gs, rp, rs, shp, shs)
```
